```python
import math, functools
import jax, jax.numpy as jnp
from jax import lax
import numpy as np

D_MODEL = 2048
BATCH = 4
SEQ = 4096
DEPTH = 4
DEC_BATCH = 16
DEC_SEQ = 16
PAST_LEN = 2048

CHUNK = 64
Q_BLOCK = 128
DIFF_HEADS = 8
DIFF_HEAD_DIM = D_MODEL // (2 * DIFF_HEADS)
DIFF_V_DIM = 2 * DIFF_HEAD_DIM
RET_HEADS = 8
RET_KEY_DIM = D_MODEL // (2 * RET_HEADS)
RET_VAL_DIM = 2 * RET_KEY_DIM
D_FF = 4 * D_MODEL
NORM_EPS = 1e-6
ROPE_BASE = 10000.0
NEG_INF = -1e30
DIFF_W = DIFF_HEADS * 2 * DIFF_HEAD_DIM
RET_QK_W = RET_HEADS * RET_KEY_DIM
RET_V_W = RET_HEADS * RET_VAL_DIM
SPLIT_SIZES = (DIFF_W, DIFF_W, DIFF_W, RET_QK_W, RET_QK_W, RET_V_W, RET_V_W, D_MODEL, D_MODEL)
IN_COLS = sum(SPLIT_SIZES)
SPLIT_POINTS = tuple(int(p) for p in np.cumsum(SPLIT_SIZES)[:-1])

kernel_name = 'hybrid_diffattn_retention_stream_step'

f32 = jnp.float32


def rmsnorm(x, g, eps=NORM_EPS):
    xf = x.astype(f32)
    y = xf * lax.rsqrt(jnp.mean(xf * xf, axis=-1, keepdims=True) + eps)
    return (y * g.astype(f32)).astype(x.dtype)


def ret_log_gamma():
    return jnp.log1p(-jnp.exp2(-5.0 - jnp.arange(RET_HEADS, dtype=f32)))


def theta_shift(x, pos):
    inv = 1.0 / (ROPE_BASE ** jnp.linspace(0.0, 1.0, RET_KEY_DIM // 2, dtype=f32))
    ang = jnp.repeat(inv, 2)
    th = pos.astype(f32)[:, None] * ang[None, :]
    sin = jnp.sin(th)[None, :, None, :]
    cos = jnp.cos(th)[None, :, None, :]
    xf = x.astype(f32)
    rot = jnp.stack([-xf[..., 1::2], xf[..., 0::2]], axis=-1).reshape(xf.shape)
    return (xf * cos + rot * sin).astype(x.dtype)


def diff_attend(q, k, v, lam, mask):
    s = jnp.einsum('bqhmd,bkhmd->bhmqk', q.astype(f32), k.astype(f32)) * (DIFF_HEAD_DIM ** -0.5)
    if mask is not None:
        s = jnp.where(mask, s, NEG_INF)
    p = jax.nn.softmax(s, axis=-1)
    w = p[:, :, 0] - lam * p[:, :, 1]
    return jnp.einsum('bhqk,bkhv->bqhv', w, v.astype(f32))


def diff_attn_prompt(q, k, v, lam):
    B, T = q.shape[:2]
    nb = T // Q_BLOCK
    qb = q.reshape(B, nb, Q_BLOCK, DIFF_HEADS, 2, DIFF_HEAD_DIM).swapaxes(0, 1)
    k_chunk = jnp.arange(T) // CHUNK

    def one_block(args):
        q_blk, bi = args
        q_chunk = (bi * Q_BLOCK + jnp.arange(Q_BLOCK)) // CHUNK
        mask = k_chunk[None, :] <= q_chunk[:, None]
        return diff_attend(q_blk, k, v, lam, mask)

    o = lax.map(one_block, (qb, jnp.arange(nb)))
    return o.swapaxes(0, 1).reshape(B, T, DIFF_HEADS, DIFF_V_DIM)


def retention_chunk(q, k, v, state, log_gamma):
    C = q.shape[1]
    idx = jnp.arange(C, dtype=f32)
    diff = idx[:, None] - idx[None, :]
    decay = jnp.where(diff >= 0, jnp.exp(log_gamma[:, None, None] * jnp.maximum(diff, 0.0)), 0.0)
    qf, kf, vf = q.astype(f32), k.astype(f32), v.astype(f32)
    scores = jnp.einsum('bihd,bjhd->bhij', qf, kf) * decay[None]
    inner = jnp.einsum('bhij,bjhv->bihv', scores, vf)
    cross_decay = jnp.exp(log_gamma[None, :] * (idx + 1.0)[:, None])[None, :, :, None]
    cross = jnp.einsum('bihd,bhdv->bihv', qf, state) * cross_decay
    k_dec = kf * jnp.exp(log_gamma[None, :] * (C - 1.0 - idx)[:, None])[None, :, :, None]
    new_state = jnp.exp(log_gamma * C)[None, :, None, None] * state + jnp.einsum('bjhd,bjhv->bhdv', k_dec, vf)
    return inner + cross, new_state


def retention_prompt(q, k, v, log_gamma):
    B, T = q.shape[:2]
    nc = T // CHUNK

    def to_chunks(a):
        return a.reshape(B, nc, CHUNK, *a.shape[2:]).swapaxes(0, 1)

    def step(state, xs):
        o, state = retention_chunk(xs[0], xs[1], xs[2], state, log_gamma)
        return state, o

    s0 = jnp.zeros((B, RET_HEADS, RET_KEY_DIM, RET_VAL_DIM), f32)
    s_final, o = lax.scan(step, s0, (to_chunks(q), to_chunks(k), to_chunks(v)))
    return o.swapaxes(0, 1).reshape(B, T, RET_HEADS, RET_VAL_DIM), s_final


def mixer_block(x, pos, past, layer, norm_g, w_in, lq1, lk1, lq2, lk2, subln_g, ret_g, w_a, w_b, w_out):
    B, T, _ = x.shape
    h = rmsnorm(x, norm_g)
    proj = h @ w_in
    dq, dk, dv, rq, rk, rv, rg, ga, gb = jnp.split(proj, SPLIT_POINTS, axis=-1)
    dq = dq.reshape(B, T, DIFF_HEADS, 2, DIFF_HEAD_DIM)
    dk = dk.reshape(B, T, DIFF_HEADS, 2 * DIFF_HEAD_DIM)
    dv = dv.reshape(B, T, DIFF_HEADS, DIFF_V_DIM)
    lam_init = 0.8 - 0.6 * math.exp(-0.3 * layer)
    lam = (jnp.exp(jnp.sum(lq1.astype(f32) * lk1.astype(f32)))
           - jnp.exp(jnp.sum(lq2.astype(f32) * lk2.astype(f32))) + lam_init)
    rq = theta_shift(rq.reshape(B, T, RET_HEADS, RET_KEY_DIM), pos)
    rk = theta_shift(rk.reshape(B, T, RET_HEADS, RET_KEY_DIM), pos) * (RET_KEY_DIM ** -0.5)
    rv = rv.reshape(B, T, RET_HEADS, RET_VAL_DIM)
    log_gamma = ret_log_gamma()
    if past is None:
        oa = diff_attn_prompt(dq, dk.reshape(B, T, DIFF_HEADS, 2, DIFF_HEAD_DIM), dv, lam)
        ob, st = retention_prompt(rq, rk, rv, log_gamma)
    else:
        ck, cv, cs = past
        k_all = jnp.concatenate([ck.astype(dk.dtype), dk], axis=1)
        v_all = jnp.concatenate([cv.astype(dv.dtype), dv], axis=1)
        oa = diff_attend(dq, k_all.reshape(B, -1, DIFF_HEADS, 2, DIFF_HEAD_DIM), v_all, lam, None)
        ob, st = retention_chunk(rq, rk, rv, cs.astype(f32), log_gamma)
    oa = rmsnorm(oa, subln_g) * (1.0 - lam_init)
    ob = rmsnorm(ob, ret_g) * jax.nn.silu(rg.reshape(B, T, RET_HEADS, RET_VAL_DIM).astype(f32))
    oa = oa.reshape(B, T, DIFF_W).astype(x.dtype)
    ob = ob.reshape(B, T, RET_V_W).astype(x.dtype)
    merged = jax.nn.sigmoid(ga) * (oa @ w_a) + jax.nn.sigmoid(gb) * (ob @ w_b)
    return x + merged @ w_out, dk, dv, st.astype(x.dtype)


def ffn_block(x, g, w_up, w_down):
    h = rmsnorm(x, g)
    return x + jnp.square(jax.nn.relu(h @ w_up)) @ w_down


def setup_inputs(seed: int = 0) -> dict:
    key = jax.random.key(seed)
    ks = jax.random.split(key, 24)
    n = jax.random.normal
    return {
        'x_prompt': n(ks[0], (BATCH, SEQ, D_MODEL), f32),
        'x_sample': n(ks[1], (DEC_BATCH, DEC_SEQ, D_MODEL), f32),
        'cache_diff_k': n(ks[2], (DEPTH, DEC_BATCH, PAST_LEN, DIFF_HEADS, 2 * DIFF_HEAD_DIM), f32),
        'cache_diff_v': n(ks[3], (DEPTH, DEC_BATCH, PAST_LEN, DIFF_HEADS, DIFF_V_DIM), f32),
        'state_ret': n(ks[4], (DEPTH, DEC_BATCH, RET_HEADS, RET_KEY_DIM, RET_VAL_DIM), f32),
        'norm_mix_g': 1.0 + 0.02 * n(ks[5], (DEPTH, D_MODEL), f32),
        'w_in': n(ks[6], (DEPTH, D_MODEL, IN_COLS), f32) * D_MODEL ** -0.5,
        'lambda_q1': 0.1 * n(ks[7], (DEPTH, DIFF_HEAD_DIM), f32),
        'lambda_k1': 0.1 * n(ks[8], (DEPTH, DIFF_HEAD_DIM), f32),
        'lambda_q2': 0.1 * n(ks[9], (DEPTH, DIFF_HEAD_DIM), f32),
        'lambda_k2': 0.1 * n(ks[10], (DEPTH, DIFF_HEAD_DIM), f32),
        'diff_subln_g': 1.0 + 0.02 * n(ks[11], (DEPTH, DIFF_V_DIM), f32),
        'ret_norm_g': 1.0 + 0.02 * n(ks[12], (DEPTH, RET_VAL_DIM), f32),
        'w_branch_a': n(ks[13], (DEPTH, DIFF_W, D_MODEL), f32) * DIFF_W ** -0.5,
        'w_branch_b': n(ks[14], (DEPTH, RET_V_W, D_MODEL), f32) * RET_V_W ** -0.5,
        'w_out': n(ks[15], (DEPTH, D_MODEL, D_MODEL), f32) * D_MODEL ** -0.5,
        'norm_ffn_g': 1.0 + 0.02 * n(ks[16], (DEPTH, D_MODEL), f32),
        'w_up': n(ks[17], (DEPTH, D_MODEL, D_FF), f32) * D_MODEL ** -0.5,
        'w_down': n(ks[18], (DEPTH, D_FF, D_MODEL), f32) * D_FF ** -0.5,
        'final_norm_g': 1.0 + 0.02 * n(ks[19], (D_MODEL,), f32),
    }


def reference(x_prompt, x_sample, cache_diff_k, cache_diff_v, state_ret, norm_mix_g, w_in,
              lambda_q1, lambda_k1, lambda_q2, lambda_k2, diff_subln_g, ret_norm_g,
              w_branch_a, w_branch_b, w_out, norm_ffn_g, w_up, w_down, final_norm_g):
    past_len = cache_diff_k.shape[2]
    pos_p = jnp.arange(x_prompt.shape[1])
    pos_s = past_len + jnp.arange(x_sample.shape[1])
    xp, xs = x_prompt, x_sample
    kp_l, vp_l, sp_l, ks_l, vs_l, ss_l = [], [], [], [], [], []
    for l in range(DEPTH):
        mix = functools.partial(
            mixer_block, layer=l, norm_g=norm_mix_g[l], w_in=w_in[l],
            lq1=lambda_q1[l], lk1=lambda_k1[l], lq2=lambda_q2[l], lk2=lambda_k2[l],
            subln_g=diff_subln_g[l], ret_g=ret_norm_g[l],
            w_a=w_branch_a[l], w_b=w_branch_b[l], w_out=w_out[l])
        xp, kp, vp, sp = mix(xp, pos_p, None)
        xs, kk, vv, ss = mix(xs, pos_s, (cache_diff_k[l], cache_diff_v[l], state_ret[l]))
        xp = ffn_block(xp, norm_ffn_g[l], w_up[l], w_down[l])
        xs = ffn_block(xs, norm_ffn_g[l], w_up[l], w_down[l])
        kp_l.append(kp); vp_l.append(vp); sp_l.append(sp)
        ks_l.append(kk); vs_l.append(vv); ss_l.append(ss)
    y_prompt = rmsnorm(xp, final_norm_g)
    y_sample = rmsnorm(xs, final_norm_g)
    return (y_prompt, y_sample, jnp.stack(kp_l), jnp.stack(vp_l), jnp.stack(sp_l),
            jnp.stack(ks_l), jnp.stack(vs_l), jnp.stack(ss_l))
```

```python
import functools
import math

import jax
import jax.numpy as jnp
from jax import lax
from jax.experimental import pallas as pl
from jax.experimental.pallas import tpu as pltpu

F32 = jnp.float32
BF16 = jnp.bfloat16

NORM_EPS = 1e-6
ROPE_BASE = 10000.0
MASK_CHUNK = 64
NEG_INF = -1e30
LOG2E = 1.4426950408889634

V7X_VMEM_BYTES = 64 * 1024 * 1024
V7X_LANES = 128
VMEM_REQUEST_CAP = V7X_VMEM_BYTES - 8 * 1024 * 1024

NT_DIMS = (((1,), (1,)), ((), ()))
TN_DIMS = (((0,), (0,)), ((), ()))


def _pick_tile(n, pref, align):
    if n <= pref:
        return n
    t = pref - pref % align
    while t >= align:
        if n % t == 0:
            return t
        t -= align
    raise ValueError(f"no tile for {n} (pref {pref}, align {align})")


def _params(vmem_bytes, n_grid):
    return pltpu.CompilerParams(
        dimension_semantics=("arbitrary",) * n_grid,
        vmem_limit_bytes=int(min(VMEM_REQUEST_CAP, vmem_bytes)))


def _rms_rows(x, gain):
    ms = jnp.mean(x * x, axis=-1, keepdims=True)
    return x * lax.rsqrt(ms + NORM_EPS) * gain


def _proj_kernel(*refs, has_norm, emit_h, epi, row_chunk, rope_scale):
    refs = list(refs)
    x_ref = refs.pop(0)
    g_ref = refs.pop(0) if has_norm else None
    w_ref = refs.pop(0)
    if epi == "rope":
        cos_ref, sina_ref, sinb_ref = refs.pop(0), refs.pop(0), refs.pop(0)
    if epi == "residual":
        res_ref = refs.pop(0)
    o_ref = refs.pop(0)
    o16_ref = refs.pop(0) if epi == "f32_bf16" else None
    hout_ref = refs.pop(0) if emit_h else None
    h_scr = refs.pop(0) if has_norm else None

    j = pl.program_id(1)
    if has_norm:
        @pl.when(j == 0)
        def _():
            def rows(r, carry):
                rs = pl.ds(pl.multiple_of(r * row_chunk, row_chunk), row_chunk)
                hrow = _rms_rows(x_ref[rs, :], g_ref[...]).astype(BF16)
                h_scr[rs, :] = hrow
                if emit_h:
                    hout_ref[rs, :] = hrow
                return carry
            lax.fori_loop(0, x_ref.shape[0] // row_chunk, rows, 0)
        lhs = h_scr[...]
    else:
        lhs = x_ref[...]

    acc = jnp.dot(lhs, w_ref[...], preferred_element_type=F32)
    if epi == "bf16":
        o_ref[...] = acc.astype(BF16)
    elif epi == "f32_bf16":
        o_ref[...] = acc
        o16_ref[...] = acc.astype(BF16)
    elif epi == "residual":
        o_ref[...] = res_ref[...] + acc
    elif epi == "rope":
        n_tiles = pl.num_programs(1)
        scale = jnp.where(j < n_tiles // 2, 1.0, rope_scale).astype(F32)
        cos, sina, sinb = cos_ref[...], sina_ref[...], sinb_ref[...]
        for c in range(acc.shape[1] // V7X_LANES):
            cs = slice(c * V7X_LANES, (c + 1) * V7X_LANES)
            xs = acc[:, cs]
            nxt = pltpu.roll(xs, V7X_LANES - 1, 1)
            prv = pltpu.roll(xs, 1, 1)
            o_ref[:, cs] = ((xs * cos + nxt * sina + prv * sinb) * scale).astype(BF16)
    else:
        raise ValueError(epi)


def _proj(lhs, w, layer, col0, ncols, *, gain=None, emit_h=False, epi="bf16", extra=(),
          rope_scale=1.0, tm_pref=1024, tn_pref=1024):
    n, k = lhs.shape
    has_norm = gain is not None
    tm = _pick_tile(n, tm_pref, 16)
    tn = _pick_tile(ncols, tn_pref, V7X_LANES)
    assert col0 % tn == 0
    cb = col0 // tn
    grid = (n // tm, ncols // tn)

    in_specs = [pl.BlockSpec((tm, k), lambda i, j: (i, 0))]
    args = [lhs]
    if has_norm:
        in_specs.append(pl.BlockSpec((1, k), lambda i, j: (0, 0)))
        args.append(gain.reshape(1, k))
    in_specs.append(pl.BlockSpec((None, k, tn), lambda i, j: (layer, 0, cb + j)))
    args.append(w)
    if epi == "rope":
        for t in extra:
            in_specs.append(pl.BlockSpec((tm, V7X_LANES), lambda i, j: (i, 0)))
            args.append(t)
    elif epi == "residual":
        in_specs.append(pl.BlockSpec((tm, tn), lambda i, j: (i, j)))
        args.append(extra[0])

    tile_spec = pl.BlockSpec((tm, tn), lambda i, j: (i, j))
    out_dtype = F32 if epi in ("f32_bf16", "residual") else BF16
    out_shape = [jax.ShapeDtypeStruct((n, ncols), out_dtype)]
    out_specs = [tile_spec]
    out_bytes = tm * tn * jnp.dtype(out_dtype).itemsize
    if epi == "f32_bf16":
        out_shape.append(jax.ShapeDtypeStruct((n, ncols), BF16))
        out_specs.append(tile_spec)
        out_bytes += tm * tn * 2
    if emit_h:
        out_shape.append(jax.ShapeDtypeStruct((n, k), BF16))
        out_specs.append(pl.BlockSpec((tm, k), lambda i, j: (i, 0)))
        out_bytes += tm * k * 2
    scratch = [pltpu.VMEM((tm, k), BF16)] if has_norm else []

    vmem = (2 * tm * k * lhs.dtype.itemsize + 2 * k * tn * 2 + 2 * out_bytes
            + (tm * k * 2 if has_norm else 0) + 3 * tm * tn * 4
            + (2 * tm * tn * 4 if epi == "residual" else 0) + (4 << 20))
    outs = pl.pallas_call(
        functools.partial(_proj_kernel, has_norm=has_norm, emit_h=emit_h, epi=epi,
                          row_chunk=min(tm, 64), rope_scale=rope_scale),
        grid=grid, in_specs=in_specs, out_specs=out_specs, out_shape=out_shape,
        scratch_shapes=scratch, compiler_params=_params(vmem, 2),
        name=f"proj_{epi}")(*args)
    return outs[0] if len(outs) == 1 else tuple(outs)


def _lam_value(lamc_ref, lq1_ref, lk1_ref, lq2_ref, lk2_ref):
    a = jnp.exp(jnp.sum(lq1_ref[...] * lk1_ref[...], axis=-1, keepdims=True))
    b = jnp.exp(jnp.sum(lq2_ref[...] * lk2_ref[...], axis=-1, keepdims=True))
    return a - b + lamc_ref[0]


def _softmax_step(s, v, m_ref, l_ref, acc_ref, exp_scale):
    rows, keys = s.shape
    m_prev = m_ref[...]
    m_new = jnp.maximum(m_prev, jnp.max(s, axis=1, keepdims=True))
    alpha = jnp.exp2((m_prev - m_new) * exp_scale)
    p = jnp.exp2((s - pltpu.repeat(m_new, keys // V7X_LANES, 1)) * exp_scale)
    l_ref[...] = alpha * l_ref[...] + jnp.sum(p, axis=1, keepdims=True)
    pv = jnp.dot(p.astype(BF16), v, preferred_element_type=F32)
    acc_ref[...] = acc_ref[...] * pltpu.repeat(alpha, acc_ref.shape[1] // V7X_LANES, 1) + pv
    m_ref[...] = m_new


def _diff_finish(a1, l1, a2, l2, lam, gain, out_scale):
    rep = a1.shape[1] // V7X_LANES
    o = a1 * pltpu.repeat(1.0 / l1, rep, 1) - lam * (a2 * pltpu.repeat(1.0 / l2, rep, 1))
    return _rms_rows(o, gain) * out_scale


def _attn_prompt_kernel(lamc_ref, lq1_ref, lk1_ref, lq2_ref, lk2_ref, q_ref, k_ref, v_ref, g_ref,
                        o_ref, m1, l1, a1, m2, l2, a2, *, dh, exp_scale):
    qi = pl.program_id(2)
    tq = q_ref.shape[0]
    tk = tq
    for m_ref, l_ref, a_ref in ((m1, l1, a1), (m2, l2, a2)):
        m_ref[...] = jnp.full(m_ref.shape, -jnp.inf, F32)
        l_ref[...] = jnp.zeros(l_ref.shape, F32)
        a_ref[...] = jnp.zeros(a_ref.shape, F32)
    q = q_ref[...]
    maps = ((q[:, :dh], 0, m1, l1, a1), (q[:, dh:], dh, m2, l2, a2))

    def kv_block(j, mask):
        ks = pl.ds(pl.multiple_of(j * tk, tk), tk)
        k = k_ref[ks, :]
        v = v_ref[ks, :]
        for qm, lo, m_ref, l_ref, a_ref in maps:
            s = lax.dot_general(qm, k[:, lo:lo + dh], NT_DIMS, preferred_element_type=F32)
            if mask is not None:
                s = jnp.where(mask, s, NEG_INF)
            _softmax_step(s, v, m_ref, l_ref, a_ref, exp_scale)

    def full_block(j, carry):
        kv_block(j, None)
        return carry
    lax.fori_loop(0, qi, full_block, 0)

    shift = MASK_CHUNK.bit_length() - 1
    row_chunk = lax.shift_right_logical(lax.broadcasted_iota(jnp.int32, (tq, tk), 0), shift)
    col_chunk = lax.shift_right_logical(lax.broadcasted_iota(jnp.int32, (tq, tk), 1), shift)
    kv_block(qi, col_chunk <= row_chunk)

    lam = _lam_value(lamc_ref, lq1_ref, lk1_ref, lq2_ref, lk2_ref)
    o = _diff_finish(a1[...], l1[...], a2[...], l2[...], lam, g_ref[...], lamc_ref[1])
    o_ref[...] = o.astype(BF16)


def _lam_inputs(lam_init, lams):
    specs = [pl.BlockSpec(memory_space=pltpu.SMEM)]
    args = [jnp.array([lam_init, 1.0 - lam_init], F32)]
    for v in lams:
        specs.append(pl.BlockSpec((1, v.shape[-1]), lambda *_: (0, 0)))
        args.append(v.reshape(1, -1))
    return specs, args


def _attn_prompt(dq, k16, v16, gain, lam_init, lams, *, batch, seq, heads, tq_pref=512):
    n, width = dq.shape
    hd = width // heads
    dh = hd // 2
    tq = _pick_tile(seq, tq_pref, V7X_LANES)
    nq = seq // tq
    lam_specs, lam_args = _lam_inputs(lam_init, lams)
    in_specs = lam_specs + [
        pl.BlockSpec((tq, hd), lambda b, h, i: (b * nq + i, h)),
        pl.BlockSpec((seq, hd), lambda b, h, i: (b, h)),
        pl.BlockSpec((seq, hd), lambda b, h, i: (b, h)),
        pl.BlockSpec((1, hd), lambda b, h, i: (0, 0)),
    ]
    stat = pltpu.VMEM((tq, V7X_LANES), F32)
    accs = pltpu.VMEM((tq, hd), F32)
    vmem = 4 * seq * hd * 2 + 4 * tq * hd * 2 + 2 * tq * hd * 4 + 4 * tq * V7X_LANES * 4 \
        + 8 * tq * tq * 4 + (4 << 20)
    return pl.pallas_call(
        functools.partial(_attn_prompt_kernel, dh=dh, exp_scale=dh ** -0.5 * LOG2E),
        grid=(batch, heads, nq), in_specs=in_specs,
        out_specs=pl.BlockSpec((tq, hd), lambda b, h, i: (b * nq + i, h)),
        out_shape=jax.ShapeDtypeStruct((n, width), BF16),
        scratch_shapes=[stat, stat, accs, stat, stat, accs],
        compiler_params=_params(vmem, 3), name="diff_attn_prompt",
    )(*lam_args, dq, k16, v16, gain.reshape(1, hd))


def _attn_decode_kernel(lamc_ref, lq1_ref, lk1_ref, lq2_ref, lk2_ref, q_ref, kn_ref, vn_ref,
                        kc_ref, vc_ref, g_ref, o_ref, qd, m_scr, l_scr, a_scr,
                        *, heads, dh, n_new, exp_scale):
    j = pl.program_id(1)
    hd = 2 * dh
    s_q = q_ref.shape[0]

    @pl.when(j == 0)
    def _():
        qd[...] = jnp.zeros(qd.shape, BF16)
        m_scr[...] = jnp.full(m_scr.shape, -jnp.inf, F32)
        l_scr[...] = jnp.zeros(l_scr.shape, F32)
        a_scr[...] = jnp.zeros(a_scr.shape, F32)
        new_mask = lax.broadcasted_iota(jnp.int32, (2 * s_q, kn_ref.shape[0]), 1) < n_new
        for h in range(heads):
            hs = slice(h * hd, (h + 1) * hd)
            qd[h, 0:s_q, 0:dh] = q_ref[:, h * hd:h * hd + dh]
            qd[h, s_q:2 * s_q, dh:hd] = q_ref[:, h * hd + dh:(h + 1) * hd]
            s = lax.dot_general(qd[h], kn_ref[:, hs], NT_DIMS, preferred_element_type=F32)
            s = jnp.where(new_mask, s, NEG_INF)
            _softmax_step(s, vn_ref[:, hs], m_scr.at[h], l_scr.at[h], a_scr.at[h], exp_scale)

    for h in range(heads):
        hs = slice(h * hd, (h + 1) * hd)
        k = kc_ref[:, hs].astype(BF16)
        v = vc_ref[:, hs].astype(BF16)
        s = lax.dot_general(qd[h], k, NT_DIMS, preferred_element_type=F32)
        _softmax_step(s, v, m_scr.at[h], l_scr.at[h], a_scr.at[h], exp_scale)

    @pl.when(j == pl.num_programs(1) - 1)
    def _():
        lam = _lam_value(lamc_ref, lq1_ref, lk1_ref, lq2_ref, lk2_ref)
        for h in range(heads):
            a, l = a_scr[h], l_scr[h]
            o = _diff_finish(a[:s_q], l[:s_q], a[s_q:], l[s_q:], lam, g_ref[...], lamc_ref[1])
            o_ref[:, h * hd:(h + 1) * hd] = o.astype(BF16)


def _attn_decode(dq, k16, v16, cache_k, cache_v, layer, gain, lam_init, lams, *, batch, seq,
                 heads, tk_pref=512):
    n, width = dq.shape
    hd = width // heads
    dh = hd // 2
    past = cache_k.shape[2]
    tk = _pick_tile(past, tk_pref, V7X_LANES)
    pad = V7X_LANES
    kn = jnp.pad(k16.reshape(batch, seq, width), ((0, 0), (0, pad - seq), (0, 0)))
    vn = jnp.pad(v16.reshape(batch, seq, width), ((0, 0), (0, pad - seq), (0, 0)))
    lam_specs, lam_args = _lam_inputs(lam_init, lams)
    in_specs = lam_specs + [
        pl.BlockSpec((seq, width), lambda b, j: (b, 0)),
        pl.BlockSpec((None, pad, width), lambda b, j: (b, 0, 0)),
        pl.BlockSpec((None, pad, width), lambda b, j: (b, 0, 0)),
        pl.BlockSpec((None, None, tk, width), lambda b, j: (layer, b, j, 0)),
        pl.BlockSpec((None, None, tk, width), lambda b, j: (layer, b, j, 0)),
        pl.BlockSpec((1, hd), lambda b, j: (0, 0)),
    ]
    scratch = [pltpu.VMEM((heads, 2 * seq, hd), BF16),
               pltpu.VMEM((heads, 2 * seq, V7X_LANES), F32),
               pltpu.VMEM((heads, 2 * seq, V7X_LANES), F32),
               pltpu.VMEM((heads, 2 * seq, hd), F32)]
    vmem = 4 * tk * width * 4 + 2 * tk * width * 2 + 8 * pad * width * 2 + (8 << 20)
    return pl.pallas_call(
        functools.partial(_attn_decode_kernel, heads=heads, dh=dh, n_new=seq,
                          exp_scale=dh ** -0.5 * LOG2E),
        grid=(batch, past // tk), in_specs=in_specs,
        out_specs=pl.BlockSpec((seq, width), lambda b, j: (b, 0)),
        out_shape=jax.ShapeDtypeStruct((n, width), BF16),
        scratch_shapes=scratch, compiler_params=_params(vmem, 2), name="diff_attn_decode",
    )(*lam_args, dq, kn, vn, cache_k, cache_v, gain.reshape(1, hd))


def _ret_tables(heads, chunk, dk, dv):
    log_gamma = jnp.log1p(-jnp.exp2(-5.0 - jnp.arange(heads, dtype=F32)))
    idx = jnp.arange(chunk, dtype=F32)
    diff = idx[:, None] - idx[None, :]
    decay = jnp.where(diff >= 0, jnp.exp(log_gamma[:, None, None] * jnp.maximum(diff, 0.0)), 0.0)
    cross = jnp.exp(log_gamma[:, None] * (idx + 1.0)[None, :])
    kdec = jnp.exp(log_gamma[:, None] * (chunk - 1.0 - idx)[None, :])
    carry = jnp.exp(log_gamma * chunk)
    return (decay,
            jnp.broadcast_to(cross[:, :, None], (heads, chunk, dv)),
            jnp.broadcast_to(kdec[:, :, None], (heads, chunk, dk)),
            carry)


def _ret_kernel(*refs, has_init):
    refs = list(refs)
    carry_ref, q_ref, k_ref, v_ref, rg_ref, dec_ref, cross_ref, kdec_ref, g_ref = refs[:9]
    refs = refs[9:]
    s0_ref = refs.pop(0) if has_init else None
    o_ref, sout_ref, s_scr = refs

    h = pl.program_id(1)

    @pl.when(pl.program_id(2) == 0)
    def _():
        s_scr[...] = s0_ref[...] if has_init else jnp.zeros(s_scr.shape, F32)

    q, k, v = q_ref[...], k_ref[...], v_ref[...]
    state = s_scr[...]
    scores = lax.dot_general(q, k, NT_DIMS, preferred_element_type=F32) * dec_ref[...]
    inner = jnp.dot(scores.astype(BF16), v, preferred_element_type=F32)
    cross = jnp.dot(q, state.astype(BF16), preferred_element_type=F32) * cross_ref[...]
    k_dec = (k.astype(F32) * kdec_ref[...]).astype(BF16)
    new_state = carry_ref[h] * state + lax.dot_general(k_dec, v, TN_DIMS,
                                                       preferred_element_type=F32)
    s_scr[...] = new_state
    sout_ref[...] = new_state

    gate = rg_ref[...].astype(F32)
    y = _rms_rows(inner + cross, g_ref[...]) * (gate * jax.nn.sigmoid(gate))
    o_ref[...] = y.astype(BF16)


def _retention(rqk, rest, gain, state0, layer, *, batch, seq, heads, chunk_pref=256):
    n = rqk.shape[0]
    dk = rqk.shape[1] // (2 * heads)
    dv = rest.shape[1] // (4 * heads)
    chunk = _pick_tile(seq, chunk_pref, 16)
    nc = seq // chunk
    decay, cross, kdec, carry = _ret_tables(heads, chunk, dk, dv)
    has_init = state0 is not None
    row = lambda b, h, c: b * nc + c
    in_specs = [
        pl.BlockSpec(memory_space=pltpu.SMEM),
        pl.BlockSpec((chunk, dk), lambda b, h, c: (row(b, h, c), h)),
        pl.BlockSpec((chunk, dk), lambda b, h, c: (row(b, h, c), heads + h)),
        pl.BlockSpec((chunk, dv), lambda b, h, c: (row(b, h, c), h)),
        pl.BlockSpec((chunk, dv), lambda b, h, c: (row(b, h, c), heads + h)),
        pl.BlockSpec((None, chunk, chunk), lambda b, h, c: (h, 0, 0)),
        pl.BlockSpec((None, chunk, dv), lambda b, h, c: (h, 0, 0)),
        pl.BlockSpec((None, chunk, dk), lambda b, h, c: (h, 0, 0)),
        pl.BlockSpec((1, dv), lambda b, h, c: (0, 0)),
    ]
    args = [carry, rqk, rqk, rest, rest, decay, cross, kdec, gain.reshape(1, dv)]
    if has_init:
        in_specs.append(pl.BlockSpec((None, None, None, dk, dv),
                                     lambda b, h, c: (layer, b, h, 0, 0)))
        args.append(state0)
    vmem = 8 * chunk * (chunk + 2 * dv + 2 * dk) * 4 + 8 * dk * dv * 4 + (8 << 20)
    ob, st = pl.pallas_call(
        functools.partial(_ret_kernel, has_init=has_init),
        grid=(batch, heads, nc), in_specs=in_specs,
        out_specs=[pl.BlockSpec((chunk, dv), lambda b, h, c: (row(b, h, c), h)),
                   pl.BlockSpec((None, None, dk, dv), lambda b, h, c: (b, h, 0, 0))],
        out_shape=[jax.ShapeDtypeStruct((n, heads * dv), BF16),
                   jax.ShapeDtypeStruct((batch, heads, dk, dv), F32)],
        scratch_shapes=[pltpu.VMEM((dk, dv), F32)],
        compiler_params=_params(vmem, 3), name="retention",
    )(*args)
    return ob, st


def _merge_kernel(oa_ref, ob_ref, wa_ref, wb_ref, ga_ref, gb_ref, o_ref):
    a = jnp.dot(oa_ref[...], wa_ref[...], preferred_element_type=F32)
    b = jnp.dot(ob_ref[...], wb_ref[...], preferred_element_type=F32)
    ga = jax.nn.sigmoid(ga_ref[...].astype(F32))
    gb = jax.nn.sigmoid(gb_ref[...].astype(F32))
    o_ref[...] = (ga * a + gb * b).astype(BF16)


def _merge(oa, ob, rest, w_a, w_b, layer, *, tm_pref=1024, tn_pref=512):
    n, k = oa.shape
    d = w_a.shape[2]
    tm = _pick_tile(n, tm_pref, 16)
    tn = _pick_tile(d, tn_pref, V7X_LANES)
    ga0 = (rest.shape[1] // 2) // tn
    gb0 = (3 * rest.shape[1] // 4) // tn
    vmem = 4 * tm * k * 2 + 4 * k * tn * 2 + 6 * tm * tn * 2 + 4 * tm * tn * 4 + (4 << 20)
    return pl.pallas_call(
        _merge_kernel, grid=(n // tm, d // tn),
        in_specs=[pl.BlockSpec((tm, k), lambda i, j: (i, 0)),
                  pl.BlockSpec((tm, k), lambda i, j: (i, 0)),
                  pl.BlockSpec((None, k, tn), lambda i, j: (layer, 0, j)),
                  pl.BlockSpec((None, k, tn), lambda i, j: (layer, 0, j)),
                  pl.BlockSpec((tm, tn), lambda i, j: (i, ga0 + j)),
                  pl.BlockSpec((tm, tn), lambda i, j: (i, gb0 + j))],
        out_specs=pl.BlockSpec((tm, tn), lambda i, j: (i, j)),
        out_shape=jax.ShapeDtypeStruct((n, d), BF16),
        compiler_params=_params(vmem, 2), name="branch_merge",
    )(oa, ob, w_a, w_b, rest, rest)


def _ffn_kernel(*refs, row_chunk, final_norm):
    refs = list(refs)
    x_ref, g_ref, wu_ref, wd_ref = refs[:4]
    fg_ref = refs[4] if final_norm else None
    o_ref, h_scr = refs[-2:]
    f = pl.program_id(1)
    n_chunks = x_ref.shape[0] // row_chunk

    @pl.when(f == 0)
    def _():
        def rows(r, carry):
            rs = pl.ds(pl.multiple_of(r * row_chunk, row_chunk), row_chunk)
            x = x_ref[rs, :]
            h_scr[rs, :] = _rms_rows(x, g_ref[...]).astype(BF16)
            o_ref[rs, :] = x
            return carry
        lax.fori_loop(0, n_chunks, rows, 0)

    u = jnp.dot(h_scr[...], wu_ref[...], preferred_element_type=F32)
    u = jnp.square(jnp.maximum(u, 0.0)).astype(BF16)
    o_ref[...] += jnp.dot(u, wd_ref[...], preferred_element_type=F32)

    if final_norm:
        @pl.when(f == pl.num_programs(1) - 1)
        def _():
            def rows(r, carry):
                rs = pl.ds(pl.multiple_of(r * row_chunk, row_chunk), row_chunk)
                o_ref[rs, :] = _rms_rows(o_ref[rs, :], fg_ref[...])
                return carry
            lax.fori_loop(0, n_chunks, rows, 0)


def _ffn(x, gain, w_up, w_down, layer, final_gain=None, *, tm_pref=512, tf_pref=512):
    n, d = x.shape
    d_ff = w_up.shape[2]
    tm = _pick_tile(n, tm_pref, 16)
    tf = _pick_tile(d_ff, tf_pref, V7X_LANES)
    final_norm = final_gain is not None
    in_specs = [pl.BlockSpec((tm, d), lambda i, f: (i, 0)),
                pl.BlockSpec((1, d), lambda i, f: (0, 0)),
                pl.BlockSpec((None, d, tf), lambda i, f: (layer, 0, f)),
                pl.BlockSpec((None, tf, d), lambda i, f: (layer, f, 0))]
    args = [x, gain.reshape(1, d), w_up, w_down]
    if final_norm:
        in_specs.append(pl.BlockSpec((1, d), lambda i, f: (0, 0)))
        args.append(final_gain.reshape(1, d))
    vmem = 4 * tm * d * 4 + tm * d * 2 + 8 * d * tf * 2 + 2 * tm * tf * 4 + 2 * tm * d * 4 \
        + (4 << 20)
    return pl.pallas_call(
        functools.partial(_ffn_kernel, row_chunk=min(tm, 64), final_norm=final_norm),
        grid=(n // tm, d_ff // tf), in_specs=in_specs,
        out_specs=pl.BlockSpec((tm, d), lambda i, f: (i, 0)),
        out_shape=jax.ShapeDtypeStruct((n, d), F32),
        scratch_shapes=[pltpu.VMEM((tm, d), BF16)],
        compiler_params=_params(vmem, 2), name="ffn",
    )(*args)


def _rope_tables(pos, dk, batch):
    inv = 1.0 / (ROPE_BASE ** jnp.linspace(0.0, 1.0, dk // 2, dtype=F32))
    th = pos.astype(F32)[:, None] * jnp.repeat(inv, 2)[None, :]
    sin, cos = jnp.sin(th), jnp.cos(th)
    even = (jnp.arange(dk) % 2) == 0
    sin_next = jnp.where(even, -sin, 0.0)
    sin_prev = jnp.where(even, 0.0, sin)
    return tuple(jnp.tile(t, (batch, 1)) for t in (cos, sin_next, sin_prev))


def _mixer(x, layer, w, p, rope, past, *, batch, seq, heads, ret_heads):
    d = x.shape[1]
    hd = p["subln_g"].shape[-1]
    diff_w = heads * hd
    dk = p["state"].shape[-2]
    dv = p["state"].shape[-1]
    qk_w = ret_heads * dk
    c_k, c_v, c_rqk, c_rest = diff_w, 2 * diff_w, 3 * diff_w, 3 * diff_w + 2 * qk_w

    dq, h = _proj(x, w["in"], layer, 0, diff_w, gain=p["norm_mix_g"], emit_h=True)
    k32, k16 = _proj(h, w["in"], layer, c_k, diff_w, epi="f32_bf16")
    v32, v16 = _proj(h, w["in"], layer, c_v, diff_w, epi="f32_bf16")
    rqk = _proj(h, w["in"], layer, c_rqk, 2 * qk_w, epi="rope", extra=rope,
                rope_scale=dk ** -0.5, tn_pref=qk_w)
    rest = _proj(h, w["in"], layer, c_rest, 4 * ret_heads * dv)

    lam_init = 0.8 - 0.6 * math.exp(-0.3 * layer)
    lams = (p["lq1"], p["lk1"], p["lq2"], p["lk2"])
    if past:
        oa = _attn_decode(dq, k16, v16, p["cache_k"], p["cache_v"], layer, p["subln_g"],
                          lam_init, lams, batch=batch, seq=seq, heads=heads)
        ob, st = _retention(rqk, rest, p["ret_g"], p["state"], layer,
                            batch=batch, seq=seq, heads=ret_heads)
    else:
        oa = _attn_prompt(dq, k16, v16, p["subln_g"], lam_init, lams,
                          batch=batch, seq=seq, heads=heads)
        ob, st = _retention(rqk, rest, p["ret_g"], None, layer,
                            batch=batch, seq=seq, heads=ret_heads)
    merged = _merge(oa, ob, rest, w["a"], w["b"], layer)
    x = _proj(merged, w["out"], layer, 0, d, epi="residual", extra=(x,))
    return x, k32, v32, st


def kernel(x_prompt, x_sample, cache_diff_k, cache_diff_v, state_ret, norm_mix_g, w_in,
           lambda_q1, lambda_k1, lambda_q2, lambda_k2, diff_subln_g, ret_norm_g,
           w_branch_a, w_branch_b, w_out, norm_ffn_g, w_up, w_down, final_norm_g):
    depth, dec_batch, past_len, heads, hd = cache_diff_k.shape
    batch, seq, d = x_prompt.shape
    dec_seq = x_sample.shape[1]
    ret_heads, dk, dv = state_ret.shape[2:]

    w = {"in": w_in.astype(BF16), "a": w_branch_a.astype(BF16), "b": w_branch_b.astype(BF16),
         "out": w_out.astype(BF16), "up": w_up.astype(BF16), "down": w_down.astype(BF16)}
    cache_k = cache_diff_k.reshape(depth, dec_batch, past_len, heads * hd)
    cache_v = cache_diff_v.reshape(depth, dec_batch, past_len, heads * hd)
    rope_p = _rope_tables(jnp.arange(seq), dk, batch)
    rope_s = _rope_tables(past_len + jnp.arange(dec_seq), dk, dec_batch)

    xp = x_prompt.reshape(batch * seq, d)
    xs = x_sample.reshape(dec_batch * dec_seq, d)
    outs = [[] for _ in range(6)]
    for l in range(depth):
        p = {"norm_mix_g": norm_mix_g[l], "lq1": lambda_q1[l], "lk1": lambda_k1[l],
             "lq2": lambda_q2[l], "lk2": lambda_k2[l], "subln_g": diff_subln_g[l],
             "ret_g": ret_norm_g[l], "cache_k": cache_k, "cache_v": cache_v, "state": state_ret}
        final_g = final_norm_g if l == depth - 1 else None
        xp, kp, vp, sp = _mixer(xp, l, w, p, rope_p, False, batch=batch, seq=seq,
                                heads=heads, ret_heads=ret_heads)
        xs, ks, vs, ss = _mixer(xs, l, w, p, rope_s, True, batch=dec_batch, seq=dec_seq,
                                heads=heads, ret_heads=ret_heads)
        xp = _ffn(xp, norm_ffn_g[l], w["up"], w["down"], l, final_g)
        xs = _ffn(xs, norm_ffn_g[l], w["up"], w["down"], l, final_g)
        for acc, val in zip(outs, (kp.reshape(batch, seq, heads, hd),
                                   vp.reshape(batch, seq, heads, hd), sp,
                                   ks.reshape(dec_batch, dec_seq, heads, hd),
                                   vs.reshape(dec_batch, dec_seq, heads, hd), ss)):
            acc.append(val)
    return (xp.reshape(batch, seq, d), xs.reshape(dec_batch, dec_seq, d),
            *(jnp.stack(o) for o in outs))
```

```python
import functools
import math

import jax
import jax.numpy as jnp
from jax import lax
from jax.experimental import pallas as pl
from jax.experimental.pallas import tpu as pltpu

F32 = jnp.float32
BF16 = jnp.bfloat16

NORM_EPS = 1e-6
ROPE_BASE = 10000.0
MASK_CHUNK = 64
NEG_INF = -1e30
LOG2E = 1.4426950408889634

V7X_VMEM_BYTES = 64 * 1024 * 1024
V7X_LANES = 128
VMEM_REQUEST_CAP = V7X_VMEM_BYTES - 8 * 1024 * 1024

NT_DIMS = (((1,), (1,)), ((), ()))
TN_DIMS = (((0,), (0,)), ((), ()))


def _pick_tile(n, pref, align):
    if n <= pref:
        return n
    t = pref - pref % align
    while t >= align:
        if n % t == 0:
            return t
        t -= align
    raise ValueError(f"no tile for {n} (pref {pref}, align {align})")


def _params(vmem_bytes, n_grid):
    return pltpu.CompilerParams(
        dimension_semantics=("arbitrary",) * n_grid,
        vmem_limit_bytes=int(min(VMEM_REQUEST_CAP, vmem_bytes)))


def _rms_rows(x, gain):
    ms = jnp.mean(x * x, axis=-1, keepdims=True)
    return x * lax.rsqrt(ms + NORM_EPS) * gain


def _proj_kernel(*refs, has_norm, emit_h, epi, row_chunk, rope_scale):
    refs = list(refs)
    x_ref = refs.pop(0)
    g_ref = refs.pop(0) if has_norm else None
    w_ref = refs.pop(0)
    if epi == "rope":
        cos_ref, sina_ref, sinb_ref = refs.pop(0), refs.pop(0), refs.pop(0)
    if epi == "residual":
        res_ref = refs.pop(0)
    o_ref = refs.pop(0)
    hout_ref = refs.pop(0) if emit_h else None
    h_scr = refs.pop(0) if has_norm else None

    j = pl.program_id(1)
    if has_norm:
        @pl.when(j == 0)
        def _():
            def rows(r, carry):
                rs = pl.ds(pl.multiple_of(r * row_chunk, row_chunk), row_chunk)
                hrow = _rms_rows(x_ref[rs, :], g_ref[...]).astype(BF16)
                h_scr[rs, :] = hrow
                if emit_h:
                    hout_ref[rs, :] = hrow
                return carry
            lax.fori_loop(0, x_ref.shape[0] // row_chunk, rows, 0)
        lhs = h_scr[...]
    else:
        lhs = x_ref[...]

    acc = jnp.dot(lhs, w_ref[...], preferred_element_type=F32)
    if epi == "bf16":
        o_ref[...] = acc.astype(BF16)
    elif epi == "residual":
        o_ref[...] = res_ref[...] + acc
    elif epi == "rope":
        n_tiles = pl.num_programs(1)
        scale = jnp.where(j < n_tiles // 2, 1.0, rope_scale).astype(F32)
        cos, sina, sinb = cos_ref[...], sina_ref[...], sinb_ref[...]
        for c in range(acc.shape[1] // V7X_LANES):
            cs = slice(c * V7X_LANES, (c + 1) * V7X_LANES)
            xs = acc[:, cs]
            nxt = pltpu.roll(xs, V7X_LANES - 1, 1)
            prv = pltpu.roll(xs, 1, 1)
            o_ref[:, cs] = ((xs * cos + nxt * sina + prv * sinb) * scale).astype(BF16)
    else:
        raise ValueError(epi)


def _proj(lhs, w, layer, col0, ncols, *, gain=None, emit_h=False, epi="bf16", extra=(),
          rope_scale=1.0, tm_pref=1024, tn_pref=1024):
    n, k = lhs.shape
    has_norm = gain is not None
    tm = _pick_tile(n, tm_pref, 16)
    tn = _pick_tile(ncols, tn_pref, V7X_LANES)
    assert col0 % tn == 0
    cb = col0 // tn
    grid = (n // tm, ncols // tn)

    in_specs = [pl.BlockSpec((tm, k), lambda i, j: (i, 0))]
    args = [lhs]
    if has_norm:
        in_specs.append(pl.BlockSpec((1, k), lambda i, j: (0, 0)))
        args.append(gain.reshape(1, k))
    in_specs.append(pl.BlockSpec((None, k, tn), lambda i, j: (layer, 0, cb + j)))
    args.append(w)
    if epi == "rope":
        for t in extra:
            in_specs.append(pl.BlockSpec((tm, V7X_LANES), lambda i, j: (i, 0)))
            args.append(t)
    elif epi == "residual":
        in_specs.append(pl.BlockSpec((tm, tn), lambda i, j: (i, j)))
        args.append(extra[0])

    tile_spec = pl.BlockSpec((tm, tn), lambda i, j: (i, j))
    out_dtype = F32 if epi == "residual" else BF16
    out_shape = [jax.ShapeDtypeStruct((n, ncols), out_dtype)]
    out_specs = [tile_spec]
    out_bytes = tm * tn * jnp.dtype(out_dtype).itemsize
    if emit_h:
        out_shape.append(jax.ShapeDtypeStruct((n, k), BF16))
        out_specs.append(pl.BlockSpec((tm, k), lambda i, j: (i, 0)))
        out_bytes += tm * k * 2
    scratch = [pltpu.VMEM((tm, k), BF16)] if has_norm else []

    vmem = (2 * tm * k * lhs.dtype.itemsize + 2 * k * tn * 2 + 2 * out_bytes
            + (tm * k * 2 if has_norm else 0) + 3 * tm * tn * 4
            + (2 * tm * tn * 4 if epi == "residual" else 0) + (4 << 20))
    outs = pl.pallas_call(
        functools.partial(_proj_kernel, has_norm=has_norm, emit_h=emit_h, epi=epi,
                          row_chunk=min(tm, 64), rope_scale=rope_scale),
        grid=grid, in_specs=in_specs, out_specs=out_specs, out_shape=out_shape,
        scratch_shapes=scratch, compiler_params=_params(vmem, 2),
        name=f"proj_{epi}")(*args)
    return outs[0] if len(outs) == 1 else tuple(outs)


def _proj_heads_kernel(*refs, aliased):
    x_ref, w_ref = refs[0], refs[1]
    o32_ref, o16_ref = refs[-2], refs[-1]
    acc = jnp.dot(x_ref[...], w_ref[...], preferred_element_type=F32)
    o16_ref[...] = acc.astype(BF16)
    o32_ref[...] = acc.reshape(o32_ref.shape)


def _proj_heads(lhs, w, layer, col0, heads, stacked, depth, *, tm_pref=512):
    n, k = lhs.shape
    ncols = k
    hd = ncols // heads
    tm = _pick_tile(n, tm_pref, 16)
    cb = col0 // ncols
    assert col0 % ncols == 0
    aliased = stacked is not None
    in_specs = [pl.BlockSpec((tm, k), lambda i: (i, 0)),
                pl.BlockSpec((None, k, ncols), lambda i: (layer, 0, cb))]
    args = [lhs, w]
    if aliased:
        in_specs.append(pl.BlockSpec(memory_space=pl.ANY))
        args.append(stacked)
    vmem = 4 * tm * k * 2 + 4 * k * ncols * 2 + 4 * tm * ncols * 4 + 4 * tm * ncols * 2 + (4 << 20)
    return pl.pallas_call(
        functools.partial(_proj_heads_kernel, aliased=aliased),
        grid=(n // tm,), in_specs=in_specs,
        out_specs=[pl.BlockSpec((None, tm, heads, hd), lambda i: (layer, i, 0, 0)),
                   pl.BlockSpec((tm, ncols), lambda i: (i, 0))],
        out_shape=[jax.ShapeDtypeStruct((depth, n, heads, hd), F32),
                   jax.ShapeDtypeStruct((n, ncols), BF16)],
        input_output_aliases={2: 0} if aliased else {},
        compiler_params=_params(vmem, 1), name="proj_heads")(*args)


def _lam_value(lamc_ref, lq1_ref, lk1_ref, lq2_ref, lk2_ref):
    a = jnp.exp(jnp.sum(lq1_ref[...] * lk1_ref[...], axis=-1, keepdims=True))
    b = jnp.exp(jnp.sum(lq2_ref[...] * lk2_ref[...], axis=-1, keepdims=True))
    return a - b + lamc_ref[0]


def _softmax_step(s, v, m_ref, l_ref, acc_ref, exp_scale):
    rows, keys = s.shape
    m_prev = m_ref[...]
    m_new = jnp.maximum(m_prev, jnp.max(s, axis=1, keepdims=True))
    alpha = jnp.exp2((m_prev - m_new) * exp_scale)
    p = jnp.exp2((s - pltpu.repeat(m_new, keys // V7X_LANES, 1)) * exp_scale)
    l_ref[...] = alpha * l_ref[...] + jnp.sum(p, axis=1, keepdims=True)
    pv = jnp.dot(p.astype(BF16), v, preferred_element_type=F32)
    acc_ref[...] = acc_ref[...] * pltpu.repeat(alpha, acc_ref.shape[1] // V7X_LANES, 1) + pv
    m_ref[...] = m_new


def _diff_finish(a1, l1, a2, l2, lam, gain, out_scale):
    rep = a1.shape[1] // V7X_LANES
    o = a1 * pltpu.repeat(1.0 / l1, rep, 1) - lam * (a2 * pltpu.repeat(1.0 / l2, rep, 1))
    return _rms_rows(o, gain) * out_scale


def _attn_prompt_kernel(lamc_ref, lq1_ref, lk1_ref, lq2_ref, lk2_ref, q_ref, k_ref, v_ref, g_ref,
                        o_ref, m1, l1, a1, m2, l2, a2, sa1, sa2, sb1, sb2, *, dh, exp_scale):
    qi = pl.program_id(2)
    tq = q_ref.shape[0]
    tk = tq
    for m_ref, l_ref, a_ref in ((m1, l1, a1), (m2, l2, a2)):
        m_ref[...] = jnp.full(m_ref.shape, -jnp.inf, F32)
        l_ref[...] = jnp.zeros(l_ref.shape, F32)
        a_ref[...] = jnp.zeros(a_ref.shape, F32)
    maps = ((0, m1, l1, a1), (dh, m2, l2, a2))

    def rows(j):
        return pl.ds(pl.multiple_of(j * tk, tk), tk)

    def scores(j, bufs):
        for (lo, _, _, _), s_scr in zip(maps, bufs):
            s_scr[...] = lax.dot_general(q_ref[:, lo:lo + dh], k_ref[rows(j), lo:lo + dh], NT_DIMS,
                                         preferred_element_type=F32)

    def consume(j, bufs, mask):
        v = v_ref[rows(j), :]
        for (_, m_ref, l_ref, a_ref), s_scr in zip(maps, bufs):
            s = s_scr[...]
            if mask is not None:
                s = jnp.where(mask, s, NEG_INF)
            _softmax_step(s, v, m_ref, l_ref, a_ref, exp_scale)

    buf_a, buf_b = (sa1, sa2), (sb1, sb2)
    scores(0, buf_a)

    def block_pair(t, carry):
        scores(2 * t + 1, buf_b)
        consume(2 * t, buf_a, None)
        scores(2 * t + 2, buf_a)
        consume(2 * t + 1, buf_b, None)
        return carry
    lax.fori_loop(0, qi // 2, block_pair, 0)

    shift = MASK_CHUNK.bit_length() - 1
    row_chunk = lax.shift_right_logical(lax.broadcasted_iota(jnp.int32, (tq, tk), 0), shift)
    col_chunk = lax.shift_right_logical(lax.broadcasted_iota(jnp.int32, (tq, tk), 1), shift)
    mask = col_chunk <= row_chunk

    @pl.when(qi % 2 == 0)
    def _():
        consume(qi, buf_a, mask)

    @pl.when(qi % 2 == 1)
    def _():
        scores(qi, buf_b)
        consume(qi - 1, buf_a, None)
        consume(qi, buf_b, mask)

    lam = _lam_value(lamc_ref, lq1_ref, lk1_ref, lq2_ref, lk2_ref)
    o = _diff_finish(a1[...], l1[...], a2[...], l2[...], lam, g_ref[...], lamc_ref[1])
    o_ref[...] = o.astype(BF16)


def _lam_inputs(lam_init, lams):
    specs = [pl.BlockSpec(memory_space=pltpu.SMEM)]
    args = [jnp.array([lam_init, 1.0 - lam_init], F32)]
    for v in lams:
        specs.append(pl.BlockSpec((1, v.shape[-1]), lambda *_: (0, 0)))
        args.append(v.reshape(1, -1))
    return specs, args


def _attn_prompt(dq, k16, v16, gain, lam_init, lams, *, batch, seq, heads, tq_pref=512):
    n, width = dq.shape
    hd = width // heads
    dh = hd // 2
    tq = _pick_tile(seq, tq_pref, V7X_LANES)
    nq = seq // tq
    lam_specs, lam_args = _lam_inputs(lam_init, lams)
    in_specs = lam_specs + [
        pl.BlockSpec((tq, hd), lambda b, h, i: (b * nq + i, h)),
        pl.BlockSpec((seq, hd), lambda b, h, i: (b, h)),
        pl.BlockSpec((seq, hd), lambda b, h, i: (b, h)),
        pl.BlockSpec((1, hd), lambda b, h, i: (0, 0)),
    ]
    stat = pltpu.VMEM((tq, V7X_LANES), F32)
    accs = pltpu.VMEM((tq, hd), F32)
    sbuf = pltpu.VMEM((tq, tq), F32)
    vmem = 4 * seq * hd * 2 + 4 * tq * hd * 2 + 2 * tq * hd * 4 + 4 * tq * V7X_LANES * 4 \
        + 12 * tq * tq * 4 + (4 << 20)
    return pl.pallas_call(
        functools.partial(_attn_prompt_kernel, dh=dh, exp_scale=dh ** -0.5 * LOG2E),
        grid=(batch, heads, nq), in_specs=in_specs,
        out_specs=pl.BlockSpec((tq, hd), lambda b, h, i: (b * nq + i, h)),
        out_shape=jax.ShapeDtypeStruct((n, width), BF16),
        scratch_shapes=[stat, stat, accs, stat, stat, accs, sbuf, sbuf, sbuf, sbuf],
        compiler_params=_params(vmem, 3), name="diff_attn_prompt",
    )(*lam_args, dq, k16, v16, gain.reshape(1, hd))


def _attn_decode_kernel(lamc_ref, lq1_ref, lk1_ref, lq2_ref, lk2_ref, q_ref, kn_ref, vn_ref,
                        kc_ref, vc_ref, g_ref, o_ref, qd, m_scr, l_scr, a_scr,
                        *, heads, dh, n_new, exp_scale):
    j = pl.program_id(1)
    hd = 2 * dh
    s_q = q_ref.shape[0]

    @pl.when(j == 0)
    def _():
        qd[...] = jnp.zeros(qd.shape, BF16)
        m_scr[...] = jnp.full(m_scr.shape, -jnp.inf, F32)
        l_scr[...] = jnp.zeros(l_scr.shape, F32)
        a_scr[...] = jnp.zeros(a_scr.shape, F32)
        new_mask = lax.broadcasted_iota(jnp.int32, (2 * s_q, kn_ref.shape[0]), 1) < n_new
        for h in range(heads):
            hs = slice(h * hd, (h + 1) * hd)
            qd[h, 0:s_q, 0:dh] = q_ref[:, h * hd:h * hd + dh]
            qd[h, s_q:2 * s_q, dh:hd] = q_ref[:, h * hd + dh:(h + 1) * hd]
            s = lax.dot_general(qd[h], kn_ref[:, hs], NT_DIMS, preferred_element_type=F32)
            s = jnp.where(new_mask, s, NEG_INF)
            _softmax_step(s, vn_ref[:, hs], m_scr.at[h], l_scr.at[h], a_scr.at[h], exp_scale)

    for h in range(heads):
        hs = slice(h * hd, (h + 1) * hd)
        k = kc_ref[:, hs].astype(BF16)
        v = vc_ref[:, hs].astype(BF16)
        s = lax.dot_general(qd[h], k, NT_DIMS, preferred_element_type=F32)
        _softmax_step(s, v, m_scr.at[h], l_scr.at[h], a_scr.at[h], exp_scale)

    @pl.when(j == pl.num_programs(1) - 1)
    def _():
        lam = _lam_value(lamc_ref, lq1_ref, lk1_ref, lq2_ref, lk2_ref)
        for h in range(heads):
            a, l = a_scr[h], l_scr[h]
            o = _diff_finish(a[:s_q], l[:s_q], a[s_q:], l[s_q:], lam, g_ref[...], lamc_ref[1])
            o_ref[:, h * hd:(h + 1) * hd] = o.astype(BF16)


def _attn_decode(dq, k16, v16, cache_k, cache_v, layer, gain, lam_init, lams, *, batch, seq,
                 heads, tk_pref=512):
    n, width = dq.shape
    hd = width // heads
    dh = hd // 2
    past = cache_k.shape[2]
    tk = _pick_tile(past, tk_pref, V7X_LANES)
    pad = V7X_LANES
    kn = jnp.pad(k16.reshape(batch, seq, width), ((0, 0), (0, pad - seq), (0, 0)))
    vn = jnp.pad(v16.reshape(batch, seq, width), ((0, 0), (0, pad - seq), (0, 0)))
    lam_specs, lam_args = _lam_inputs(lam_init, lams)
    in_specs = lam_specs + [
        pl.BlockSpec((seq, width), lambda b, j: (b, 0)),
        pl.BlockSpec((None, pad, width), lambda b, j: (b, 0, 0)),
        pl.BlockSpec((None, pad, width), lambda b, j: (b, 0, 0)),
        pl.BlockSpec((None, None, tk, width), lambda b, j: (layer, b, j, 0)),
        pl.BlockSpec((None, None, tk, width), lambda b, j: (layer, b, j, 0)),
        pl.BlockSpec((1, hd), lambda b, j: (0, 0)),
    ]
    scratch = [pltpu.VMEM((heads, 2 * seq, hd), BF16),
               pltpu.VMEM((heads, 2 * seq, V7X_LANES), F32),
               pltpu.VMEM((heads, 2 * seq, V7X_LANES), F32),
               pltpu.VMEM((heads, 2 * seq, hd), F32)]
    vmem = 4 * tk * width * 4 + 2 * tk * width * 2 + 8 * pad * width * 2 + (8 << 20)
    return pl.pallas_call(
        functools.partial(_attn_decode_kernel, heads=heads, dh=dh, n_new=seq,
                          exp_scale=dh ** -0.5 * LOG2E),
        grid=(batch, past // tk), in_specs=in_specs,
        out_specs=pl.BlockSpec((seq, width), lambda b, j: (b, 0)),
        out_shape=jax.ShapeDtypeStruct((n, width), BF16),
        scratch_shapes=scratch, compiler_params=_params(vmem, 2), name="diff_attn_decode",
    )(*lam_args, dq, kn, vn, cache_k, cache_v, gain.reshape(1, hd))


def _ret_tables(heads, chunk, dk, dv):
    log_gamma = jnp.log1p(-jnp.exp2(-5.0 - jnp.arange(heads, dtype=F32)))
    idx = jnp.arange(chunk, dtype=F32)
    diff = idx[:, None] - idx[None, :]
    decay = jnp.where(diff >= 0, jnp.exp(log_gamma[:, None, None] * jnp.maximum(diff, 0.0)), 0.0)
    cross = jnp.exp(log_gamma[:, None] * (idx + 1.0)[None, :])
    kdec = jnp.exp(log_gamma[:, None] * (chunk - 1.0 - idx)[None, :])
    carry = jnp.exp(log_gamma * chunk)
    return (decay,
            jnp.broadcast_to(cross[:, :, None], (heads, chunk, dv)),
            jnp.broadcast_to(kdec[:, :, None], (heads, chunk, dk)),
            carry)


def _ret_kernel(*refs, has_init):
    refs = list(refs)
    carry_ref, q_ref, k_ref, v_ref, rg_ref, dec_ref, cross_ref, kdec_ref, g_ref = refs[:9]
    refs = refs[9:]
    s0_ref = refs.pop(0) if has_init else None
    o_ref, sout_ref, s_scr = refs
    heads, dk, dv = s_scr.shape

    @pl.when(pl.program_id(1) == 0)
    def _():
        s_scr[...] = s0_ref[...] if has_init else jnp.zeros(s_scr.shape, F32)

    for h in range(heads):
        ks, vs = slice(h * dk, (h + 1) * dk), slice(h * dv, (h + 1) * dv)
        q, k, v = q_ref[:, ks], k_ref[:, ks], v_ref[:, vs]
        state = s_scr[h]
        scores = lax.dot_general(q, k, NT_DIMS, preferred_element_type=F32) * dec_ref[h]
        inner = jnp.dot(scores.astype(BF16), v, preferred_element_type=F32)
        cross = jnp.dot(q, state.astype(BF16), preferred_element_type=F32) * cross_ref[h]
        k_dec = (k.astype(F32) * kdec_ref[h]).astype(BF16)
        new_state = carry_ref[h] * state + lax.dot_general(k_dec, v, TN_DIMS,
                                                           preferred_element_type=F32)
        s_scr[h] = new_state
        sout_ref[h] = new_state
        gate = rg_ref[:, vs].astype(F32)
        y = _rms_rows(inner + cross, g_ref[...]) * (gate * jax.nn.sigmoid(gate))
        o_ref[:, vs] = y.astype(BF16)


def _retention(rqk, rest, gain, state0, layer, *, batch, seq, heads, chunk_pref=256):
    n = rqk.shape[0]
    dk = rqk.shape[1] // (2 * heads)
    dv = rest.shape[1] // (4 * heads)
    chunk = _pick_tile(seq, chunk_pref, 16)
    nc = seq // chunk
    decay, cross, kdec, carry = _ret_tables(heads, chunk, dk, dv)
    has_init = state0 is not None
    row = lambda b, c: b * nc + c
    whole = lambda shape: pl.BlockSpec(shape, lambda b, c: (0,) * len(shape))
    in_specs = [
        pl.BlockSpec(memory_space=pltpu.SMEM),
        pl.BlockSpec((chunk, heads * dk), lambda b, c: (row(b, c), 0)),
        pl.BlockSpec((chunk, heads * dk), lambda b, c: (row(b, c), 1)),
        pl.BlockSpec((chunk, heads * dv), lambda b, c: (row(b, c), 0)),
        pl.BlockSpec((chunk, heads * dv), lambda b, c: (row(b, c), 1)),
        whole((heads, chunk, chunk)), whole((heads, chunk, dv)), whole((heads, chunk, dk)),
        whole((1, dv)),
    ]
    args = [carry, rqk, rqk, rest, rest, decay, cross, kdec, gain.reshape(1, dv)]
    if has_init:
        in_specs.append(pl.BlockSpec((None, None, heads, dk, dv),
                                     lambda b, c: (layer, b, 0, 0, 0)))
        args.append(state0)
    vmem = (2 * heads * chunk * (chunk + dv + dk) * 4 + 8 * chunk * heads * (dk + dv) * 2
            + 6 * heads * dk * dv * 4 + (8 << 20))
    ob, st = pl.pallas_call(
        functools.partial(_ret_kernel, has_init=has_init),
        grid=(batch, nc), in_specs=in_specs,
        out_specs=[pl.BlockSpec((chunk, heads * dv), lambda b, c: (row(b, c), 0)),
                   pl.BlockSpec((None, heads, dk, dv), lambda b, c: (b, 0, 0, 0))],
        out_shape=[jax.ShapeDtypeStruct((n, heads * dv), BF16),
                   jax.ShapeDtypeStruct((batch, heads, dk, dv), F32)],
        scratch_shapes=[pltpu.VMEM((heads, dk, dv), F32)],
        compiler_params=_params(vmem, 2), name="retention",
    )(*args)
    return ob, st


def _merge_kernel(oa_ref, ob_ref, wa_ref, wb_ref, ga_ref, gb_ref, o_ref):
    a = jnp.dot(oa_ref[...], wa_ref[...], preferred_element_type=F32)
    b = jnp.dot(ob_ref[...], wb_ref[...], preferred_element_type=F32)
    ga = jax.nn.sigmoid(ga_ref[...].astype(F32))
    gb = jax.nn.sigmoid(gb_ref[...].astype(F32))
    o_ref[...] = (ga * a + gb * b).astype(BF16)


def _merge(oa, ob, rest, w_a, w_b, layer, *, tm_pref=1024, tn_pref=512):
    n, k = oa.shape
    d = w_a.shape[2]
    tm = _pick_tile(n, tm_pref, 16)
    tn = _pick_tile(d, tn_pref, V7X_LANES)
    ga0 = (rest.shape[1] // 2) // tn
    gb0 = (3 * rest.shape[1] // 4) // tn
    vmem = 4 * tm * k * 2 + 4 * k * tn * 2 + 6 * tm * tn * 2 + 4 * tm * tn * 4 + (4 << 20)
    return pl.pallas_call(
        _merge_kernel, grid=(n // tm, d // tn),
        in_specs=[pl.BlockSpec((tm, k), lambda i, j: (i, 0)),
                  pl.BlockSpec((tm, k), lambda i, j: (i, 0)),
                  pl.BlockSpec((None, k, tn), lambda i, j: (layer, 0, j)),
                  pl.BlockSpec((None, k, tn), lambda i, j: (layer, 0, j)),
                  pl.BlockSpec((tm, tn), lambda i, j: (i, ga0 + j)),
                  pl.BlockSpec((tm, tn), lambda i, j: (i, gb0 + j))],
        out_specs=pl.BlockSpec((tm, tn), lambda i, j: (i, j)),
        out_shape=jax.ShapeDtypeStruct((n, d), BF16),
        compiler_params=_params(vmem, 2), name="branch_merge",
    )(oa, ob, w_a, w_b, rest, rest)


def _ffn_kernel(*refs, row_chunk, final_norm):
    refs = list(refs)
    x_ref, g_ref, wu_ref, wd_ref = refs[:4]
    fg_ref = refs[4] if final_norm else None
    o_ref, h_scr = refs[-2:]
    f = pl.program_id(1)
    n_chunks = x_ref.shape[0] // row_chunk

    @pl.when(f == 0)
    def _():
        def rows(r, carry):
            rs = pl.ds(pl.multiple_of(r * row_chunk, row_chunk), row_chunk)
            x = x_ref[rs, :]
            h_scr[rs, :] = _rms_rows(x, g_ref[...]).astype(BF16)
            o_ref[rs, :] = x
            return carry
        lax.fori_loop(0, n_chunks, rows, 0)

    u = jnp.dot(h_scr[...], wu_ref[...], preferred_element_type=F32)
    u = jnp.square(jnp.maximum(u, 0.0)).astype(BF16)
    o_ref[...] += jnp.dot(u, wd_ref[...], preferred_element_type=F32)

    if final_norm:
        @pl.when(f == pl.num_programs(1) - 1)
        def _():
            def rows(r, carry):
                rs = pl.ds(pl.multiple_of(r * row_chunk, row_chunk), row_chunk)
                o_ref[rs, :] = _rms_rows(o_ref[rs, :], fg_ref[...])
                return carry
            lax.fori_loop(0, n_chunks, rows, 0)


def _ffn(x, gain, w_up, w_down, layer, final_gain=None, *, tm_pref=1024, tf_pref=512):
    n, d = x.shape
    d_ff = w_up.shape[2]
    tm = _pick_tile(n, tm_pref, 16)
    tf = _pick_tile(d_ff, tf_pref, V7X_LANES)
    final_norm = final_gain is not None
    in_specs = [pl.BlockSpec((tm, d), lambda i, f: (i, 0)),
                pl.BlockSpec((1, d), lambda i, f: (0, 0)),
                pl.BlockSpec((None, d, tf), lambda i, f: (layer, 0, f)),
                pl.BlockSpec((None, tf, d), lambda i, f: (layer, f, 0))]
    args = [x, gain.reshape(1, d), w_up, w_down]
    if final_norm:
        in_specs.append(pl.BlockSpec((1, d), lambda i, f: (0, 0)))
        args.append(final_gain.reshape(1, d))
    vmem = 4 * tm * d * 4 + tm * d * 2 + 4 * d * tf * 2 + 2 * tm * tf * 4 + (6 << 20)
    return pl.pallas_call(
        functools.partial(_ffn_kernel, row_chunk=min(tm, 64), final_norm=final_norm),
        grid=(n // tm, d_ff // tf), in_specs=in_specs,
        out_specs=pl.BlockSpec((tm, d), lambda i, f: (i, 0)),
        out_shape=jax.ShapeDtypeStruct((n, d), F32),
        scratch_shapes=[pltpu.VMEM((tm, d), BF16)],
        compiler_params=_params(vmem, 2), name="ffn",
    )(*args)


def _rope_tables(pos, dk, batch):
    inv = 1.0 / (ROPE_BASE ** jnp.linspace(0.0, 1.0, dk // 2, dtype=F32))
    th = pos.astype(F32)[:, None] * jnp.repeat(inv, 2)[None, :]
    sin, cos = jnp.sin(th), jnp.cos(th)
    even = (jnp.arange(dk) % 2) == 0
    sin_next = jnp.where(even, -sin, 0.0)
    sin_prev = jnp.where(even, 0.0, sin)
    return tuple(jnp.tile(t, (batch, 1)) for t in (cos, sin_next, sin_prev))


def _mixer(x, layer, w, p, rope, past, kv_stack, *, depth, batch, seq, heads, ret_heads):
    d = x.shape[1]
    hd = p["subln_g"].shape[-1]
    diff_w = heads * hd
    dk = p["state"].shape[-2]
    dv = p["state"].shape[-1]
    qk_w = ret_heads * dk
    c_k, c_v, c_rqk, c_rest = diff_w, 2 * diff_w, 3 * diff_w, 3 * diff_w + 2 * qk_w

    dq, h = _proj(x, w["in"], layer, 0, diff_w, gain=p["norm_mix_g"], emit_h=True)
    k32, k16 = _proj_heads(h, w["in"], layer, c_k, heads, kv_stack[0], depth)
    v32, v16 = _proj_heads(h, w["in"], layer, c_v, heads, kv_stack[1], depth)
    rqk = _proj(h, w["in"], layer, c_rqk, 2 * qk_w, epi="rope", extra=rope,
                rope_scale=dk ** -0.5, tn_pref=qk_w)
    rest = _proj(h, w["in"], layer, c_rest, 4 * ret_heads * dv)

    lam_init = 0.8 - 0.6 * math.exp(-0.3 * layer)
    lams = (p["lq1"], p["lk1"], p["lq2"], p["lk2"])
    if past:
        oa = _attn_decode(dq, k16, v16, p["cache_k"], p["cache_v"], layer, p["subln_g"],
                          lam_init, lams, batch=batch, seq=seq, heads=heads)
        ob, st = _retention(rqk, rest, p["ret_g"], p["state"], layer,
                            batch=batch, seq=seq, heads=ret_heads)
    else:
        oa = _attn_prompt(dq, k16, v16, p["subln_g"], lam_init, lams,
                          batch=batch, seq=seq, heads=heads)
        ob, st = _retention(rqk, rest, p["ret_g"], None, layer,
                            batch=batch, seq=seq, heads=ret_heads)
    merged = _merge(oa, ob, rest, w["a"], w["b"], layer)
    x = _proj(merged, w["out"], layer, 0, d, epi="residual", extra=(x,))
    return x, (k32, v32), st


def kernel(x_prompt, x_sample, cache_diff_k, cache_diff_v, state_ret, norm_mix_g, w_in,
           lambda_q1, lambda_k1, lambda_q2, lambda_k2, diff_subln_g, ret_norm_g,
           w_branch_a, w_branch_b, w_out, norm_ffn_g, w_up, w_down, final_norm_g):
    depth, dec_batch, past_len, heads, hd = cache_diff_k.shape
    batch, seq, d = x_prompt.shape
    dec_seq = x_sample.shape[1]
    ret_heads, dk, dv = state_ret.shape[2:]

    w = {"in": w_in.astype(BF16), "a": w_branch_a.astype(BF16), "b": w_branch_b.astype(BF16),
         "out": w_out.astype(BF16), "up": w_up.astype(BF16), "down": w_down.astype(BF16)}
    cache_k = cache_diff_k.reshape(depth, dec_batch, past_len, heads * hd)
    cache_v = cache_diff_v.reshape(depth, dec_batch, past_len, heads * hd)
    rope_p = _rope_tables(jnp.arange(seq), dk, batch)
    rope_s = _rope_tables(past_len + jnp.arange(dec_seq), dk, dec_batch)

    xp = x_prompt.reshape(batch * seq, d)
    xs = x_sample.reshape(dec_batch * dec_seq, d)
    kv_p, kv_s = (None, None), (None, None)
    states_p, states_s = [], []
    for l in range(depth):
        p = {"norm_mix_g": norm_mix_g[l], "lq1": lambda_q1[l], "lk1": lambda_k1[l],
             "lq2": lambda_q2[l], "lk2": lambda_k2[l], "subln_g": diff_subln_g[l],
             "ret_g": ret_norm_g[l], "cache_k": cache_k, "cache_v": cache_v, "state": state_ret}
        final_g = final_norm_g if l == depth - 1 else None
        xp, kv_p, sp = _mixer(xp, l, w, p, rope_p, False, kv_p, depth=depth, batch=batch,
                              seq=seq, heads=heads, ret_heads=ret_heads)
        xs, kv_s, ss = _mixer(xs, l, w, p, rope_s, True, kv_s, depth=depth, batch=dec_batch,
                              seq=dec_seq, heads=heads, ret_heads=ret_heads)
        xp = _ffn(xp, norm_ffn_g[l], w["up"], w["down"], l, final_g)
        xs = _ffn(xs, norm_ffn_g[l], w["up"], w["down"], l, final_g)
        states_p.append(sp)
        states_s.append(ss)
    return (xp.reshape(batch, seq, d), xs.reshape(dec_batch, dec_seq, d),
            kv_p[0].reshape(depth, batch, seq, heads, hd),
            kv_p[1].reshape(depth, batch, seq, heads, hd), jnp.stack(states_p),
            kv_s[0].reshape(depth, dec_batch, dec_seq, heads, hd),
            kv_s[1].reshape(depth, dec_batch, dec_seq, heads, hd), jnp.stack(states_s))
```

```python
import functools
import math

import jax
import jax.numpy as jnp
from jax import lax
from jax.experimental import pallas as pl
from jax.experimental.pallas import tpu as pltpu

F32 = jnp.float32
BF16 = jnp.bfloat16

NORM_EPS = 1e-6
ROPE_BASE = 10000.0
MASK_CHUNK = 64
NEG_INF = -1e30
LOG2E = 1.4426950408889634

V7X_VMEM_BYTES = 64 * 1024 * 1024
V7X_LANES = 128
VMEM_REQUEST_CAP = V7X_VMEM_BYTES - 8 * 1024 * 1024

NT_DIMS = (((1,), (1,)), ((), ()))
TN_DIMS = (((0,), (0,)), ((), ()))


def _pick_tile(n, pref, align):
    if n <= pref:
        return n
    t = pref - pref % align
    while t >= align:
        if n % t == 0:
            return t
        t -= align
    raise ValueError(f"no tile for {n} (pref {pref}, align {align})")


def _params(vmem_bytes, n_grid):
    return pltpu.CompilerParams(
        dimension_semantics=("arbitrary",) * n_grid,
        vmem_limit_bytes=int(min(VMEM_REQUEST_CAP, vmem_bytes)))


def _rms_rows(x, gain):
    ms = jnp.mean(x * x, axis=-1, keepdims=True)
    return x * lax.rsqrt(ms + NORM_EPS) * gain


def _proj_kernel(*refs, has_norm, emit_h, epi, row_chunk, rope_scale, lead_scale):
    refs = list(refs)
    x_ref = refs.pop(0)
    g_ref = refs.pop(0) if has_norm else None
    w_ref = refs.pop(0)
    if epi == "rope":
        cos_ref, sina_ref, sinb_ref = refs.pop(0), refs.pop(0), refs.pop(0)
    if epi == "residual":
        res_ref = refs.pop(0)
    o_ref = refs.pop(0)
    hout_ref = refs.pop(0) if emit_h else None
    h_scr = refs.pop(0) if has_norm else None

    j = pl.program_id(1)
    if has_norm:
        @pl.when(j == 0)
        def _():
            def rows(r, carry):
                rs = pl.ds(pl.multiple_of(r * row_chunk, row_chunk), row_chunk)
                hrow = _rms_rows(x_ref[rs, :], g_ref[...]).astype(BF16)
                h_scr[rs, :] = hrow
                if emit_h:
                    hout_ref[rs, :] = hrow
                return carry
            lax.fori_loop(0, x_ref.shape[0] // row_chunk, rows, 0)
        lhs = h_scr[...]
    else:
        lhs = x_ref[...]

    acc = jnp.dot(lhs, w_ref[...], preferred_element_type=F32)
    if epi == "bf16":
        if lead_scale is not None:
            acc = acc * jnp.where(j < lead_scale[0], lead_scale[1], 1.0).astype(F32)
        o_ref[...] = acc.astype(BF16)
    elif epi == "residual":
        o_ref[...] = res_ref[...] + acc
    elif epi == "rope":
        n_tiles = pl.num_programs(1)
        scale = jnp.where(j < n_tiles // 2, 1.0, rope_scale).astype(F32)
        cos, sina, sinb = cos_ref[...], sina_ref[...], sinb_ref[...]
        for c in range(acc.shape[1] // V7X_LANES):
            cs = slice(c * V7X_LANES, (c + 1) * V7X_LANES)
            xs = acc[:, cs]
            nxt = pltpu.roll(xs, V7X_LANES - 1, 1)
            prv = pltpu.roll(xs, 1, 1)
            o_ref[:, cs] = ((xs * cos + nxt * sina + prv * sinb) * scale).astype(BF16)
    else:
        raise ValueError(epi)


def _proj(lhs, w, layer, col0, ncols, *, gain=None, emit_h=False, epi="bf16", extra=(),
          rope_scale=1.0, lead_cols=0, lead_scale=1.0, tm_pref=1024, tn_pref=1024):
    n, k = lhs.shape
    has_norm = gain is not None
    tm = _pick_tile(n, tm_pref, 16)
    tn = _pick_tile(ncols, tn_pref, V7X_LANES)
    assert col0 % tn == 0 and lead_cols % tn == 0
    cb = col0 // tn
    grid = (n // tm, ncols // tn)

    in_specs = [pl.BlockSpec((tm, k), lambda i, j: (i, 0))]
    args = [lhs]
    if has_norm:
        in_specs.append(pl.BlockSpec((1, k), lambda i, j: (0, 0)))
        args.append(gain.reshape(1, k))
    in_specs.append(pl.BlockSpec((None, k, tn), lambda i, j: (layer, 0, cb + j)))
    args.append(w)
    if epi == "rope":
        for t in extra:
            in_specs.append(pl.BlockSpec((tm, V7X_LANES), lambda i, j: (i, 0)))
            args.append(t)
    elif epi == "residual":
        in_specs.append(pl.BlockSpec((tm, tn), lambda i, j: (i, j)))
        args.append(extra[0])

    tile_spec = pl.BlockSpec((tm, tn), lambda i, j: (i, j))
    out_dtype = F32 if epi == "residual" else BF16
    out_shape = [jax.ShapeDtypeStruct((n, ncols), out_dtype)]
    out_specs = [tile_spec]
    out_bytes = tm * tn * jnp.dtype(out_dtype).itemsize
    if emit_h:
        out_shape.append(jax.ShapeDtypeStruct((n, k), BF16))
        out_specs.append(pl.BlockSpec((tm, k), lambda i, j: (i, 0)))
        out_bytes += tm * k * 2
    scratch = [pltpu.VMEM((tm, k), BF16)] if has_norm else []

    vmem = (2 * tm * k * lhs.dtype.itemsize + 2 * k * tn * 2 + 2 * out_bytes
            + (tm * k * 2 if has_norm else 0) + 3 * tm * tn * 4
            + (2 * tm * tn * 4 if epi == "residual" else 0) + (4 << 20))
    outs = pl.pallas_call(
        functools.partial(_proj_kernel, has_norm=has_norm, emit_h=emit_h, epi=epi,
                          row_chunk=min(tm, 64), rope_scale=rope_scale,
                          lead_scale=(lead_cols // tn, lead_scale) if lead_cols else None),
        grid=grid, in_specs=in_specs, out_specs=out_specs, out_shape=out_shape,
        scratch_shapes=scratch, compiler_params=_params(vmem, 2),
        name=f"proj_{epi}")(*args)
    return outs[0] if len(outs) == 1 else tuple(outs)


def _proj_heads_kernel(*refs, aliased):
    x_ref, w_ref = refs[0], refs[1]
    o32_ref, o16_ref = refs[-2], refs[-1]
    acc = jnp.dot(x_ref[...], w_ref[...], preferred_element_type=F32)
    o16_ref[...] = acc.astype(BF16)
    o32_ref[...] = acc.reshape(o32_ref.shape)


def _proj_heads(lhs, w, layer, col0, heads, stacked, depth, *, tm_pref=512):
    n, k = lhs.shape
    ncols = k
    hd = ncols // heads
    tm = _pick_tile(n, tm_pref, 16)
    cb = col0 // ncols
    assert col0 % ncols == 0
    aliased = stacked is not None
    in_specs = [pl.BlockSpec((tm, k), lambda i: (i, 0)),
                pl.BlockSpec((None, k, ncols), lambda i: (layer, 0, cb))]
    args = [lhs, w]
    if aliased:
        in_specs.append(pl.BlockSpec(memory_space=pl.ANY))
        args.append(stacked)
    vmem = 4 * tm * k * 2 + 4 * k * ncols * 2 + 4 * tm * ncols * 4 + 4 * tm * ncols * 2 + (4 << 20)
    return pl.pallas_call(
        functools.partial(_proj_heads_kernel, aliased=aliased),
        grid=(n // tm,), in_specs=in_specs,
        out_specs=[pl.BlockSpec((None, tm, heads, hd), lambda i: (layer, i, 0, 0)),
                   pl.BlockSpec((tm, ncols), lambda i: (i, 0))],
        out_shape=[jax.ShapeDtypeStruct((depth, n, heads, hd), F32),
                   jax.ShapeDtypeStruct((n, ncols), BF16)],
        input_output_aliases={2: 0} if aliased else {},
        compiler_params=_params(vmem, 1), name="proj_heads")(*args)


def _lam_value(lamc_ref, lq1_ref, lk1_ref, lq2_ref, lk2_ref):
    a = jnp.exp(jnp.sum(lq1_ref[...] * lk1_ref[...], axis=-1, keepdims=True))
    b = jnp.exp(jnp.sum(lq2_ref[...] * lk2_ref[...], axis=-1, keepdims=True))
    return a - b + lamc_ref[0]


def _lane_repeat(x, reps):
    return x if reps == 1 else jnp.concatenate([x] * reps, axis=1)


def _softmax_step(s, v, m_ref, l_ref, acc_ref):
    rows, keys = s.shape
    m_prev = m_ref[...]
    m_new = jnp.maximum(m_prev, jnp.max(s, axis=1, keepdims=True))
    alpha = jnp.exp2(m_prev - m_new)
    p = jnp.exp2(s - _lane_repeat(m_new, keys // V7X_LANES))
    l_ref[...] = alpha * l_ref[...] + jnp.sum(p, axis=1, keepdims=True)
    pv = jnp.dot(p.astype(BF16), v, preferred_element_type=F32)
    acc_ref[...] = acc_ref[...] * _lane_repeat(alpha, acc_ref.shape[1] // V7X_LANES) + pv
    m_ref[...] = m_new


def _diff_finish(a1, l1, a2, l2, lam, gain, out_scale):
    rep = a1.shape[1] // V7X_LANES
    o = a1 * _lane_repeat(1.0 / l1, rep) - lam * (a2 * _lane_repeat(1.0 / l2, rep))
    return _rms_rows(o, gain) * out_scale


def _attn_prompt_kernel(lamc_ref, lq1_ref, lk1_ref, lq2_ref, lk2_ref, q_ref, k_ref, v_ref, g_ref,
                        o_ref, m1, l1, a1, m2, l2, a2, sa1, sa2, sb1, sb2, *, dh):
    qi = pl.program_id(2)
    tq = q_ref.shape[0]
    tk = tq
    for m_ref, l_ref, a_ref in ((m1, l1, a1), (m2, l2, a2)):
        m_ref[...] = jnp.full(m_ref.shape, -jnp.inf, F32)
        l_ref[...] = jnp.zeros(l_ref.shape, F32)
        a_ref[...] = jnp.zeros(a_ref.shape, F32)
    maps = ((0, m1, l1, a1), (dh, m2, l2, a2))

    def rows(j):
        return pl.ds(pl.multiple_of(j * tk, tk), tk)

    def scores(j, bufs):
        for (lo, _, _, _), s_scr in zip(maps, bufs):
            s_scr[...] = lax.dot_general(q_ref[:, lo:lo + dh], k_ref[rows(j), lo:lo + dh], NT_DIMS,
                                         preferred_element_type=F32)

    def consume(j, bufs, mask):
        v = v_ref[rows(j), :]
        for (_, m_ref, l_ref, a_ref), s_scr in zip(maps, bufs):
            s = s_scr[...]
            if mask is not None:
                s = jnp.where(mask, s, NEG_INF)
            _softmax_step(s, v, m_ref, l_ref, a_ref)

    buf_a, buf_b = (sa1, sa2), (sb1, sb2)
    scores(0, buf_a)

    def block_pair(t, carry):
        scores(2 * t + 1, buf_b)
        consume(2 * t, buf_a, None)
        scores(2 * t + 2, buf_a)
        consume(2 * t + 1, buf_b, None)
        return carry
    lax.fori_loop(0, qi // 2, block_pair, 0)

    shift = MASK_CHUNK.bit_length() - 1
    row_chunk = lax.shift_right_logical(lax.broadcasted_iota(jnp.int32, (tq, tk), 0), shift)
    col_chunk = lax.shift_right_logical(lax.broadcasted_iota(jnp.int32, (tq, tk), 1), shift)
    mask = col_chunk <= row_chunk

    @pl.when(qi % 2 == 0)
    def _():
        consume(qi, buf_a, mask)

    @pl.when(qi % 2 == 1)
    def _():
        scores(qi, buf_b)
        consume(qi - 1, buf_a, None)
        consume(qi, buf_b, mask)

    lam = _lam_value(lamc_ref, lq1_ref, lk1_ref, lq2_ref, lk2_ref)
    o = _diff_finish(a1[...], l1[...], a2[...], l2[...], lam, g_ref[...], lamc_ref[1])
    o_ref[...] = o.astype(BF16)


def _lam_inputs(lam_init, lams):
    specs = [pl.BlockSpec(memory_space=pltpu.SMEM)]
    args = [jnp.array([lam_init, 1.0 - lam_init], F32)]
    for v in lams:
        specs.append(pl.BlockSpec((1, v.shape[-1]), lambda *_: (0, 0)))
        args.append(v.reshape(1, -1))
    return specs, args


def _attn_prompt(dq, k16, v16, gain, lam_init, lams, *, batch, seq, heads, tq_pref=512):
    n, width = k16.shape
    hd = width // heads
    dh = hd // 2
    tq = _pick_tile(seq, tq_pref, V7X_LANES)
    nq = seq // tq
    lam_specs, lam_args = _lam_inputs(lam_init, lams)
    in_specs = lam_specs + [
        pl.BlockSpec((tq, hd), lambda b, h, i: (b * nq + i, h)),
        pl.BlockSpec((seq, hd), lambda b, h, i: (b, h)),
        pl.BlockSpec((seq, hd), lambda b, h, i: (b, h)),
        pl.BlockSpec((1, hd), lambda b, h, i: (0, 0)),
    ]
    stat = pltpu.VMEM((tq, V7X_LANES), F32)
    accs = pltpu.VMEM((tq, hd), F32)
    sbuf = pltpu.VMEM((tq, tq), F32)
    vmem = 4 * seq * hd * 2 + 4 * tq * hd * 2 + 2 * tq * hd * 4 + 4 * tq * V7X_LANES * 4 \
        + 12 * tq * tq * 4 + (4 << 20)
    return pl.pallas_call(
        functools.partial(_attn_prompt_kernel, dh=dh),
        grid=(batch, heads, nq), in_specs=in_specs,
        out_specs=pl.BlockSpec((tq, hd), lambda b, h, i: (b * nq + i, h)),
        out_shape=jax.ShapeDtypeStruct((n, width), BF16),
        scratch_shapes=[stat, stat, accs, stat, stat, accs, sbuf, sbuf, sbuf, sbuf],
        compiler_params=_params(vmem, 3), name="diff_attn_prompt",
    )(*lam_args, dq, k16, v16, gain.reshape(1, hd))


def _attn_decode_kernel(lamc_ref, lq1_ref, lk1_ref, lq2_ref, lk2_ref, q_ref, kn_ref, vn_ref,
                        kc_ref, vc_ref, g_ref, o_ref, qd, m_scr, l_scr, a_scr,
                        *, heads, dh, n_new):
    j = pl.program_id(1)
    hd = 2 * dh
    s_q = q_ref.shape[0]

    @pl.when(j == 0)
    def _():
        qd[...] = jnp.zeros(qd.shape, BF16)
        m_scr[...] = jnp.full(m_scr.shape, -jnp.inf, F32)
        l_scr[...] = jnp.zeros(l_scr.shape, F32)
        a_scr[...] = jnp.zeros(a_scr.shape, F32)
        new_mask = lax.broadcasted_iota(jnp.int32, (2 * s_q, kn_ref.shape[0]), 1) < n_new
        for h in range(heads):
            hs = slice(h * hd, (h + 1) * hd)
            qd[h, 0:s_q, 0:dh] = q_ref[:, h * hd:h * hd + dh]
            qd[h, s_q:2 * s_q, dh:hd] = q_ref[:, h * hd + dh:(h + 1) * hd]
            s = lax.dot_general(qd[h], kn_ref[:, hs], NT_DIMS, preferred_element_type=F32)
            s = jnp.where(new_mask, s, NEG_INF)
            _softmax_step(s, vn_ref[:, hs], m_scr.at[h], l_scr.at[h], a_scr.at[h])

    tk = kc_ref.shape[0]
    k_all = kc_ref[...].astype(BF16).reshape(tk, heads * hd)
    v_all = vc_ref[...].astype(BF16).reshape(tk, heads * hd)
    for h in range(heads):
        hs = slice(h * hd, (h + 1) * hd)
        s = lax.dot_general(qd[h], k_all[:, hs], NT_DIMS, preferred_element_type=F32)
        _softmax_step(s, v_all[:, hs], m_scr.at[h], l_scr.at[h], a_scr.at[h])

    @pl.when(j == pl.num_programs(1) - 1)
    def _():
        lam = _lam_value(lamc_ref, lq1_ref, lk1_ref, lq2_ref, lk2_ref)
        for h in range(heads):
            a, l = a_scr[h], l_scr[h]
            o = _diff_finish(a[:s_q], l[:s_q], a[s_q:], l[s_q:], lam, g_ref[...], lamc_ref[1])
            o_ref[:, h * hd:(h + 1) * hd] = o.astype(BF16)


def _attn_decode(dq, k16, v16, cache_k, cache_v, layer, gain, lam_init, lams, *, batch, seq,
                 heads, tk_pref=512):
    n, width = k16.shape
    hd = width // heads
    dh = hd // 2
    past = cache_k.shape[2]
    tk = _pick_tile(past, tk_pref, V7X_LANES)
    pad = V7X_LANES
    kn = jnp.pad(k16.reshape(batch, seq, width), ((0, 0), (0, pad - seq), (0, 0)))
    vn = jnp.pad(v16.reshape(batch, seq, width), ((0, 0), (0, pad - seq), (0, 0)))
    lam_specs, lam_args = _lam_inputs(lam_init, lams)
    in_specs = lam_specs + [
        pl.BlockSpec((seq, width), lambda b, j: (b, 0)),
        pl.BlockSpec((None, pad, width), lambda b, j: (b, 0, 0)),
        pl.BlockSpec((None, pad, width), lambda b, j: (b, 0, 0)),
        pl.BlockSpec((None, None, tk, heads, hd), lambda b, j: (layer, b, j, 0, 0)),
        pl.BlockSpec((None, None, tk, heads, hd), lambda b, j: (layer, b, j, 0, 0)),
        pl.BlockSpec((1, hd), lambda b, j: (0, 0)),
    ]
    scratch = [pltpu.VMEM((heads, 2 * seq, hd), BF16),
               pltpu.VMEM((heads, 2 * seq, V7X_LANES), F32),
               pltpu.VMEM((heads, 2 * seq, V7X_LANES), F32),
               pltpu.VMEM((heads, 2 * seq, hd), F32)]
    vmem = 4 * tk * width * 4 + 2 * tk * width * 2 + 8 * pad * width * 2 + (8 << 20)
    return pl.pallas_call(
        functools.partial(_attn_decode_kernel, heads=heads, dh=dh, n_new=seq),
        grid=(batch, past // tk), in_specs=in_specs,
        out_specs=pl.BlockSpec((seq, width), lambda b, j: (b, 0)),
        out_shape=jax.ShapeDtypeStruct((n, width), BF16),
        scratch_shapes=scratch, compiler_params=_params(vmem, 2), name="diff_attn_decode",
    )(*lam_args, dq, kn, vn, cache_k, cache_v, gain.reshape(1, hd))


def _ret_tables(heads, chunk, dk, dv):
    log_gamma = jnp.log1p(-jnp.exp2(-5.0 - jnp.arange(heads, dtype=F32)))
    idx = jnp.arange(chunk, dtype=F32)
    diff = idx[:, None] - idx[None, :]
    decay = jnp.where(diff >= 0, jnp.exp(log_gamma[:, None, None] * jnp.maximum(diff, 0.0)), 0.0)
    cross = jnp.exp(log_gamma[:, None] * (idx + 1.0)[None, :])
    kdec = jnp.exp(log_gamma[:, None] * (chunk - 1.0 - idx)[None, :])
    carry = jnp.exp(log_gamma * chunk)
    return (decay,
            jnp.broadcast_to(cross[:, :, None], (heads, chunk, dv)),
            jnp.broadcast_to(kdec[:, :, None], (heads, chunk, dk)),
            carry)


def _ret_kernel(*refs, has_init):
    refs = list(refs)
    carry_ref, q_ref, k_ref, v_ref, rg_ref, dec_ref, cross_ref, kdec_ref, g_ref = refs[:9]
    refs = refs[9:]
    s0_ref = refs.pop(0) if has_init else None
    o_ref, sout_ref, s_scr = refs
    heads, dk, dv = s_scr.shape

    @pl.when(pl.program_id(1) == 0)
    def _():
        s_scr[...] = s0_ref[...] if has_init else jnp.zeros(s_scr.shape, F32)

    for h in range(heads):
        ks, vs = slice(h * dk, (h + 1) * dk), slice(h * dv, (h + 1) * dv)
        q, k, v = q_ref[:, ks], k_ref[:, ks], v_ref[:, vs]
        state = s_scr[h]
        scores = lax.dot_general(q, k, NT_DIMS, preferred_element_type=F32) * dec_ref[h]
        inner = jnp.dot(scores.astype(BF16), v, preferred_element_type=F32)
        cross = jnp.dot(q, state.astype(BF16), preferred_element_type=F32) * cross_ref[h]
        k_dec = (k.astype(F32) * kdec_ref[h]).astype(BF16)
        new_state = carry_ref[h] * state + lax.dot_general(k_dec, v, TN_DIMS,
                                                           preferred_element_type=F32)
        s_scr[h] = new_state
        sout_ref[h] = new_state
        gate = rg_ref[:, vs].astype(F32)
        y = _rms_rows(inner + cross, g_ref[...]) * (gate * jax.nn.sigmoid(gate))
        o_ref[:, vs] = y.astype(BF16)


def _retention(rqk, main, c_v, c_gate, dv, gain, state0, layer, *, batch, seq, heads,
               chunk_pref=256):
    n = rqk.shape[0]
    dk = rqk.shape[1] // (2 * heads)
    assert c_v % (heads * dv) == 0 and c_gate % (heads * dv) == 0
    vb, gb = c_v // (heads * dv), c_gate // (heads * dv)
    chunk = _pick_tile(seq, chunk_pref, 16)
    nc = seq // chunk
    decay, cross, kdec, carry = _ret_tables(heads, chunk, dk, dv)
    has_init = state0 is not None
    row = lambda b, c: b * nc + c
    whole = lambda shape: pl.BlockSpec(shape, lambda b, c: (0,) * len(shape))
    in_specs = [
        pl.BlockSpec(memory_space=pltpu.SMEM),
        pl.BlockSpec((chunk, heads * dk), lambda b, c: (row(b, c), 0)),
        pl.BlockSpec((chunk, heads * dk), lambda b, c: (row(b, c), 1)),
        pl.BlockSpec((chunk, heads * dv), lambda b, c: (row(b, c), vb)),
        pl.BlockSpec((chunk, heads * dv), lambda b, c: (row(b, c), gb)),
        whole((heads, chunk, chunk)), whole((heads, chunk, dv)), whole((heads, chunk, dk)),
        whole((1, dv)),
    ]
    args = [carry, rqk, rqk, main, main, decay, cross, kdec, gain.reshape(1, dv)]
    if has_init:
        in_specs.append(pl.BlockSpec((None, None, heads, dk, dv),
                                     lambda b, c: (layer, b, 0, 0, 0)))
        args.append(state0)
    vmem = (2 * heads * chunk * (chunk + dv + dk) * 4 + 8 * chunk * heads * (dk + dv) * 2
            + 6 * heads * dk * dv * 4 + (8 << 20))
    ob, st = pl.pallas_call(
        functools.partial(_ret_kernel, has_init=has_init),
        grid=(batch, nc), in_specs=in_specs,
        out_specs=[pl.BlockSpec((chunk, heads * dv), lambda b, c: (row(b, c), 0)),
                   pl.BlockSpec((None, heads, dk, dv), lambda b, c: (b, 0, 0, 0))],
        out_shape=[jax.ShapeDtypeStruct((n, heads * dv), BF16),
                   jax.ShapeDtypeStruct((batch, heads, dk, dv), F32)],
        scratch_shapes=[pltpu.VMEM((heads, dk, dv), F32)],
        compiler_params=_params(vmem, 2), name="retention",
    )(*args)
    return ob, st


def _merge_kernel(oa_ref, ob_ref, wa_ref, wb_ref, ga_ref, gb_ref, o_ref):
    a = jnp.dot(oa_ref[...], wa_ref[...], preferred_element_type=F32)
    b = jnp.dot(ob_ref[...], wb_ref[...], preferred_element_type=F32)
    ga = jax.nn.sigmoid(ga_ref[...].astype(F32))
    gb = jax.nn.sigmoid(gb_ref[...].astype(F32))
    o_ref[...] = (ga * a + gb * b).astype(BF16)


def _merge(oa, ob, main, c_ga, c_gb, w_a, w_b, layer, *, tm_pref=1024, tn_pref=512):
    n, k = oa.shape
    d = w_a.shape[2]
    tm = _pick_tile(n, tm_pref, 16)
    tn = _pick_tile(d, tn_pref, V7X_LANES)
    assert c_ga % tn == 0 and c_gb % tn == 0
    ga0, gb0 = c_ga // tn, c_gb // tn
    vmem = 4 * tm * k * 2 + 4 * k * tn * 2 + 6 * tm * tn * 2 + 4 * tm * tn * 4 + (4 << 20)
    return pl.pallas_call(
        _merge_kernel, grid=(n // tm, d // tn),
        in_specs=[pl.BlockSpec((tm, k), lambda i, j: (i, 0)),
                  pl.BlockSpec((tm, k), lambda i, j: (i, 0)),
                  pl.BlockSpec((None, k, tn), lambda i, j: (layer, 0, j)),
                  pl.BlockSpec((None, k, tn), lambda i, j: (layer, 0, j)),
                  pl.BlockSpec((tm, tn), lambda i, j: (i, ga0 + j)),
                  pl.BlockSpec((tm, tn), lambda i, j: (i, gb0 + j))],
        out_specs=pl.BlockSpec((tm, tn), lambda i, j: (i, j)),
        out_shape=jax.ShapeDtypeStruct((n, d), BF16),
        compiler_params=_params(vmem, 2), name="branch_merge",
    )(oa, ob, w_a, w_b, main, main)


def _ffn_kernel(*refs, row_chunk, final_norm):
    refs = list(refs)
    x_ref, g_ref, wu_ref, wd_ref = refs[:4]
    fg_ref = refs[4] if final_norm else None
    o_ref, h_scr = refs[-2:]
    f = pl.program_id(1)
    n_chunks = x_ref.shape[0] // row_chunk

    @pl.when(f == 0)
    def _():
        def rows(r, carry):
            rs = pl.ds(pl.multiple_of(r * row_chunk, row_chunk), row_chunk)
            x = x_ref[rs, :]
            h_scr[rs, :] = _rms_rows(x, g_ref[...]).astype(BF16)
            o_ref[rs, :] = x
            return carry
        lax.fori_loop(0, n_chunks, rows, 0)

    u = jnp.dot(h_scr[...], wu_ref[...], preferred_element_type=F32)
    u = jnp.square(jnp.maximum(u, 0.0)).astype(BF16)
    o_ref[...] += jnp.dot(u, wd_ref[...], preferred_element_type=F32)

    if final_norm:
        @pl.when(f == pl.num_programs(1) - 1)
        def _():
            def rows(r, carry):
                rs = pl.ds(pl.multiple_of(r * row_chunk, row_chunk), row_chunk)
                o_ref[rs, :] = _rms_rows(o_ref[rs, :], fg_ref[...])
                return carry
            lax.fori_loop(0, n_chunks, rows, 0)


def _ffn(x, gain, w_up, w_down, layer, final_gain=None, *, tm_pref=1024, tf_pref=512):
    n, d = x.shape
    d_ff = w_up.shape[2]
    tm = _pick_tile(n, tm_pref, 16)
    tf = _pick_tile(d_ff, tf_pref, V7X_LANES)
    final_norm = final_gain is not None
    in_specs = [pl.BlockSpec((tm, d), lambda i, f: (i, 0)),
                pl.BlockSpec((1, d), lambda i, f: (0, 0)),
                pl.BlockSpec((None, d, tf), lambda i, f: (layer, 0, f)),
                pl.BlockSpec((None, tf, d), lambda i, f: (layer, f, 0))]
    args = [x, gain.reshape(1, d), w_up, w_down]
    if final_norm:
        in_specs.append(pl.BlockSpec((1, d), lambda i, f: (0, 0)))
        args.append(final_gain.reshape(1, d))
    vmem = 4 * tm * d * 4 + tm * d * 2 + 4 * d * tf * 2 + 2 * tm * tf * 4 + (6 << 20)
    return pl.pallas_call(
        functools.partial(_ffn_kernel, row_chunk=min(tm, 64), final_norm=final_norm),
        grid=(n // tm, d_ff // tf), in_specs=in_specs,
        out_specs=pl.BlockSpec((tm, d), lambda i, f: (i, 0)),
        out_shape=jax.ShapeDtypeStruct((n, d), F32),
        scratch_shapes=[pltpu.VMEM((tm, d), BF16)],
        compiler_params=_params(vmem, 2), name="ffn",
    )(*args)


def _rope_tables(pos, dk, batch):
    inv = 1.0 / (ROPE_BASE ** jnp.linspace(0.0, 1.0, dk // 2, dtype=F32))
    th = pos.astype(F32)[:, None] * jnp.repeat(inv, 2)[None, :]
    sin, cos = jnp.sin(th), jnp.cos(th)
    even = (jnp.arange(dk) % 2) == 0
    sin_next = jnp.where(even, -sin, 0.0)
    sin_prev = jnp.where(even, 0.0, sin)
    return tuple(jnp.tile(t, (batch, 1)) for t in (cos, sin_next, sin_prev))


def _permute_w_in(w_in, diff_w, qk_w):
    kv_end = 3 * diff_w
    rest0 = kv_end + 2 * qk_w
    return jnp.concatenate([w_in[..., :diff_w], w_in[..., rest0:], w_in[..., diff_w:kv_end],
                            w_in[..., kv_end:rest0]], axis=-1).astype(BF16)


def _mixer(x, layer, w, p, rope, past, kv_stack, *, depth, batch, seq, heads, ret_heads):
    d = x.shape[1]
    hd = p["subln_g"].shape[-1]
    diff_w = heads * hd
    dk = p["state"].shape[-2]
    dv = p["state"].shape[-1]
    qk_w = ret_heads * dk
    ret_w = ret_heads * dv
    c_rv, c_rg, c_ga, c_gb = diff_w, diff_w + ret_w, diff_w + 2 * ret_w, diff_w + 2 * ret_w + d
    main_w = c_gb + d
    c_k, c_v, c_rqk = main_w, main_w + diff_w, main_w + 2 * diff_w

    main, h = _proj(x, w["in"], layer, 0, main_w, gain=p["norm_mix_g"], emit_h=True,
                    lead_cols=diff_w, lead_scale=(hd // 2) ** -0.5 * LOG2E)
    k32, k16 = _proj_heads(h, w["in"], layer, c_k, heads, kv_stack[0], depth)
    v32, v16 = _proj_heads(h, w["in"], layer, c_v, heads, kv_stack[1], depth)
    rqk = _proj(h, w["in"], layer, c_rqk, 2 * qk_w, epi="rope", extra=rope,
                rope_scale=dk ** -0.5, tn_pref=qk_w)

    lam_init = 0.8 - 0.6 * math.exp(-0.3 * layer)
    lams = (p["lq1"], p["lk1"], p["lq2"], p["lk2"])
    if past:
        oa = _attn_decode(main, k16, v16, p["cache_k"], p["cache_v"], layer, p["subln_g"],
                          lam_init, lams, batch=batch, seq=seq, heads=heads)
        state0 = p["state"]
    else:
        oa = _attn_prompt(main, k16, v16, p["subln_g"], lam_init, lams,
                          batch=batch, seq=seq, heads=heads)
        state0 = None
    ob, st = _retention(rqk, main, c_rv, c_rg, dv, p["ret_g"], state0, layer,
                        batch=batch, seq=seq, heads=ret_heads)
    merged = _merge(oa, ob, main, c_ga, c_gb, w["a"], w["b"], layer)
    x = _proj(merged, w["out"], layer, 0, d, epi="residual", extra=(x,))
    return x, (k32, v32), st


def kernel(x_prompt, x_sample, cache_diff_k, cache_diff_v, state_ret, norm_mix_g, w_in,
           lambda_q1, lambda_k1, lambda_q2, lambda_k2, diff_subln_g, ret_norm_g,
           w_branch_a, w_branch_b, w_out, norm_ffn_g, w_up, w_down, final_norm_g):
    depth, dec_batch, past_len, heads, hd = cache_diff_k.shape
    batch, seq, d = x_prompt.shape
    dec_seq = x_sample.shape[1]
    ret_heads, dk, dv = state_ret.shape[2:]

    w = {"in": _permute_w_in(w_in, heads * hd, ret_heads * dk),
         "a": w_branch_a.astype(BF16), "b": w_branch_b.astype(BF16),
         "out": w_out.astype(BF16), "up": w_up.astype(BF16), "down": w_down.astype(BF16)}
    cache_k, cache_v = cache_diff_k, cache_diff_v
    rope_p = _rope_tables(jnp.arange(seq), dk, batch)
    rope_s = _rope_tables(past_len + jnp.arange(dec_seq), dk, dec_batch)

    xp = x_prompt.reshape(batch * seq, d)
    xs = x_sample.reshape(dec_batch * dec_seq, d)
    kv_p, kv_s = (None, None), (None, None)
    states_p, states_s = [], []
    for l in range(depth):
        p = {"norm_mix_g": norm_mix_g[l], "lq1": lambda_q1[l], "lk1": lambda_k1[l],
             "lq2": lambda_q2[l], "lk2": lambda_k2[l], "subln_g": diff_subln_g[l],
             "ret_g": ret_norm_g[l], "cache_k": cache_k, "cache_v": cache_v, "state": state_ret}
        final_g = final_norm_g if l == depth - 1 else None
        xp, kv_p, sp = _mixer(xp, l, w, p, rope_p, False, kv_p, depth=depth, batch=batch,
                              seq=seq, heads=heads, ret_heads=ret_heads)
        xs, kv_s, ss = _mixer(xs, l, w, p, rope_s, True, kv_s, depth=depth, batch=dec_batch,
                              seq=dec_seq, heads=heads, ret_heads=ret_heads)
        xp = _ffn(xp, norm_ffn_g[l], w["up"], w["down"], l, final_g)
        xs = _ffn(xs, norm_ffn_g[l], w["up"], w["down"], l, final_g)
        states_p.append(sp)
        states_s.append(ss)
    return (xp.reshape(batch, seq, d), xs.reshape(dec_batch, dec_seq, d),
            kv_p[0].reshape(depth, batch, seq, heads, hd),
            kv_p[1].reshape(depth, batch, seq, heads, hd), jnp.stack(states_p),
            kv_s[0].reshape(depth, dec_batch, dec_seq, heads, hd),
            kv_s[1].reshape(depth, dec_batch, dec_seq, heads, hd), jnp.stack(states_s))
```

```python
import functools
import math

import jax
import jax.numpy as jnp
from jax import lax
from jax.experimental import pallas as pl
from jax.experimental.pallas import tpu as pltpu

F32 = jnp.float32
BF16 = jnp.bfloat16

NORM_EPS = 1e-6
ROPE_BASE = 10000.0
MASK_CHUNK = 64
NEG_INF = -1e30
SHIFT_LIMIT = 48.0
LOG2E = 1.4426950408889634

V7X_VMEM_BYTES = 64 * 1024 * 1024
V7X_LANES = 128
VMEM_REQUEST_CAP = V7X_VMEM_BYTES - 8 * 1024 * 1024

NT_DIMS = (((1,), (1,)), ((), ()))
TN_DIMS = (((0,), (0,)), ((), ()))


def _pick_tile(n, pref, align):
    if n <= pref:
        return n
    t = pref - pref % align
    while t >= align:
        if n % t == 0:
            return t
        t -= align
    raise ValueError(f"no tile for {n} (pref {pref}, align {align})")


def _params(vmem_bytes, n_grid):
    return pltpu.CompilerParams(
        dimension_semantics=("arbitrary",) * n_grid,
        vmem_limit_bytes=int(min(VMEM_REQUEST_CAP, vmem_bytes)))


def _rms_rows(x, gain):
    ms = jnp.mean(x * x, axis=-1, keepdims=True)
    return x * lax.rsqrt(ms + NORM_EPS) * gain


def _proj_kernel(*refs, has_norm, emit_h, epi, row_chunk, rope_scale):
    refs = list(refs)
    x_ref = refs.pop(0)
    g_ref = refs.pop(0) if has_norm else None
    w_ref = refs.pop(0)
    if epi == "rope":
        cos_ref, sina_ref, sinb_ref = refs.pop(0), refs.pop(0), refs.pop(0)
    if epi == "residual":
        res_ref = refs.pop(0)
    o_ref = refs.pop(0)
    hout_ref = refs.pop(0) if emit_h else None
    h_scr = refs.pop(0) if has_norm else None

    j = pl.program_id(1)
    if has_norm:
        @pl.when(j == 0)
        def _():
            def rows(r, carry):
                rs = pl.ds(pl.multiple_of(r * row_chunk, row_chunk), row_chunk)
                hrow = _rms_rows(x_ref[rs, :], g_ref[...]).astype(BF16)
                h_scr[rs, :] = hrow
                if emit_h:
                    hout_ref[rs, :] = hrow
                return carry
            lax.fori_loop(0, x_ref.shape[0] // row_chunk, rows, 0)
        lhs = h_scr[...]
    else:
        lhs = x_ref[...]

    acc = jnp.dot(lhs, w_ref[...], preferred_element_type=F32)
    if epi == "bf16":
        o_ref[...] = acc.astype(BF16)
    elif epi == "residual":
        o_ref[...] = res_ref[...] + acc
    elif epi == "rope":
        n_tiles = pl.num_programs(1)
        scale = jnp.where(j < n_tiles // 2, 1.0, rope_scale).astype(F32)
        cos, sina, sinb = cos_ref[...], sina_ref[...], sinb_ref[...]
        for c in range(acc.shape[1] // V7X_LANES):
            cs = slice(c * V7X_LANES, (c + 1) * V7X_LANES)
            xs = acc[:, cs]
            nxt = pltpu.roll(xs, V7X_LANES - 1, 1)
            prv = pltpu.roll(xs, 1, 1)
            o_ref[:, cs] = ((xs * cos + nxt * sina + prv * sinb) * scale).astype(BF16)
    else:
        raise ValueError(epi)


def _proj(lhs, w, layer, col0, ncols, *, gain=None, emit_h=False, epi="bf16", extra=(),
          rope_scale=1.0, gap=(0, 0), tm_pref=1024, tn_pref=1024):
    n, k = lhs.shape
    has_norm = gain is not None
    tm = _pick_tile(n, tm_pref, 16)
    tn = _pick_tile(ncols, tn_pref, V7X_LANES)
    assert col0 % tn == 0 and gap[0] % tn == 0 and gap[1] % tn == 0
    cb = col0 // tn
    lead_tiles, skip_tiles = gap[0] // tn, gap[1] // tn
    grid = (n // tm, ncols // tn)

    def w_tile(j):
        return cb + j + jnp.where(j >= lead_tiles, skip_tiles, 0) if skip_tiles else cb + j

    in_specs = [pl.BlockSpec((tm, k), lambda i, j: (i, 0))]
    args = [lhs]
    if has_norm:
        in_specs.append(pl.BlockSpec((1, k), lambda i, j: (0, 0)))
        args.append(gain.reshape(1, k))
    in_specs.append(pl.BlockSpec((None, k, tn), lambda i, j: (layer, 0, w_tile(j))))
    args.append(w)
    if epi == "rope":
        for t in extra:
            in_specs.append(pl.BlockSpec((tm, V7X_LANES), lambda i, j: (i, 0)))
            args.append(t)
    elif epi == "residual":
        in_specs.append(pl.BlockSpec((tm, tn), lambda i, j: (i, j)))
        args.append(extra[0])

    tile_spec = pl.BlockSpec((tm, tn), lambda i, j: (i, j))
    out_dtype = F32 if epi == "residual" else BF16
    out_shape = [jax.ShapeDtypeStruct((n, ncols), out_dtype)]
    out_specs = [tile_spec]
    out_bytes = tm * tn * jnp.dtype(out_dtype).itemsize
    if emit_h:
        out_shape.append(jax.ShapeDtypeStruct((n, k), BF16))
        out_specs.append(pl.BlockSpec((tm, k), lambda i, j: (i, 0)))
        out_bytes += tm * k * 2
    scratch = [pltpu.VMEM((tm, k), BF16)] if has_norm else []

    vmem = (2 * tm * k * lhs.dtype.itemsize + 2 * k * tn * 2 + 2 * out_bytes
            + (tm * k * 2 if has_norm else 0) + 3 * tm * tn * 4
            + (2 * tm * tn * 4 if epi == "residual" else 0) + (4 << 20))
    outs = pl.pallas_call(
        functools.partial(_proj_kernel, has_norm=has_norm, emit_h=emit_h, epi=epi,
                          row_chunk=min(tm, 64), rope_scale=rope_scale),
        grid=grid, in_specs=in_specs, out_specs=out_specs, out_shape=out_shape,
        scratch_shapes=scratch, compiler_params=_params(vmem, 2),
        name=f"proj_{epi}")(*args)
    return outs[0] if len(outs) == 1 else tuple(outs)


def _proj_heads_kernel(*refs, aliased):
    x_ref, w_ref = refs[0], refs[1]
    o32_ref, o16_ref = refs[-2], refs[-1]
    acc = jnp.dot(x_ref[...], w_ref[...], preferred_element_type=F32)
    o16_ref[...] = acc.astype(BF16)
    o32_ref[...] = acc.reshape(o32_ref.shape)


def _proj_heads(lhs, w, layer, col0, heads, stacked, depth, *, tm_pref=512):
    n, k = lhs.shape
    ncols = k
    hd = ncols // heads
    tm = _pick_tile(n, tm_pref, 16)
    cb = col0 // ncols
    assert col0 % ncols == 0
    aliased = stacked is not None
    in_specs = [pl.BlockSpec((tm, k), lambda i: (i, 0)),
                pl.BlockSpec((None, k, ncols), lambda i: (layer, 0, cb))]
    args = [lhs, w]
    if aliased:
        in_specs.append(pl.BlockSpec(memory_space=pl.ANY))
        args.append(stacked)
    vmem = 4 * tm * k * 2 + 4 * k * ncols * 2 + 4 * tm * ncols * 4 + 4 * tm * ncols * 2 + (4 << 20)
    return pl.pallas_call(
        functools.partial(_proj_heads_kernel, aliased=aliased),
        grid=(n // tm,), in_specs=in_specs,
        out_specs=[pl.BlockSpec((None, tm, heads, hd), lambda i: (layer, i, 0, 0)),
                   pl.BlockSpec((tm, ncols), lambda i: (i, 0))],
        out_shape=[jax.ShapeDtypeStruct((depth, n, heads, hd), F32),
                   jax.ShapeDtypeStruct((n, ncols), BF16)],
        input_output_aliases={2: 0} if aliased else {},
        compiler_params=_params(vmem, 1), name="proj_heads")(*args)


def _lam_value(lamc_ref, lq1_ref, lk1_ref, lq2_ref, lk2_ref):
    a = jnp.exp(jnp.sum(lq1_ref[...] * lk1_ref[...], axis=-1, keepdims=True))
    b = jnp.exp(jnp.sum(lq2_ref[...] * lk2_ref[...], axis=-1, keepdims=True))
    return a - b + lamc_ref[0]


def _lane_repeat(x, reps):
    return x if reps == 1 else jnp.concatenate([x] * reps, axis=1)


def _softmax_step(s, v, m_ref, l_ref, acc_ref):
    rows, keys = s.shape
    m_prev = m_ref[...]
    m_new = jnp.maximum(m_prev, jnp.max(s, axis=1, keepdims=True))
    alpha = jnp.exp2(m_prev - m_new)
    p = jnp.exp2(s - _lane_repeat(m_new, keys // V7X_LANES))
    l_ref[...] = alpha * l_ref[...] + jnp.sum(p, axis=1, keepdims=True)
    pv = jnp.dot(p.astype(BF16), v, preferred_element_type=F32)
    acc_ref[...] = acc_ref[...] * _lane_repeat(alpha, acc_ref.shape[1] // V7X_LANES) + pv
    m_ref[...] = m_new


def _diff_finish(a1, l1, a2, l2, lam, gain, out_scale):
    rep = a1.shape[1] // V7X_LANES
    o = a1 * _lane_repeat(1.0 / l1, rep) - lam * (a2 * _lane_repeat(1.0 / l2, rep))
    return _rms_rows(o, gain) * out_scale


def _attn_prompt_kernel(lamc_ref, lq1_ref, lk1_ref, lq2_ref, lk2_ref, q_ref, k_ref, v_ref, g_ref,
                        o_ref, m1, l1, a1, m2, l2, a2, sa1, sa2, sb1, sb2, ka1, ka2, qa1, qa2,
                        knorm, *, dh):
    qi = pl.program_id(2)
    tq = q_ref.shape[0]
    tk = tq
    seq = k_ref.shape[0]
    maps = ((0, m1, l1, a1, ka1, qa1), (dh, m2, l2, a2, ka2, qa2))

    @pl.when(qi == 0)
    def _():
        minus_one = jnp.where(lax.broadcasted_iota(jnp.int32, (tk, dh), 1) == 0, -1.0, 0.0)
        for idx, (lo, _, _, _, ka, _) in enumerate(maps):
            def widen(c, kmax2):
                rs = pl.ds(pl.multiple_of(c * tk, tk), tk)
                kc = k_ref[rs, lo:lo + dh]
                ka[rs, 0:dh] = kc
                ka[rs, dh:2 * dh] = minus_one.astype(BF16)
                kf = kc.astype(F32)
                n2 = jnp.sum(kf * kf, axis=1, keepdims=True)
                return jnp.maximum(kmax2, jnp.max(n2, axis=0, keepdims=True))
            kmax2 = lax.fori_loop(0, seq // tk, widen, jnp.zeros((1, 1), F32))
            knorm[idx] = jnp.broadcast_to(jnp.sqrt(kmax2), knorm.shape[1:])

    for m_ref, l_ref, a_ref in ((m1, l1, a1), (m2, l2, a2)):
        m_ref[...] = jnp.full(m_ref.shape, -jnp.inf, F32)
        l_ref[...] = jnp.zeros(l_ref.shape, F32)
        a_ref[...] = jnp.zeros(a_ref.shape, F32)

    worst = jnp.zeros((1, 1), F32)
    for idx, (lo, _, _, _, _, qa) in enumerate(maps):
        q = q_ref[:, lo:lo + dh]
        qf = q.astype(F32)
        bound = jnp.sqrt(jnp.sum(qf * qf, axis=1, keepdims=True)) * knorm[idx][0:1, 0:1]
        qa[:, 0:dh] = q
        qa[:, dh:2 * dh] = jnp.broadcast_to(bound, (tq, dh)).astype(BF16)
        worst = jnp.maximum(worst, jnp.max(bound, axis=0, keepdims=True))
    fast = jnp.max(worst) <= SHIFT_LIMIT

    def rows(j):
        return pl.ds(pl.multiple_of(j * tk, tk), tk)

    def scores_exact(j, bufs):
        for (lo, *_), s_scr in zip(maps, bufs):
            s_scr[...] = lax.dot_general(q_ref[:, lo:lo + dh], k_ref[rows(j), lo:lo + dh], NT_DIMS,
                                         preferred_element_type=F32)

    def consume_exact(j, bufs, mask):
        v = v_ref[rows(j), :]
        for (_, m_ref, l_ref, a_ref, _, _), s_scr in zip(maps, bufs):
            s = s_scr[...]
            if mask is not None:
                s = jnp.where(mask, s, NEG_INF)
            _softmax_step(s, v, m_ref, l_ref, a_ref)

    def scores_fast(j, bufs):
        for (_, _, _, _, ka, qa), s_scr in zip(maps, bufs):
            s_scr[...] = lax.dot_general(qa[...], ka[rows(j), :], NT_DIMS,
                                         preferred_element_type=F32)

    def consume_fast(j, bufs, mask):
        v = v_ref[rows(j), :]
        for (_, _, l_ref, a_ref, _, _), s_scr in zip(maps, bufs):
            s = s_scr[...]
            if mask is not None:
                s = jnp.where(mask, s, NEG_INF)
            p = jnp.exp2(s)
            l_ref[...] += jnp.sum(p, axis=1, keepdims=True)
            a_ref[...] += jnp.dot(p.astype(BF16), v, preferred_element_type=F32)

    shift = MASK_CHUNK.bit_length() - 1
    row_chunk = lax.shift_right_logical(lax.broadcasted_iota(jnp.int32, (tq, tk), 0), shift)
    col_chunk = lax.shift_right_logical(lax.broadcasted_iota(jnp.int32, (tq, tk), 1), shift)
    mask = col_chunk <= row_chunk
    buf_a, buf_b = (sa1, sa2), (sb1, sb2)

    def sweep(scores, consume):
        scores(0, buf_a)

        def block_pair(t, carry):
            scores(2 * t + 1, buf_b)
            consume(2 * t, buf_a, None)
            scores(2 * t + 2, buf_a)
            consume(2 * t + 1, buf_b, None)
            return carry
        lax.fori_loop(0, qi // 2, block_pair, 0)

        @pl.when(qi % 2 == 0)
        def _():
            consume(qi, buf_a, mask)

        @pl.when(qi % 2 == 1)
        def _():
            scores(qi, buf_b)
            consume(qi - 1, buf_a, None)
            consume(qi, buf_b, mask)

    @pl.when(fast)
    def _():
        sweep(scores_fast, consume_fast)

    @pl.when(jnp.logical_not(fast))
    def _():
        sweep(scores_exact, consume_exact)

    lam = _lam_value(lamc_ref, lq1_ref, lk1_ref, lq2_ref, lk2_ref)
    o = _diff_finish(a1[...], l1[...], a2[...], l2[...], lam, g_ref[...], lamc_ref[1])
    o_ref[...] = o.astype(BF16)


def _lam_inputs(lam_init, lams):
    specs = [pl.BlockSpec(memory_space=pltpu.SMEM)]
    args = [jnp.array([lam_init, 1.0 - lam_init], F32)]
    for v in lams:
        specs.append(pl.BlockSpec((1, v.shape[-1]), lambda *_: (0, 0)))
        args.append(v.reshape(1, -1))
    return specs, args


def _attn_prompt(dq, k16, v16, gain, lam_init, lams, *, batch, seq, heads, tq_pref=512):
    n, width = k16.shape
    hd = width // heads
    dh = hd // 2
    tq = _pick_tile(seq, tq_pref, V7X_LANES)
    nq = seq // tq
    lam_specs, lam_args = _lam_inputs(lam_init, lams)
    in_specs = lam_specs + [
        pl.BlockSpec((tq, hd), lambda b, h, i: (b * nq + i, h)),
        pl.BlockSpec((seq, hd), lambda b, h, i: (b, h)),
        pl.BlockSpec((seq, hd), lambda b, h, i: (b, h)),
        pl.BlockSpec((1, hd), lambda b, h, i: (0, 0)),
    ]
    stat = pltpu.VMEM((tq, V7X_LANES), F32)
    accs = pltpu.VMEM((tq, hd), F32)
    sbuf = pltpu.VMEM((tq, tq), F32)
    vmem = 4 * seq * hd * 2 + 4 * tq * hd * 2 + 2 * tq * hd * 4 + 4 * tq * V7X_LANES * 4 \
        + 12 * tq * tq * 4 + 2 * seq * hd * 2 + (4 << 20)
    wide_k = pltpu.VMEM((seq, hd), BF16)
    wide_q = pltpu.VMEM((tq, hd), BF16)
    return pl.pallas_call(
        functools.partial(_attn_prompt_kernel, dh=dh),
        grid=(batch, heads, nq), in_specs=in_specs,
        out_specs=pl.BlockSpec((tq, hd), lambda b, h, i: (b * nq + i, h)),
        out_shape=jax.ShapeDtypeStruct((n, width), BF16),
        scratch_shapes=[stat, stat, accs, stat, stat, accs, sbuf, sbuf, sbuf, sbuf,
                        wide_k, wide_k, wide_q, wide_q, pltpu.VMEM((2, 8, V7X_LANES), F32)],
        compiler_params=_params(vmem, 3), name="diff_attn_prompt",
    )(*lam_args, dq, k16, v16, gain.reshape(1, hd))


def _attn_decode_kernel(lamc_ref, lq1_ref, lk1_ref, lq2_ref, lk2_ref, q_ref, kn_ref, vn_ref,
                        kc_ref, vc_ref, g_ref, o_ref, qd, m_scr, l_scr, a_scr,
                        *, heads, dh, n_new):
    j = pl.program_id(1)
    hd = 2 * dh
    s_q = q_ref.shape[0]

    @pl.when(j == 0)
    def _():
        qd[...] = jnp.zeros(qd.shape, BF16)
        m_scr[...] = jnp.full(m_scr.shape, -jnp.inf, F32)
        l_scr[...] = jnp.zeros(l_scr.shape, F32)
        a_scr[...] = jnp.zeros(a_scr.shape, F32)
        new_mask = lax.broadcasted_iota(jnp.int32, (2 * s_q, kn_ref.shape[0]), 1) < n_new
        for h in range(heads):
            hs = slice(h * hd, (h + 1) * hd)
            qd[h, 0:s_q, 0:dh] = q_ref[:, h * hd:h * hd + dh]
            qd[h, s_q:2 * s_q, dh:hd] = q_ref[:, h * hd + dh:(h + 1) * hd]
            s = lax.dot_general(qd[h], kn_ref[:, hs], NT_DIMS, preferred_element_type=F32)
            s = jnp.where(new_mask, s, NEG_INF)
            _softmax_step(s, vn_ref[:, hs], m_scr.at[h], l_scr.at[h], a_scr.at[h])

    tk = kc_ref.shape[0]
    k_all = kc_ref[...].astype(BF16).reshape(tk, heads * hd)
    v_all = vc_ref[...].astype(BF16).reshape(tk, heads * hd)
    scores = [lax.dot_general(qd[h], k_all[:, h * hd:(h + 1) * hd], NT_DIMS,
                              preferred_element_type=F32) for h in range(heads)]
    for h in range(heads):
        _softmax_step(scores[h], v_all[:, h * hd:(h + 1) * hd], m_scr.at[h], l_scr.at[h],
                      a_scr.at[h])

    @pl.when(j == pl.num_programs(1) - 1)
    def _():
        lam = _lam_value(lamc_ref, lq1_ref, lk1_ref, lq2_ref, lk2_ref)
        for h in range(heads):
            a, l = a_scr[h], l_scr[h]
            o = _diff_finish(a[:s_q], l[:s_q], a[s_q:], l[s_q:], lam, g_ref[...], lamc_ref[1])
            o_ref[:, h * hd:(h + 1) * hd] = o.astype(BF16)


def _attn_decode(dq, k16, v16, cache_k, cache_v, layer, gain, lam_init, lams, *, batch, seq,
                 heads, tk_pref=512):
    n, width = k16.shape
    hd = width // heads
    dh = hd // 2
    past = cache_k.shape[2]
    tk = _pick_tile(past, tk_pref, V7X_LANES)
    pad = V7X_LANES
    kn = jnp.pad(k16.reshape(batch, seq, width), ((0, 0), (0, pad - seq), (0, 0)))
    vn = jnp.pad(v16.reshape(batch, seq, width), ((0, 0), (0, pad - seq), (0, 0)))
    lam_specs, lam_args = _lam_inputs(lam_init, lams)
    in_specs = lam_specs + [
        pl.BlockSpec((seq, width), lambda b, j: (b, 0)),
        pl.BlockSpec((None, pad, width), lambda b, j: (b, 0, 0)),
        pl.BlockSpec((None, pad, width), lambda b, j: (b, 0, 0)),
        pl.BlockSpec((None, None, tk, heads, hd), lambda b, j: (layer, b, j, 0, 0)),
        pl.BlockSpec((None, None, tk, heads, hd), lambda b, j: (layer, b, j, 0, 0)),
        pl.BlockSpec((1, hd), lambda b, j: (0, 0)),
    ]
    scratch = [pltpu.VMEM((heads, 2 * seq, hd), BF16),
               pltpu.VMEM((heads, 2 * seq, V7X_LANES), F32),
               pltpu.VMEM((heads, 2 * seq, V7X_LANES), F32),
               pltpu.VMEM((heads, 2 * seq, hd), F32)]
    vmem = 4 * tk * width * 4 + 2 * tk * width * 2 + 8 * pad * width * 2 + (8 << 20)
    return pl.pallas_call(
        functools.partial(_attn_decode_kernel, heads=heads, dh=dh, n_new=seq),
        grid=(batch, past // tk), in_specs=in_specs,
        out_specs=pl.BlockSpec((seq, width), lambda b, j: (b, 0)),
        out_shape=jax.ShapeDtypeStruct((n, width), BF16),
        scratch_shapes=scratch, compiler_params=_params(vmem, 2), name="diff_attn_decode",
    )(*lam_args, dq, kn, vn, cache_k, cache_v, gain.reshape(1, hd))


def _ret_tables(heads, chunk, dk, dv):
    log_gamma = jnp.log1p(-jnp.exp2(-5.0 - jnp.arange(heads, dtype=F32)))
    idx = jnp.arange(chunk, dtype=F32)
    diff = idx[:, None] - idx[None, :]
    decay = jnp.where(diff >= 0, jnp.exp(log_gamma[:, None, None] * jnp.maximum(diff, 0.0)), 0.0)
    cross = jnp.exp(log_gamma[:, None] * (idx + 1.0)[None, :])
    kdec = jnp.exp(log_gamma[:, None] * (chunk - 1.0 - idx)[None, :])
    carry = jnp.exp(log_gamma * chunk)
    return (decay,
            jnp.broadcast_to(cross[:, :, None], (heads, chunk, dv)),
            jnp.broadcast_to(kdec[:, :, None], (heads, chunk, dk)),
            carry)


def _ret_kernel(*refs, has_init):
    refs = list(refs)
    carry_ref, q_ref, k_ref, v_ref, rg_ref, dec_ref, cross_ref, kdec_ref, g_ref = refs[:9]
    refs = refs[9:]
    s0_ref = refs.pop(0) if has_init else None
    o_ref, sout_ref, s_scr = refs
    heads, dk, dv = s_scr.shape

    @pl.when(pl.program_id(1) == 0)
    def _():
        s_scr[...] = s0_ref[...] if has_init else jnp.zeros(s_scr.shape, F32)

    for h in range(heads):
        ks, vs = slice(h * dk, (h + 1) * dk), slice(h * dv, (h + 1) * dv)
        q, k, v = q_ref[:, ks], k_ref[:, ks], v_ref[:, vs]
        state = s_scr[h]
        scores = lax.dot_general(q, k, NT_DIMS, preferred_element_type=F32) * dec_ref[h]
        inner = jnp.dot(scores.astype(BF16), v, preferred_element_type=F32)
        cross = jnp.dot(q, state.astype(BF16), preferred_element_type=F32) * cross_ref[h]
        k_dec = (k.astype(F32) * kdec_ref[h]).astype(BF16)
        new_state = carry_ref[h] * state + lax.dot_general(k_dec, v, TN_DIMS,
                                                           preferred_element_type=F32)
        s_scr[h] = new_state
        sout_ref[h] = new_state
        gate = rg_ref[:, vs].astype(F32)
        y = _rms_rows(inner + cross, g_ref[...]) * (gate * jax.nn.sigmoid(gate))
        o_ref[:, vs] = y.astype(BF16)


def _retention(rqk, main, c_v, c_gate, dv, gain, state0, layer, *, batch, seq, heads,
               chunk_pref=256):
    n = rqk.shape[0]
    dk = rqk.shape[1] // (2 * heads)
    assert c_v % (heads * dv) == 0 and c_gate % (heads * dv) == 0
    vb, gb = c_v // (heads * dv), c_gate // (heads * dv)
    chunk = _pick_tile(seq, chunk_pref, 16)
    nc = seq // chunk
    decay, cross, kdec, carry = _ret_tables(heads, chunk, dk, dv)
    has_init = state0 is not None
    row = lambda b, c: b * nc + c
    whole = lambda shape: pl.BlockSpec(shape, lambda b, c: (0,) * len(shape))
    in_specs = [
        pl.BlockSpec(memory_space=pltpu.SMEM),
        pl.BlockSpec((chunk, heads * dk), lambda b, c: (row(b, c), 0)),
        pl.BlockSpec((chunk, heads * dk), lambda b, c: (row(b, c), 1)),
        pl.BlockSpec((chunk, heads * dv), lambda b, c: (row(b, c), vb)),
        pl.BlockSpec((chunk, heads * dv), lambda b, c: (row(b, c), gb)),
        whole((heads, chunk, chunk)), whole((heads, chunk, dv)), whole((heads, chunk, dk)),
        whole((1, dv)),
    ]
    args = [carry, rqk, rqk, main, main, decay, cross, kdec, gain.reshape(1, dv)]
    if has_init:
        in_specs.append(pl.BlockSpec((None, None, heads, dk, dv),
                                     lambda b, c: (layer, b, 0, 0, 0)))
        args.append(state0)
    vmem = (2 * heads * chunk * (chunk + dv + dk) * 4 + 8 * chunk * heads * (dk + dv) * 2
            + 6 * heads * dk * dv * 4 + (8 << 20))
    ob, st = pl.pallas_call(
        functools.partial(_ret_kernel, has_init=has_init),
        grid=(batch, nc), in_specs=in_specs,
        out_specs=[pl.BlockSpec((chunk, heads * dv), lambda b, c: (row(b, c), 0)),
                   pl.BlockSpec((None, heads, dk, dv), lambda b, c: (b, 0, 0, 0))],
        out_shape=[jax.ShapeDtypeStruct((n, heads * dv), BF16),
                   jax.ShapeDtypeStruct((batch, heads, dk, dv), F32)],
        scratch_shapes=[pltpu.VMEM((heads, dk, dv), F32)],
        compiler_params=_params(vmem, 2), name="retention",
    )(*args)
    return ob, st


def _merge_kernel(oa_ref, ob_ref, wa_ref, wb_ref, ga_ref, gb_ref, o_ref):
    a = jnp.dot(oa_ref[...], wa_ref[...], preferred_element_type=F32)
    b = jnp.dot(ob_ref[...], wb_ref[...], preferred_element_type=F32)
    ga = jax.nn.sigmoid(ga_ref[...].astype(F32))
    gb = jax.nn.sigmoid(gb_ref[...].astype(F32))
    o_ref[...] = (ga * a + gb * b).astype(BF16)


def _merge(oa, ob, main, c_ga, c_gb, w_a, w_b, layer, *, tm_pref=1024, tn_pref=512):
    n, k = oa.shape
    d = w_a.shape[2]
    tm = _pick_tile(n, tm_pref, 16)
    tn = _pick_tile(d, tn_pref, V7X_LANES)
    assert c_ga % tn == 0 and c_gb % tn == 0
    ga0, gb0 = c_ga // tn, c_gb // tn
    vmem = 4 * tm * k * 2 + 4 * k * tn * 2 + 6 * tm * tn * 2 + 4 * tm * tn * 4 + (4 << 20)
    return pl.pallas_call(
        _merge_kernel, grid=(n // tm, d // tn),
        in_specs=[pl.BlockSpec((tm, k), lambda i, j: (i, 0)),
                  pl.BlockSpec((tm, k), lambda i, j: (i, 0)),
                  pl.BlockSpec((None, k, tn), lambda i, j: (layer, 0, j)),
                  pl.BlockSpec((None, k, tn), lambda i, j: (layer, 0, j)),
                  pl.BlockSpec((tm, tn), lambda i, j: (i, ga0 + j)),
                  pl.BlockSpec((tm, tn), lambda i, j: (i, gb0 + j))],
        out_specs=pl.BlockSpec((tm, tn), lambda i, j: (i, j)),
        out_shape=jax.ShapeDtypeStruct((n, d), BF16),
        compiler_params=_params(vmem, 2), name="branch_merge",
    )(oa, ob, w_a, w_b, main, main)


def _ffn_kernel(*refs, row_chunk, final_norm):
    refs = list(refs)
    x_ref, g_ref, wu_ref, wd_ref = refs[:4]
    fg_ref = refs[4] if final_norm else None
    o_ref, h_scr = refs[-2:]
    f = pl.program_id(1)
    n_chunks = x_ref.shape[0] // row_chunk

    @pl.when(f == 0)
    def _():
        def rows(r, carry):
            rs = pl.ds(pl.multiple_of(r * row_chunk, row_chunk), row_chunk)
            x = x_ref[rs, :]
            h_scr[rs, :] = _rms_rows(x, g_ref[...]).astype(BF16)
            o_ref[rs, :] = x
            return carry
        lax.fori_loop(0, n_chunks, rows, 0)

    u = jnp.dot(h_scr[...], wu_ref[...], preferred_element_type=F32)
    u = jnp.square(jnp.maximum(u, 0.0)).astype(BF16)
    o_ref[...] += jnp.dot(u, wd_ref[...], preferred_element_type=F32)

    if final_norm:
        @pl.when(f == pl.num_programs(1) - 1)
        def _():
            def rows(r, carry):
                rs = pl.ds(pl.multiple_of(r * row_chunk, row_chunk), row_chunk)
                o_ref[rs, :] = _rms_rows(o_ref[rs, :], fg_ref[...])
                return carry
            lax.fori_loop(0, n_chunks, rows, 0)


def _ffn(x, gain, w_up, w_down, layer, final_gain=None, *, tm_pref=1024, tf_pref=512):
    n, d = x.shape
    d_ff = w_up.shape[2]
    tm = _pick_tile(n, tm_pref, 16)
    tf = _pick_tile(d_ff, tf_pref, V7X_LANES)
    final_norm = final_gain is not None
    in_specs = [pl.BlockSpec((tm, d), lambda i, f: (i, 0)),
                pl.BlockSpec((1, d), lambda i, f: (0, 0)),
                pl.BlockSpec((None, d, tf), lambda i, f: (layer, 0, f)),
                pl.BlockSpec((None, tf, d), lambda i, f: (layer, f, 0))]
    args = [x, gain.reshape(1, d), w_up, w_down]
    if final_norm:
        in_specs.append(pl.BlockSpec((1, d), lambda i, f: (0, 0)))
        args.append(final_gain.reshape(1, d))
    vmem = 4 * tm * d * 4 + tm * d * 2 + 4 * d * tf * 2 + 2 * tm * tf * 4 + (6 << 20)
    return pl.pallas_call(
        functools.partial(_ffn_kernel, row_chunk=min(tm, 64), final_norm=final_norm),
        grid=(n // tm, d_ff // tf), in_specs=in_specs,
        out_specs=pl.BlockSpec((tm, d), lambda i, f: (i, 0)),
        out_shape=jax.ShapeDtypeStruct((n, d), F32),
        scratch_shapes=[pltpu.VMEM((tm, d), BF16)],
        compiler_params=_params(vmem, 2), name="ffn",
    )(*args)


def _rope_tables(pos, dk, batch):
    inv = 1.0 / (ROPE_BASE ** jnp.linspace(0.0, 1.0, dk // 2, dtype=F32))
    th = pos.astype(F32)[:, None] * jnp.repeat(inv, 2)[None, :]
    sin, cos = jnp.sin(th), jnp.cos(th)
    even = (jnp.arange(dk) % 2) == 0
    sin_next = jnp.where(even, -sin, 0.0)
    sin_prev = jnp.where(even, 0.0, sin)
    return tuple(jnp.tile(t, (batch, 1)) for t in (cos, sin_next, sin_prev))


def _cast_w_in(w_in, diff_w, q_scale):
    col_scale = jnp.where(jnp.arange(w_in.shape[-1]) < diff_w, q_scale, 1.0).astype(F32)
    return (w_in * col_scale).astype(BF16)


def _mixer(x, layer, w, p, rope, past, kv_stack, *, depth, batch, seq, heads, ret_heads):
    d = x.shape[1]
    hd = p["subln_g"].shape[-1]
    diff_w = heads * hd
    dk = p["state"].shape[-2]
    dv = p["state"].shape[-1]
    qk_w = ret_heads * dk
    ret_w = ret_heads * dv
    c_k, c_v, c_rqk = diff_w, 2 * diff_w, 3 * diff_w
    skipped = 2 * diff_w + 2 * qk_w
    c_rv, c_rg, c_ga, c_gb = diff_w, diff_w + ret_w, diff_w + 2 * ret_w, diff_w + 2 * ret_w + d
    main_w = c_gb + d

    main, h = _proj(x, w["in"], layer, 0, main_w, gain=p["norm_mix_g"], emit_h=True,
                    gap=(diff_w, skipped))
    k32, k16 = _proj_heads(h, w["in"], layer, c_k, heads, kv_stack[0], depth)
    v32, v16 = _proj_heads(h, w["in"], layer, c_v, heads, kv_stack[1], depth)
    rqk = _proj(h, w["in"], layer, c_rqk, 2 * qk_w, epi="rope", extra=rope,
                rope_scale=dk ** -0.5, tn_pref=qk_w)

    lam_init = 0.8 - 0.6 * math.exp(-0.3 * layer)
    lams = (p["lq1"], p["lk1"], p["lq2"], p["lk2"])
    if past:
        oa = _attn_decode(main, k16, v16, p["cache_k"], p["cache_v"], layer, p["subln_g"],
                          lam_init, lams, batch=batch, seq=seq, heads=heads)
        state0 = p["state"]
    else:
        oa = _attn_prompt(main, k16, v16, p["subln_g"], lam_init, lams,
                          batch=batch, seq=seq, heads=heads)
        state0 = None
    ob, st = _retention(rqk, main, c_rv, c_rg, dv, p["ret_g"], state0, layer,
                        batch=batch, seq=seq, heads=ret_heads)
    merged = _merge(oa, ob, main, c_ga, c_gb, w["a"], w["b"], layer)
    x = _proj(merged, w["out"], layer, 0, d, epi="residual", extra=(x,))
    return x, (k32, v32), st


def kernel(x_prompt, x_sample, cache_diff_k, cache_diff_v, state_ret, norm_mix_g, w_in,
           lambda_q1, lambda_k1, lambda_q2, lambda_k2, diff_subln_g, ret_norm_g,
           w_branch_a, w_branch_b, w_out, norm_ffn_g, w_up, w_down, final_norm_g):
    depth, dec_batch, past_len, heads, hd = cache_diff_k.shape
    batch, seq, d = x_prompt.shape
    dec_seq = x_sample.shape[1]
    ret_heads, dk, dv = state_ret.shape[2:]

    w = {"in": _cast_w_in(w_in, heads * hd, (hd // 2) ** -0.5 * LOG2E),
         "a": w_branch_a.astype(BF16), "b": w_branch_b.astype(BF16),
         "out": w_out.astype(BF16), "up": w_up.astype(BF16), "down": w_down.astype(BF16)}
    cache_k, cache_v = cache_diff_k, cache_diff_v
    rope_p = _rope_tables(jnp.arange(seq), dk, batch)
    rope_s = _rope_tables(past_len + jnp.arange(dec_seq), dk, dec_batch)

    xp = x_prompt.reshape(batch * seq, d)
    xs = x_sample.reshape(dec_batch * dec_seq, d)
    kv_p, kv_s = (None, None), (None, None)
    states_p, states_s = [], []
    for l in range(depth):
        p = {"norm_mix_g": norm_mix_g[l], "lq1": lambda_q1[l], "lk1": lambda_k1[l],
             "lq2": lambda_q2[l], "lk2": lambda_k2[l], "subln_g": diff_subln_g[l],
             "ret_g": ret_norm_g[l], "cache_k": cache_k, "cache_v": cache_v, "state": state_ret}
        final_g = final_norm_g if l == depth - 1 else None
        xp, kv_p, sp = _mixer(xp, l, w, p, rope_p, False, kv_p, depth=depth, batch=batch,
                              seq=seq, heads=heads, ret_heads=ret_heads)
        xs, kv_s, ss = _mixer(xs, l, w, p, rope_s, True, kv_s, depth=depth, batch=dec_batch,
                              seq=dec_seq, heads=heads, ret_heads=ret_heads)
        xp = _ffn(xp, norm_ffn_g[l], w["up"], w["down"], l, final_g)
        xs = _ffn(xs, norm_ffn_g[l], w["up"], w["down"], l, final_g)
        states_p.append(sp)
        states_s.append(ss)
    return (xp.reshape(batch, seq, d), xs.reshape(dec_batch, dec_seq, d),
            kv_p[0].reshape(depth, batch, seq, heads, hd),
            kv_p[1].reshape(depth, batch, seq, heads, hd), jnp.stack(states_p),
            kv_s[0].reshape(depth, dec_batch, dec_seq, heads, hd),
            kv_s[1].reshape(depth, dec_batch, dec_seq, heads, hd), jnp.stack(states_s))
```

```python
import functools
import math

import jax
import jax.numpy as jnp
from jax import lax
from jax.experimental import pallas as pl
from jax.experimental.pallas import tpu as pltpu

F32 = jnp.float32
BF16 = jnp.bfloat16

NORM_EPS = 1e-6
ROPE_BASE = 10000.0
MASK_CHUNK = 64
NEG_INF = -1e30
SHIFT_LIMIT = 48.0
NORM_SLACK = 1.02
LOG2E = 1.4426950408889634

V7X_VMEM_BYTES = 64 * 1024 * 1024
V7X_LANES = 128
VMEM_REQUEST_CAP = V7X_VMEM_BYTES - 8 * 1024 * 1024

NT_DIMS = (((1,), (1,)), ((), ()))
TN_DIMS = (((0,), (0,)), ((), ()))


def _pick_tile(n, pref, align):
    if n <= pref:
        return n
    t = pref - pref % align
    while t >= align:
        if n % t == 0:
            return t
        t -= align
    raise ValueError(f"no tile for {n} (pref {pref}, align {align})")


def _params(vmem_bytes, n_grid):
    return pltpu.CompilerParams(
        dimension_semantics=("arbitrary",) * n_grid,
        vmem_limit_bytes=int(min(VMEM_REQUEST_CAP, vmem_bytes)))


def _rms_rows(x, gain):
    ms = jnp.mean(x * x, axis=-1, keepdims=True)
    return x * lax.rsqrt(ms + NORM_EPS) * gain


def _proj_kernel(*refs, has_norm, emit_h, epi, row_chunk, rope_scale):
    refs = list(refs)
    x_ref = refs.pop(0)
    g_ref = refs.pop(0) if has_norm else None
    w_ref = refs.pop(0)
    if epi == "rope":
        cos_ref, sina_ref, sinb_ref = refs.pop(0), refs.pop(0), refs.pop(0)
    if epi == "residual":
        res_ref = refs.pop(0)
    o_ref = refs.pop(0)
    hout_ref = refs.pop(0) if emit_h else None
    h_scr = refs.pop(0) if has_norm else None

    j = pl.program_id(1)
    if has_norm:
        @pl.when(j == 0)
        def _():
            def rows(r, carry):
                rs = pl.ds(pl.multiple_of(r * row_chunk, row_chunk), row_chunk)
                hrow = _rms_rows(x_ref[rs, :], g_ref[...]).astype(BF16)
                h_scr[rs, :] = hrow
                if emit_h:
                    hout_ref[rs, :] = hrow
                return carry
            lax.fori_loop(0, x_ref.shape[0] // row_chunk, rows, 0)
        lhs = h_scr[...]
    else:
        lhs = x_ref[...]

    acc = jnp.dot(lhs, w_ref[...], preferred_element_type=F32)
    if epi == "bf16":
        o_ref[...] = acc.astype(BF16)
    elif epi == "residual":
        o_ref[...] = res_ref[...] + acc
    elif epi == "rope":
        n_tiles = pl.num_programs(1)
        scale = jnp.where(j < n_tiles // 2, 1.0, rope_scale).astype(F32)
        cos, sina, sinb = cos_ref[...], sina_ref[...], sinb_ref[...]
        for c in range(acc.shape[1] // V7X_LANES):
            cs = slice(c * V7X_LANES, (c + 1) * V7X_LANES)
            xs = acc[:, cs]
            nxt = pltpu.roll(xs, V7X_LANES - 1, 1)
            prv = pltpu.roll(xs, 1, 1)
            o_ref[:, cs] = ((xs * cos + nxt * sina + prv * sinb) * scale).astype(BF16)
    else:
        raise ValueError(epi)


def _proj(lhs, w, layer, col0, ncols, *, gain=None, emit_h=False, epi="bf16", extra=(),
          rope_scale=1.0, gap=(0, 0), tm_pref=1024, tn_pref=1024):
    n, k = lhs.shape
    has_norm = gain is not None
    tm = _pick_tile(n, tm_pref, 16)
    tn = _pick_tile(ncols, tn_pref, V7X_LANES)
    assert col0 % tn == 0 and gap[0] % tn == 0 and gap[1] % tn == 0
    cb = col0 // tn
    lead_tiles, skip_tiles = gap[0] // tn, gap[1] // tn
    grid = (n // tm, ncols // tn)

    def w_tile(j):
        return cb + j + jnp.where(j >= lead_tiles, skip_tiles, 0) if skip_tiles else cb + j

    in_specs = [pl.BlockSpec((tm, k), lambda i, j: (i, 0))]
    args = [lhs]
    if has_norm:
        in_specs.append(pl.BlockSpec((1, k), lambda i, j: (0, 0)))
        args.append(gain.reshape(1, k))
    in_specs.append(pl.BlockSpec((None, k, tn), lambda i, j: (layer, 0, w_tile(j))))
    args.append(w)
    if epi == "rope":
        for t in extra:
            in_specs.append(pl.BlockSpec((tm, V7X_LANES), lambda i, j: (i, 0)))
            args.append(t)
    elif epi == "residual":
        in_specs.append(pl.BlockSpec((tm, tn), lambda i, j: (i, j)))
        args.append(extra[0])

    tile_spec = pl.BlockSpec((tm, tn), lambda i, j: (i, j))
    out_dtype = F32 if epi == "residual" else BF16
    out_shape = [jax.ShapeDtypeStruct((n, ncols), out_dtype)]
    out_specs = [tile_spec]
    out_bytes = tm * tn * jnp.dtype(out_dtype).itemsize
    if emit_h:
        out_shape.append(jax.ShapeDtypeStruct((n, k), BF16))
        out_specs.append(pl.BlockSpec((tm, k), lambda i, j: (i, 0)))
        out_bytes += tm * k * 2
    scratch = [pltpu.VMEM((tm, k), BF16)] if has_norm else []

    vmem = (2 * tm * k * lhs.dtype.itemsize + 2 * k * tn * 2 + 2 * out_bytes
            + (tm * k * 2 if has_norm else 0) + 3 * tm * tn * 4
            + (2 * tm * tn * 4 if epi == "residual" else 0) + (4 << 20))
    outs = pl.pallas_call(
        functools.partial(_proj_kernel, has_norm=has_norm, emit_h=emit_h, epi=epi,
                          row_chunk=min(tm, 64), rope_scale=rope_scale),
        grid=grid, in_specs=in_specs, out_specs=out_specs, out_shape=out_shape,
        scratch_shapes=scratch, compiler_params=_params(vmem, 2),
        name=f"proj_{epi}")(*args)
    return outs[0] if len(outs) == 1 else tuple(outs)


def _proj_heads_kernel(*refs, aliased):
    x_ref, w_ref = refs[0], refs[1]
    o32_ref, o16_ref = refs[-2], refs[-1]
    acc = jnp.dot(x_ref[...], w_ref[...], preferred_element_type=F32)
    o16_ref[...] = acc.astype(BF16)
    o32_ref[...] = acc.reshape(o32_ref.shape)


def _proj_heads(lhs, w, layer, col0, heads, stacked, depth, *, tm_pref=512):
    n, k = lhs.shape
    ncols = k
    hd = ncols // heads
    tm = _pick_tile(n, tm_pref, 16)
    cb = col0 // ncols
    assert col0 % ncols == 0
    aliased = stacked is not None
    in_specs = [pl.BlockSpec((tm, k), lambda i: (i, 0)),
                pl.BlockSpec((None, k, ncols), lambda i: (layer, 0, cb))]
    args = [lhs, w]
    if aliased:
        in_specs.append(pl.BlockSpec(memory_space=pl.ANY))
        args.append(stacked)
    vmem = 4 * tm * k * 2 + 4 * k * ncols * 2 + 4 * tm * ncols * 4 + 4 * tm * ncols * 2 + (4 << 20)
    return pl.pallas_call(
        functools.partial(_proj_heads_kernel, aliased=aliased),
        grid=(n // tm,), in_specs=in_specs,
        out_specs=[pl.BlockSpec((None, tm, heads, hd), lambda i: (layer, i, 0, 0)),
                   pl.BlockSpec((tm, ncols), lambda i: (i, 0))],
        out_shape=[jax.ShapeDtypeStruct((depth, n, heads, hd), F32),
                   jax.ShapeDtypeStruct((n, ncols), BF16)],
        input_output_aliases={2: 0} if aliased else {},
        compiler_params=_params(vmem, 1), name="proj_heads")(*args)


def _lam_value(lamc_ref, lq1_ref, lk1_ref, lq2_ref, lk2_ref):
    a = jnp.exp(jnp.sum(lq1_ref[...] * lk1_ref[...], axis=-1, keepdims=True))
    b = jnp.exp(jnp.sum(lq2_ref[...] * lk2_ref[...], axis=-1, keepdims=True))
    return a - b + lamc_ref[0]


def _lane_repeat(x, reps):
    return x if reps == 1 else jnp.concatenate([x] * reps, axis=1)


def _softmax_update(s, v, m_prev, l_prev, acc_prev):
    keys = s.shape[1]
    m_new = jnp.maximum(m_prev, jnp.max(s, axis=1, keepdims=True))
    alpha = jnp.exp2(m_prev - m_new)
    p = jnp.exp2(s - _lane_repeat(m_new, keys // V7X_LANES))
    l_new = alpha * l_prev + jnp.sum(p, axis=1, keepdims=True)
    pv = jnp.dot(p.astype(BF16), v, preferred_element_type=F32)
    acc_new = acc_prev * _lane_repeat(alpha, acc_prev.shape[1] // V7X_LANES) + pv
    return m_new, l_new, acc_new


def _softmax_step(s, v, m_ref, l_ref, acc_ref):
    m_ref[...], l_ref[...], acc_ref[...] = _softmax_update(s, v, m_ref[...], l_ref[...],
                                                           acc_ref[...])


def _diff_finish(a1, l1, a2, l2, lam, gain, out_scale):
    rep = a1.shape[1] // V7X_LANES
    o = a1 * _lane_repeat(1.0 / l1, rep) - lam * (a2 * _lane_repeat(1.0 / l2, rep))
    return _rms_rows(o, gain) * out_scale


def _attn_prompt_kernel(lamc_ref, lq1_ref, lk1_ref, lq2_ref, lk2_ref, q_ref, k_ref, v_ref, g_ref,
                        o_ref, m1, l1, a1, m2, l2, a2, sa1, sa2, sb1, sb2, ka1, ka2, ext, *, dh, tq):
    seq = q_ref.shape[0]
    tk = tq
    nq = seq // tq
    n_units = nq * (nq + 1) // 2
    shift = MASK_CHUNK.bit_length() - 1
    assert tq // MASK_CHUNK < dh
    maps = ((0, m1, l1, a1, ka1), (dh, m2, l2, a2, ka2))

    lane = lax.broadcasted_iota(jnp.int32, (tk, dh), 1)
    row_chunk = lax.shift_right_logical(lax.broadcasted_iota(jnp.int32, (tk, dh), 0), shift)
    key_ext = jnp.where(lane == 0, -1.0, jnp.where(lane - 1 == row_chunk, 1.0, 0.0)).astype(BF16)
    hidden = (lane >= 1) & (lane <= tq // MASK_CHUNK) & (lane - 1 > row_chunk)

    ones = jnp.ones((dh, dh), BF16)

    def widen(c, carry):
        rs = pl.ds(pl.multiple_of(c * tk, tk), tk)
        out = []
        for idx, (lo, _, l_ref, a_ref, ka) in enumerate(maps):
            l_ref[rs, :] = jnp.zeros((tk, V7X_LANES), F32)
            a_ref[rs, :] = jnp.zeros((tk, 2 * dh), F32)
            kc = k_ref[rs, lo:lo + dh]
            qc = q_ref[rs, lo:lo + dh]
            ka[rs, 0:dh] = kc
            ka[rs, dh:2 * dh] = key_ext
            k2 = jnp.dot(kc * kc, ones, preferred_element_type=F32)
            q2 = jnp.dot(qc * qc, ones, preferred_element_type=F32)
            out += [jnp.maximum(carry[2 * idx], jnp.max(k2, axis=0, keepdims=True)),
                    jnp.maximum(carry[2 * idx + 1], jnp.max(q2, axis=0, keepdims=True))]
        return tuple(out)
    norms2 = lax.fori_loop(0, nq, widen, (jnp.zeros((1, dh), F32),) * 4)

    worst = jnp.zeros((1, 1), F32)
    for idx in range(2):
        bound = jnp.sqrt(norms2[2 * idx][:, 0:1] * norms2[2 * idx + 1][:, 0:1]) * NORM_SLACK
        worst = jnp.maximum(worst, bound)
        base = jnp.where(lane == 0, bound, 0.0)
        ext[(2 * idx) * tq:(2 * idx + 1) * tq, :] = base.astype(BF16)
        ext[(2 * idx + 1) * tq:(2 * idx + 2) * tq, :] = jnp.where(hidden, NEG_INF, base).astype(BF16)
    fast = jnp.max(worst) <= SHIFT_LIMIT

    lam = _lam_value(lamc_ref, lq1_ref, lk1_ref, lq2_ref, lk2_ref)
    gain, out_scale = g_ref[...], lamc_ref[1]

    def q_rows(qi):
        return pl.ds(pl.multiple_of(qi * tq, tq), tq)

    def k_rows(j):
        return pl.ds(pl.multiple_of(j * tk, tk), tk)

    def scores(qi, j, bufs):
        diag = jnp.where(qi == j, 1, 0)
        for idx, ((lo, _, _, _, ka), s_scr) in enumerate(zip(maps, bufs)):
            e_rows = pl.ds(pl.multiple_of((2 * idx + diag) * tq, tq), tq)
            qa = jnp.concatenate([q_ref[q_rows(qi), lo:lo + dh], ext[e_rows, :]], axis=1)
            s_scr[...] = lax.dot_general(qa, ka[k_rows(j), :], NT_DIMS,
                                         preferred_element_type=F32)

    def consume(qi, j, bufs, exact):
        v = v_ref[k_rows(j), :]
        rq = q_rows(qi)
        for (_, m_ref, l_ref, a_ref, _), s_scr in zip(maps, bufs):
            s = s_scr[...]
            l_prev, a_prev = l_ref[rq, :], a_ref[rq, :]
            if exact:
                m_ref[rq, :], l_new, a_new = _softmax_update(s, v, m_ref[rq, :], l_prev, a_prev)
            else:
                p = jnp.exp2(s)
                l_new = l_prev + jnp.sum(p, axis=1, keepdims=True)
                a_new = a_prev + jnp.dot(p.astype(BF16), v, preferred_element_type=F32)
            l_ref[rq, :], a_ref[rq, :] = l_new, a_new

    def advance(qi, j):
        last = j == qi
        return jnp.where(last, qi + 1, qi), jnp.where(last, 0, j + 1)

    buf_a, buf_b = (sa1, sa2), (sb1, sb2)

    def sweep(exact):
        if exact:
            for m_ref in (m1, m2):
                m_ref[...] = jnp.full(m_ref.shape, -jnp.inf, F32)
        scores(0, 0, buf_a)

        def unit_pair(t, carry):
            qi, j = carry
            qi1, j1 = advance(qi, j)
            qi2, j2 = advance(qi1, j1)
            over = qi2 >= nq
            qi2c, j2c = jnp.where(over, nq - 1, qi2), jnp.where(over, 0, j2)
            scores(qi1, j1, buf_b)
            consume(qi, j, buf_a, exact)
            scores(qi2c, j2c, buf_a)
            consume(qi1, j1, buf_b, exact)
            return qi2, j2
        lax.fori_loop(0, n_units // 2, unit_pair, (jnp.int32(0), jnp.int32(0)))
        if n_units % 2:
            consume(nq - 1, nq - 1, buf_a, exact)

    @pl.when(fast)
    def _():
        sweep(False)

    @pl.when(jnp.logical_not(fast))
    def _():
        sweep(True)

    def finish(qi, carry):
        rq = q_rows(qi)
        o = _diff_finish(a1[rq, :], l1[rq, :], a2[rq, :], l2[rq, :], lam, gain, out_scale)
        o_ref[rq, :] = o.astype(BF16)
        return carry
    lax.fori_loop(0, nq, finish, 0)


def _lam_inputs(lam_init, lams):
    specs = [pl.BlockSpec(memory_space=pltpu.SMEM)]
    args = [jnp.array([lam_init, 1.0 - lam_init], F32)]
    for v in lams:
        specs.append(pl.BlockSpec((1, v.shape[-1]), lambda *_: (0, 0)))
        args.append(v.reshape(1, -1))
    return specs, args


def _attn_prompt(dq, k16, v16, gain, lam_init, lams, *, batch, seq, heads, tq_pref=512):
    n, width = k16.shape
    hd = width // heads
    dh = hd // 2
    tq = _pick_tile(seq, tq_pref, V7X_LANES)
    lam_specs, lam_args = _lam_inputs(lam_init, lams)
    per_head = pl.BlockSpec((seq, hd), lambda b, h: (b, h))
    in_specs = lam_specs + [per_head, per_head, per_head,
                            pl.BlockSpec((1, hd), lambda b, h: (0, 0))]
    stat = pltpu.VMEM((seq, V7X_LANES), F32)
    accs = pltpu.VMEM((seq, hd), F32)
    sbuf = pltpu.VMEM((tq, tq), F32)
    wide_k = pltpu.VMEM((seq, hd), BF16)
    vmem = 8 * seq * hd * 2 + 2 * seq * hd * 2 + 2 * seq * hd * 4 + 4 * seq * V7X_LANES * 4 \
        + 12 * tq * tq * 4 + (4 << 20)
    return pl.pallas_call(
        functools.partial(_attn_prompt_kernel, dh=dh, tq=tq),
        grid=(batch, heads), in_specs=in_specs, out_specs=per_head,
        out_shape=jax.ShapeDtypeStruct((n, width), BF16),
        scratch_shapes=[stat, stat, accs, stat, stat, accs, sbuf, sbuf, sbuf, sbuf,
                        wide_k, wide_k, pltpu.VMEM((4 * tq, dh), BF16)],
        compiler_params=_params(vmem, 2), name="diff_attn_prompt",
    )(*lam_args, dq, k16, v16, gain.reshape(1, hd))


def _attn_decode_kernel(lamc_ref, lq1_ref, lk1_ref, lq2_ref, lk2_ref, q_ref, kn_ref, vn_ref,
                        kc_ref, vc_ref, g_ref, o_ref, qd, m_scr, l_scr, a_scr,
                        *, heads, dh, n_new):
    j = pl.program_id(1)
    hd = 2 * dh
    s_q = q_ref.shape[0]

    @pl.when(j == 0)
    def _():
        qd[...] = jnp.zeros(qd.shape, BF16)
        m_scr[...] = jnp.full(m_scr.shape, -jnp.inf, F32)
        l_scr[...] = jnp.zeros(l_scr.shape, F32)
        a_scr[...] = jnp.zeros(a_scr.shape, F32)
        new_mask = lax.broadcasted_iota(jnp.int32, (2 * s_q, kn_ref.shape[0]), 1) < n_new
        for h in range(heads):
            hs = slice(h * hd, (h + 1) * hd)
            qd[h, 0:s_q, 0:dh] = q_ref[:, h * hd:h * hd + dh]
            qd[h, s_q:2 * s_q, dh:hd] = q_ref[:, h * hd + dh:(h + 1) * hd]
            s = lax.dot_general(qd[h], kn_ref[:, hs], NT_DIMS, preferred_element_type=F32)
            s = jnp.where(new_mask, s, NEG_INF)
            _softmax_step(s, vn_ref[:, hs], m_scr.at[h], l_scr.at[h], a_scr.at[h])

    tk = kc_ref.shape[0]
    k_all = kc_ref[...].astype(BF16).reshape(tk, heads * hd)
    v_all = vc_ref[...].astype(BF16).reshape(tk, heads * hd)
    scores = [lax.dot_general(qd[h], k_all[:, h * hd:(h + 1) * hd], NT_DIMS,
                              preferred_element_type=F32) for h in range(heads)]
    for h in range(heads):
        _softmax_step(scores[h], v_all[:, h * hd:(h + 1) * hd], m_scr.at[h], l_scr.at[h],
                      a_scr.at[h])

    @pl.when(j == pl.num_programs(1) - 1)
    def _():
        lam = _lam_value(lamc_ref, lq1_ref, lk1_ref, lq2_ref, lk2_ref)
        for h in range(heads):
            a, l = a_scr[h], l_scr[h]
            o = _diff_finish(a[:s_q], l[:s_q], a[s_q:], l[s_q:], lam, g_ref[...], lamc_ref[1])
            o_ref[:, h * hd:(h + 1) * hd] = o.astype(BF16)


def _attn_decode(dq, k16, v16, cache_k, cache_v, layer, gain, lam_init, lams, *, batch, seq,
                 heads, tk_pref=512):
    n, width = k16.shape
    hd = width // heads
    dh = hd // 2
    past = cache_k.shape[2]
    tk = _pick_tile(past, tk_pref, V7X_LANES)
    pad = V7X_LANES
    kn = jnp.pad(k16.reshape(batch, seq, width), ((0, 0), (0, pad - seq), (0, 0)))
    vn = jnp.pad(v16.reshape(batch, seq, width), ((0, 0), (0, pad - seq), (0, 0)))
    lam_specs, lam_args = _lam_inputs(lam_init, lams)
    in_specs = lam_specs + [
        pl.BlockSpec((seq, width), lambda b, j: (b, 0)),
        pl.BlockSpec((None, pad, width), lambda b, j: (b, 0, 0)),
        pl.BlockSpec((None, pad, width), lambda b, j: (b, 0, 0)),
        pl.BlockSpec((None, None, tk, heads, hd), lambda b, j: (layer, b, j, 0, 0)),
        pl.BlockSpec((None, None, tk, heads, hd), lambda b, j: (layer, b, j, 0, 0)),
        pl.BlockSpec((1, hd), lambda b, j: (0, 0)),
    ]
    scratch = [pltpu.VMEM((heads, 2 * seq, hd), BF16),
               pltpu.VMEM((heads, 2 * seq, V7X_LANES), F32),
               pltpu.VMEM((heads, 2 * seq, V7X_LANES), F32),
               pltpu.VMEM((heads, 2 * seq, hd), F32)]
    vmem = 4 * tk * width * 4 + 2 * tk * width * 2 + 8 * pad * width * 2 + (8 << 20)
    return pl.pallas_call(
        functools.partial(_attn_decode_kernel, heads=heads, dh=dh, n_new=seq),
        grid=(batch, past // tk), in_specs=in_specs,
        out_specs=pl.BlockSpec((seq, width), lambda b, j: (b, 0)),
        out_shape=jax.ShapeDtypeStruct((n, width), BF16),
        scratch_shapes=scratch, compiler_params=_params(vmem, 2), name="diff_attn_decode",
    )(*lam_args, dq, kn, vn, cache_k, cache_v, gain.reshape(1, hd))


def _ret_tables(heads, chunk, dk, dv):
    log_gamma = jnp.log1p(-jnp.exp2(-5.0 - jnp.arange(heads, dtype=F32)))
    idx = jnp.arange(chunk, dtype=F32)
    diff = idx[:, None] - idx[None, :]
    decay = jnp.where(diff >= 0, jnp.exp(log_gamma[:, None, None] * jnp.maximum(diff, 0.0)), 0.0)
    cross = jnp.exp(log_gamma[:, None] * (idx + 1.0)[None, :])
    kdec = jnp.exp(log_gamma[:, None] * (chunk - 1.0 - idx)[None, :])
    carry = jnp.exp(log_gamma * chunk)
    return (decay,
            jnp.broadcast_to(cross[:, :, None], (heads, chunk, dv)),
            jnp.broadcast_to(kdec[:, :, None], (heads, chunk, dk)),
            carry)


def _ret_kernel(*refs, has_init):
    refs = list(refs)
    carry_ref, q_ref, k_ref, v_ref, rg_ref, dec_ref, cross_ref, kdec_ref, g_ref = refs[:9]
    refs = refs[9:]
    s0_ref = refs.pop(0) if has_init else None
    o_ref, sout_ref, s_scr = refs
    heads, dk, dv = s_scr.shape

    @pl.when(pl.program_id(1) == 0)
    def _():
        s_scr[...] = s0_ref[...] if has_init else jnp.zeros(s_scr.shape, F32)

    for h in range(heads):
        ks, vs = slice(h * dk, (h + 1) * dk), slice(h * dv, (h + 1) * dv)
        q, k, v = q_ref[:, ks], k_ref[:, ks], v_ref[:, vs]
        state = s_scr[h]
        scores = lax.dot_general(q, k, NT_DIMS, preferred_element_type=F32) * dec_ref[h]
        inner = jnp.dot(scores.astype(BF16), v, preferred_element_type=F32)
        cross = jnp.dot(q, state.astype(BF16), preferred_element_type=F32) * cross_ref[h]
        k_dec = (k.astype(F32) * kdec_ref[h]).astype(BF16)
        new_state = carry_ref[h] * state + lax.dot_general(k_dec, v, TN_DIMS,
                                                           preferred_element_type=F32)
        s_scr[h] = new_state
        sout_ref[h] = new_state
        gate = rg_ref[:, vs].astype(F32)
        y = _rms_rows(inner + cross, g_ref[...]) * (gate * jax.nn.sigmoid(gate))
        o_ref[:, vs] = y.astype(BF16)


def _retention(rqk, main, c_v, c_gate, dv, gain, state0, layer, *, batch, seq, heads,
               chunk_pref=256):
    n = rqk.shape[0]
    dk = rqk.shape[1] // (2 * heads)
    assert c_v % (heads * dv) == 0 and c_gate % (heads * dv) == 0
    vb, gb = c_v // (heads * dv), c_gate // (heads * dv)
    chunk = _pick_tile(seq, chunk_pref, 16)
    nc = seq // chunk
    decay, cross, kdec, carry = _ret_tables(heads, chunk, dk, dv)
    has_init = state0 is not None
    row = lambda b, c: b * nc + c
    whole = lambda shape: pl.BlockSpec(shape, lambda b, c: (0,) * len(shape))
    in_specs = [
        pl.BlockSpec(memory_space=pltpu.SMEM),
        pl.BlockSpec((chunk, heads * dk), lambda b, c: (row(b, c), 0)),
        pl.BlockSpec((chunk, heads * dk), lambda b, c: (row(b, c), 1)),
        pl.BlockSpec((chunk, heads * dv), lambda b, c: (row(b, c), vb)),
        pl.BlockSpec((chunk, heads * dv), lambda b, c: (row(b, c), gb)),
        whole((heads, chunk, chunk)), whole((heads, chunk, dv)), whole((heads, chunk, dk)),
        whole((1, dv)),
    ]
    args = [carry, rqk, rqk, main, main, decay, cross, kdec, gain.reshape(1, dv)]
    if has_init:
        in_specs.append(pl.BlockSpec((None, None, heads, dk, dv),
                                     lambda b, c: (layer, b, 0, 0, 0)))
        args.append(state0)
    vmem = (2 * heads * chunk * (chunk + dv + dk) * 4 + 8 * chunk * heads * (dk + dv) * 2
            + 6 * heads * dk * dv * 4 + (8 << 20))
    ob, st = pl.pallas_call(
        functools.partial(_ret_kernel, has_init=has_init),
        grid=(batch, nc), in_specs=in_specs,
        out_specs=[pl.BlockSpec((chunk, heads * dv), lambda b, c: (row(b, c), 0)),
                   pl.BlockSpec((None, heads, dk, dv), lambda b, c: (b, 0, 0, 0))],
        out_shape=[jax.ShapeDtypeStruct((n, heads * dv), BF16),
                   jax.ShapeDtypeStruct((batch, heads, dk, dv), F32)],
        scratch_shapes=[pltpu.VMEM((heads, dk, dv), F32)],
        compiler_params=_params(vmem, 2), name="retention",
    )(*args)
    return ob, st


def _merge_kernel(oa_ref, ob_ref, wa_ref, wb_ref, ga_ref, gb_ref, o_ref):
    a = jnp.dot(oa_ref[...], wa_ref[...], preferred_element_type=F32)
    b = jnp.dot(ob_ref[...], wb_ref[...], preferred_element_type=F32)
    ga = jax.nn.sigmoid(ga_ref[...].astype(F32))
    gb = jax.nn.sigmoid(gb_ref[...].astype(F32))
    o_ref[...] = (ga * a + gb * b).astype(BF16)


def _merge(oa, ob, main, c_ga, c_gb, w_a, w_b, layer, *, tm_pref=1024, tn_pref=512):
    n, k = oa.shape
    d = w_a.shape[2]
    tm = _pick_tile(n, tm_pref, 16)
    tn = _pick_tile(d, tn_pref, V7X_LANES)
    assert c_ga % tn == 0 and c_gb % tn == 0
    ga0, gb0 = c_ga // tn, c_gb // tn
    vmem = 4 * tm * k * 2 + 4 * k * tn * 2 + 6 * tm * tn * 2 + 4 * tm * tn * 4 + (4 << 20)
    return pl.pallas_call(
        _merge_kernel, grid=(n // tm, d // tn),
        in_specs=[pl.BlockSpec((tm, k), lambda i, j: (i, 0)),
                  pl.BlockSpec((tm, k), lambda i, j: (i, 0)),
                  pl.BlockSpec((None, k, tn), lambda i, j: (layer, 0, j)),
                  pl.BlockSpec((None, k, tn), lambda i, j: (layer, 0, j)),
                  pl.BlockSpec((tm, tn), lambda i, j: (i, ga0 + j)),
                  pl.BlockSpec((tm, tn), lambda i, j: (i, gb0 + j))],
        out_specs=pl.BlockSpec((tm, tn), lambda i, j: (i, j)),
        out_shape=jax.ShapeDtypeStruct((n, d), BF16),
        compiler_params=_params(vmem, 2), name="branch_merge",
    )(oa, ob, w_a, w_b, main, main)


def _ffn_kernel(*refs, row_chunk, final_norm):
    refs = list(refs)
    x_ref, g_ref, wu_ref, wd_ref = refs[:4]
    fg_ref = refs[4] if final_norm else None
    o_ref, h_scr = refs[-2:]
    f = pl.program_id(1)
    n_chunks = x_ref.shape[0] // row_chunk

    @pl.when(f == 0)
    def _():
        def rows(r, carry):
            rs = pl.ds(pl.multiple_of(r * row_chunk, row_chunk), row_chunk)
            x = x_ref[rs, :]
            h_scr[rs, :] = _rms_rows(x, g_ref[...]).astype(BF16)
            o_ref[rs, :] = x
            return carry
        lax.fori_loop(0, n_chunks, rows, 0)

    u = jnp.dot(h_scr[...], wu_ref[...], preferred_element_type=F32)
    u = jnp.square(jnp.maximum(u, 0.0)).astype(BF16)
    o_ref[...] += jnp.dot(u, wd_ref[...], preferred_element_type=F32)

    if final_norm:
        @pl.when(f == pl.num_programs(1) - 1)
        def _():
            def rows(r, carry):
                rs = pl.ds(pl.multiple_of(r * row_chunk, row_chunk), row_chunk)
                o_ref[rs, :] = _rms_rows(o_ref[rs, :], fg_ref[...])
                return carry
            lax.fori_loop(0, n_chunks, rows, 0)


def _ffn(x, gain, w_up, w_down, layer, final_gain=None, *, tm_pref=1024, tf_pref=512):
    n, d = x.shape
    d_ff = w_up.shape[2]
    tm = _pick_tile(n, tm_pref, 16)
    tf = _pick_tile(d_ff, tf_pref, V7X_LANES)
    final_norm = final_gain is not None
    in_specs = [pl.BlockSpec((tm, d), lambda i, f: (i, 0)),
                pl.BlockSpec((1, d), lambda i, f: (0, 0)),
                pl.BlockSpec((None, d, tf), lambda i, f: (layer, 0, f)),
                pl.BlockSpec((None, tf, d), lambda i, f: (layer, f, 0))]
    args = [x, gain.reshape(1, d), w_up, w_down]
    if final_norm:
        in_specs.append(pl.BlockSpec((1, d), lambda i, f: (0, 0)))
        args.append(final_gain.reshape(1, d))
    vmem = 4 * tm * d * 4 + tm * d * 2 + 4 * d * tf * 2 + 2 * tm * tf * 4 + (6 << 20)
    return pl.pallas_call(
        functools.partial(_ffn_kernel, row_chunk=min(tm, 64), final_norm=final_norm),
        grid=(n // tm, d_ff // tf), in_specs=in_specs,
        out_specs=pl.BlockSpec((tm, d), lambda i, f: (i, 0)),
        out_shape=jax.ShapeDtypeStruct((n, d), F32),
        scratch_shapes=[pltpu.VMEM((tm, d), BF16)],
        compiler_params=_params(vmem, 2), name="ffn",
    )(*args)


def _rope_tables(pos, dk, batch):
    inv = 1.0 / (ROPE_BASE ** jnp.linspace(0.0, 1.0, dk // 2, dtype=F32))
    th = pos.astype(F32)[:, None] * jnp.repeat(inv, 2)[None, :]
    sin, cos = jnp.sin(th), jnp.cos(th)
    even = (jnp.arange(dk) % 2) == 0
    sin_next = jnp.where(even, -sin, 0.0)
    sin_prev = jnp.where(even, 0.0, sin)
    return tuple(jnp.tile(t, (batch, 1)) for t in (cos, sin_next, sin_prev))


def _cast_w_in(w_in, diff_w, q_scale):
    col_scale = jnp.where(jnp.arange(w_in.shape[-1]) < diff_w, q_scale, 1.0).astype(F32)
    return (w_in * col_scale).astype(BF16)


def _mixer(x, layer, w, p, rope, past, kv_stack, *, depth, batch, seq, heads, ret_heads):
    d = x.shape[1]
    hd = p["subln_g"].shape[-1]
    diff_w = heads * hd
    dk = p["state"].shape[-2]
    dv = p["state"].shape[-1]
    qk_w = ret_heads * dk
    ret_w = ret_heads * dv
    c_k, c_v, c_rqk = diff_w, 2 * diff_w, 3 * diff_w
    skipped = 2 * diff_w + 2 * qk_w
    c_rv, c_rg, c_ga, c_gb = diff_w, diff_w + ret_w, diff_w + 2 * ret_w, diff_w + 2 * ret_w + d
    main_w = c_gb + d

    main, h = _proj(x, w["in"], layer, 0, main_w, gain=p["norm_mix_g"], emit_h=True,
                    gap=(diff_w, skipped))
    k32, k16 = _proj_heads(h, w["in"], layer, c_k, heads, kv_stack[0], depth)
    v32, v16 = _proj_heads(h, w["in"], layer, c_v, heads, kv_stack[1], depth)
    rqk = _proj(h, w["in"], layer, c_rqk, 2 * qk_w, epi="rope", extra=rope,
                rope_scale=dk ** -0.5, tn_pref=qk_w)

    lam_init = 0.8 - 0.6 * math.exp(-0.3 * layer)
    lams = (p["lq1"], p["lk1"], p["lq2"], p["lk2"])
    if past:
        oa = _attn_decode(main, k16, v16, p["cache_k"], p["cache_v"], layer, p["subln_g"],
                          lam_init, lams, batch=batch, seq=seq, heads=heads)
        state0 = p["state"]
    else:
        oa = _attn_prompt(main, k16, v16, p["subln_g"], lam_init, lams,
                          batch=batch, seq=seq, heads=heads)
        state0 = None
    ob, st = _retention(rqk, main, c_rv, c_rg, dv, p["ret_g"], state0, layer,
                        batch=batch, seq=seq, heads=ret_heads)
    merged = _merge(oa, ob, main, c_ga, c_gb, w["a"], w["b"], layer)
    x = _proj(merged, w["out"], layer, 0, d, epi="residual", extra=(x,))
    return x, (k32, v32), st


def kernel(x_prompt, x_sample, cache_diff_k, cache_diff_v, state_ret, norm_mix_g, w_in,
           lambda_q1, lambda_k1, lambda_q2, lambda_k2, diff_subln_g, ret_norm_g,
           w_branch_a, w_branch_b, w_out, norm_ffn_g, w_up, w_down, final_norm_g):
    depth, dec_batch, past_len, heads, hd = cache_diff_k.shape
    batch, seq, d = x_prompt.shape
    dec_seq = x_sample.shape[1]
    ret_heads, dk, dv = state_ret.shape[2:]

    w = {"in": _cast_w_in(w_in, heads * hd, (hd // 2) ** -0.5 * LOG2E),
         "a": w_branch_a.astype(BF16), "b": w_branch_b.astype(BF16),
         "out": w_out.astype(BF16), "up": w_up.astype(BF16), "down": w_down.astype(BF16)}
    cache_k, cache_v = cache_diff_k, cache_diff_v
    rope_p = _rope_tables(jnp.arange(seq), dk, batch)
    rope_s = _rope_tables(past_len + jnp.arange(dec_seq), dk, dec_batch)

    xp = x_prompt.reshape(batch * seq, d)
    xs = x_sample.reshape(dec_batch * dec_seq, d)
    kv_p, kv_s = (None, None), (None, None)
    states_p, states_s = [], []
    for l in range(depth):
        p = {"norm_mix_g": norm_mix_g[l], "lq1": lambda_q1[l], "lk1": lambda_k1[l],
             "lq2": lambda_q2[l], "lk2": lambda_k2[l], "subln_g": diff_subln_g[l],
             "ret_g": ret_norm_g[l], "cache_k": cache_k, "cache_v": cache_v, "state": state_ret}
        final_g = final_norm_g if l == depth - 1 else None
        xp, kv_p, sp = _mixer(xp, l, w, p, rope_p, False, kv_p, depth=depth, batch=batch,
                              seq=seq, heads=heads, ret_heads=ret_heads)
        xs, kv_s, ss = _mixer(xs, l, w, p, rope_s, True, kv_s, depth=depth, batch=dec_batch,
                              seq=dec_seq, heads=heads, ret_heads=ret_heads)
        xp = _ffn(xp, norm_ffn_g[l], w["up"], w["down"], l, final_g)
        xs = _ffn(xs, norm_ffn_g[l], w["up"], w["down"], l, final_g)
        states_p.append(sp)
        states_s.append(ss)
    return (xp.reshape(batch, seq, d), xs.reshape(dec_batch, dec_seq, d),
            kv_p[0].reshape(depth, batch, seq, heads, hd),
            kv_p[1].reshape(depth, batch, seq, heads, hd), jnp.stack(states_p),
            kv_s[0].reshape(depth, dec_batch, dec_seq, heads, hd),
            kv_s[1].reshape(depth, dec_batch, dec_seq, heads, hd), jnp.stack(states_s))
```

```python
import functools
import math

import jax
import jax.numpy as jnp
from jax import lax
from jax.experimental import pallas as pl
from jax.experimental.pallas import tpu as pltpu

F32 = jnp.float32
BF16 = jnp.bfloat16

NORM_EPS = 1e-6
ROPE_BASE = 10000.0
MASK_CHUNK = 64
NEG_INF = -1e30
SHIFT_LIMIT = 48.0
NORM_SLACK = 1.02
FAST_SWEEP_GROUP = 6
LOG2E = 1.4426950408889634

V7X_VMEM_BYTES = 64 * 1024 * 1024
V7X_LANES = 128
VMEM_REQUEST_CAP = V7X_VMEM_BYTES - 8 * 1024 * 1024

NT_DIMS = (((1,), (1,)), ((), ()))
TN_DIMS = (((0,), (0,)), ((), ()))


def _pick_tile(n, pref, align):
    if n <= pref:
        return n
    t = pref - pref % align
    while t >= align:
        if n % t == 0:
            return t
        t -= align
    raise ValueError(f"no tile for {n} (pref {pref}, align {align})")


def _params(vmem_bytes, n_grid):
    return pltpu.CompilerParams(
        dimension_semantics=("arbitrary",) * n_grid,
        vmem_limit_bytes=int(min(VMEM_REQUEST_CAP, vmem_bytes)))


def _rms_rows(x, gain):
    ms = jnp.mean(x * x, axis=-1, keepdims=True)
    return x * lax.rsqrt(ms + NORM_EPS) * gain


def _proj_kernel(*refs, has_norm, emit_h, epi, row_chunk, rope_scale):
    refs = list(refs)
    x_ref = refs.pop(0)
    g_ref = refs.pop(0) if has_norm else None
    w_ref = refs.pop(0)
    if epi == "rope":
        cos_ref, sina_ref, sinb_ref = refs.pop(0), refs.pop(0), refs.pop(0)
    if epi == "residual":
        res_ref = refs.pop(0)
    o_ref = refs.pop(0)
    hout_ref = refs.pop(0) if emit_h else None
    h_scr = refs.pop(0) if has_norm else None

    j = pl.program_id(1)
    if has_norm:
        @pl.when(j == 0)
        def _():
            def rows(r, carry):
                rs = pl.ds(pl.multiple_of(r * row_chunk, row_chunk), row_chunk)
                hrow = _rms_rows(x_ref[rs, :], g_ref[...]).astype(BF16)
                h_scr[rs, :] = hrow
                if emit_h:
                    hout_ref[rs, :] = hrow
                return carry
            lax.fori_loop(0, x_ref.shape[0] // row_chunk, rows, 0, unroll=2)
        lhs = h_scr[...]
    else:
        lhs = x_ref[...]

    acc = jnp.dot(lhs, w_ref[...], preferred_element_type=F32)
    if epi == "bf16":
        o_ref[...] = acc.astype(BF16)
    elif epi == "residual":
        o_ref[...] = res_ref[...] + acc
    elif epi == "rope":
        n_tiles = pl.num_programs(1)
        scale = jnp.where(j < n_tiles // 2, 1.0, rope_scale).astype(F32)
        cos, sina, sinb = cos_ref[...], sina_ref[...], sinb_ref[...]
        for c in range(acc.shape[1] // V7X_LANES):
            cs = slice(c * V7X_LANES, (c + 1) * V7X_LANES)
            xs = acc[:, cs]
            nxt = pltpu.roll(xs, V7X_LANES - 1, 1)
            prv = pltpu.roll(xs, 1, 1)
            o_ref[:, cs] = ((xs * cos + nxt * sina + prv * sinb) * scale).astype(BF16)
    else:
        raise ValueError(epi)


def _proj(lhs, w, layer, col0, ncols, *, gain=None, emit_h=False, epi="bf16", extra=(),
          rope_scale=1.0, gap=(0, 0), tm_pref=1024, tn_pref=1024):
    n, k = lhs.shape
    has_norm = gain is not None
    tm = _pick_tile(n, tm_pref, 16)
    tn = _pick_tile(ncols, tn_pref, V7X_LANES)
    assert col0 % tn == 0 and gap[0] % tn == 0 and gap[1] % tn == 0
    cb = col0 // tn
    lead_tiles, skip_tiles = gap[0] // tn, gap[1] // tn
    grid = (n // tm, ncols // tn)

    def w_tile(j):
        return cb + j + jnp.where(j >= lead_tiles, skip_tiles, 0) if skip_tiles else cb + j

    in_specs = [pl.BlockSpec((tm, k), lambda i, j: (i, 0))]
    args = [lhs]
    if has_norm:
        in_specs.append(pl.BlockSpec((1, k), lambda i, j: (0, 0)))
        args.append(gain.reshape(1, k))
    in_specs.append(pl.BlockSpec((None, k, tn), lambda i, j: (layer, 0, w_tile(j))))
    args.append(w)
    if epi == "rope":
        for t in extra:
            in_specs.append(pl.BlockSpec((tm, V7X_LANES), lambda i, j: (i, 0)))
            args.append(t)
    elif epi == "residual":
        in_specs.append(pl.BlockSpec((tm, tn), lambda i, j: (i, j)))
        args.append(extra[0])

    tile_spec = pl.BlockSpec((tm, tn), lambda i, j: (i, j))
    out_dtype = F32 if epi == "residual" else BF16
    out_shape = [jax.ShapeDtypeStruct((n, ncols), out_dtype)]
    out_specs = [tile_spec]
    out_bytes = tm * tn * jnp.dtype(out_dtype).itemsize
    if emit_h:
        out_shape.append(jax.ShapeDtypeStruct((n, k), BF16))
        out_specs.append(pl.BlockSpec((tm, k), lambda i, j: (i, 0)))
        out_bytes += tm * k * 2
    scratch = [pltpu.VMEM((tm, k), BF16)] if has_norm else []

    vmem = (2 * tm * k * lhs.dtype.itemsize + 2 * k * tn * 2 + 2 * out_bytes
            + (tm * k * 2 if has_norm else 0) + 3 * tm * tn * 4
            + (2 * tm * tn * 4 if epi == "residual" else 0) + (4 << 20))
    outs = pl.pallas_call(
        functools.partial(_proj_kernel, has_norm=has_norm, emit_h=emit_h, epi=epi,
                          row_chunk=min(tm, 64), rope_scale=rope_scale),
        grid=grid, in_specs=in_specs, out_specs=out_specs, out_shape=out_shape,
        scratch_shapes=scratch, compiler_params=_params(vmem, 2),
        name=f"proj_{epi}")(*args)
    return outs[0] if len(outs) == 1 else tuple(outs)


def _proj_heads_kernel(*refs, aliased):
    x_ref, w_ref = refs[0], refs[1]
    o32_ref, o16_ref = refs[-2], refs[-1]
    acc = jnp.dot(x_ref[...], w_ref[...], preferred_element_type=F32)
    o16_ref[...] = acc.astype(BF16)
    o32_ref[...] = acc.reshape(o32_ref.shape)


def _proj_heads(lhs, w, layer, col0, heads, stacked, depth, *, tm_pref=512):
    n, k = lhs.shape
    ncols = k
    hd = ncols // heads
    tm = _pick_tile(n, tm_pref, 16)
    cb = col0 // ncols
    assert col0 % ncols == 0
    aliased = stacked is not None
    in_specs = [pl.BlockSpec((tm, k), lambda i: (i, 0)),
                pl.BlockSpec((None, k, ncols), lambda i: (layer, 0, cb))]
    args = [lhs, w]
    if aliased:
        in_specs.append(pl.BlockSpec(memory_space=pl.ANY))
        args.append(stacked)
    vmem = 4 * tm * k * 2 + 4 * k * ncols * 2 + 4 * tm * ncols * 4 + 4 * tm * ncols * 2 + (4 << 20)
    return pl.pallas_call(
        functools.partial(_proj_heads_kernel, aliased=aliased),
        grid=(n // tm,), in_specs=in_specs,
        out_specs=[pl.BlockSpec((None, tm, heads, hd), lambda i: (layer, i, 0, 0)),
                   pl.BlockSpec((tm, ncols), lambda i: (i, 0))],
        out_shape=[jax.ShapeDtypeStruct((depth, n, heads, hd), F32),
                   jax.ShapeDtypeStruct((n, ncols), BF16)],
        input_output_aliases={2: 0} if aliased else {},
        compiler_params=_params(vmem, 1), name="proj_heads")(*args)


def _lam_value(lamc_ref, lq1_ref, lk1_ref, lq2_ref, lk2_ref):
    a = jnp.exp(jnp.sum(lq1_ref[...] * lk1_ref[...], axis=-1, keepdims=True))
    b = jnp.exp(jnp.sum(lq2_ref[...] * lk2_ref[...], axis=-1, keepdims=True))
    return a - b + lamc_ref[0]


def _lane_repeat(x, reps):
    return x if reps == 1 else jnp.concatenate([x] * reps, axis=1)


def _softmax_update(s, v, m_prev, l_prev, acc_prev):
    keys = s.shape[1]
    m_new = jnp.maximum(m_prev, jnp.max(s, axis=1, keepdims=True))
    alpha = jnp.exp2(m_prev - m_new)
    p = jnp.exp2(s - _lane_repeat(m_new, keys // V7X_LANES))
    l_new = alpha * l_prev + jnp.sum(p, axis=1, keepdims=True)
    pv = jnp.dot(p.astype(BF16), v, preferred_element_type=F32)
    acc_new = acc_prev * _lane_repeat(alpha, acc_prev.shape[1] // V7X_LANES) + pv
    return m_new, l_new, acc_new


def _softmax_step(s, v, m_ref, l_ref, acc_ref):
    m_ref[...], l_ref[...], acc_ref[...] = _softmax_update(s, v, m_ref[...], l_ref[...],
                                                           acc_ref[...])


def _diff_finish(a1, l1, a2, l2, lam, gain, out_scale):
    rep = a1.shape[1] // V7X_LANES
    o = a1 * _lane_repeat(1.0 / l1, rep) - lam * (a2 * _lane_repeat(1.0 / l2, rep))
    return _rms_rows(o, gain) * out_scale


def _attn_prompt_kernel(lamc_ref, lq1_ref, lk1_ref, lq2_ref, lk2_ref, q_ref, k_ref, v_ref, g_ref,
                        o_ref, m1, l1, a1, m2, l2, a2, sa1, sa2, sb1, sb2, ka1, ka2, ext, *, dh, tq):
    seq = q_ref.shape[0]
    tk = tq
    nq = seq // tq
    n_units = nq * (nq + 1) // 2
    shift = MASK_CHUNK.bit_length() - 1
    assert tq // MASK_CHUNK < dh
    maps = ((0, m1, l1, a1, ka1), (dh, m2, l2, a2, ka2))

    lane = lax.broadcasted_iota(jnp.int32, (tk, dh), 1)
    row_chunk = lax.shift_right_logical(lax.broadcasted_iota(jnp.int32, (tk, dh), 0), shift)
    key_ext = jnp.where(lane == 0, -1.0, jnp.where(lane - 1 == row_chunk, 1.0, 0.0)).astype(BF16)
    hidden = (lane >= 1) & (lane <= tq // MASK_CHUNK) & (lane - 1 > row_chunk)

    ones = jnp.ones((dh, dh), BF16)

    def widen(c, carry):
        rs = pl.ds(pl.multiple_of(c * tk, tk), tk)
        out = []
        for idx, (lo, _, l_ref, a_ref, ka) in enumerate(maps):
            l_ref[rs, :] = jnp.zeros((tk, V7X_LANES), F32)
            a_ref[rs, :] = jnp.zeros((tk, 2 * dh), F32)
            kc = k_ref[rs, lo:lo + dh]
            qc = q_ref[rs, lo:lo + dh]
            ka[rs, 0:dh] = kc
            ka[rs, dh:2 * dh] = key_ext
            k2 = jnp.dot(kc * kc, ones, preferred_element_type=F32)
            q2 = jnp.dot(qc * qc, ones, preferred_element_type=F32)
            out += [jnp.maximum(carry[2 * idx], jnp.max(k2, axis=0, keepdims=True)),
                    jnp.maximum(carry[2 * idx + 1], jnp.max(q2, axis=0, keepdims=True))]
        return tuple(out)
    norms2 = lax.fori_loop(0, nq, widen, (jnp.zeros((1, dh), F32),) * 4)

    worst = jnp.zeros((1, 1), F32)
    for idx in range(2):
        bound = jnp.sqrt(norms2[2 * idx][:, 0:1] * norms2[2 * idx + 1][:, 0:1]) * NORM_SLACK
        worst = jnp.maximum(worst, bound)
        base = jnp.where(lane == 0, bound, 0.0)
        ext[(2 * idx) * tq:(2 * idx + 1) * tq, :] = base.astype(BF16)
        ext[(2 * idx + 1) * tq:(2 * idx + 2) * tq, :] = jnp.where(hidden, NEG_INF, base).astype(BF16)
    fast = jnp.max(worst) <= SHIFT_LIMIT

    lam = _lam_value(lamc_ref, lq1_ref, lk1_ref, lq2_ref, lk2_ref)
    gain, out_scale = g_ref[...], lamc_ref[1]

    def q_rows(qi):
        return pl.ds(pl.multiple_of(qi * tq, tq), tq)

    def k_rows(j):
        return pl.ds(pl.multiple_of(j * tk, tk), tk)

    def scores(qi, j, bufs):
        diag = jnp.where(qi == j, 1, 0)
        for idx, ((lo, _, _, _, ka), s_scr) in enumerate(zip(maps, bufs)):
            e_rows = pl.ds(pl.multiple_of((2 * idx + diag) * tq, tq), tq)
            qa = jnp.concatenate([q_ref[q_rows(qi), lo:lo + dh], ext[e_rows, :]], axis=1)
            s_scr[...] = lax.dot_general(qa, ka[k_rows(j), :], NT_DIMS,
                                         preferred_element_type=F32)

    def consume(qi, j, bufs, exact):
        v = v_ref[k_rows(j), :]
        rq = q_rows(qi)
        for (_, m_ref, l_ref, a_ref, _), s_scr in zip(maps, bufs):
            s = s_scr[...]
            l_prev, a_prev = l_ref[rq, :], a_ref[rq, :]
            if exact:
                m_ref[rq, :], l_new, a_new = _softmax_update(s, v, m_ref[rq, :], l_prev, a_prev)
            else:
                p = jnp.exp2(s)
                l_new = l_prev + jnp.sum(p, axis=1, keepdims=True)
                a_new = a_prev + jnp.dot(p.astype(BF16), v, preferred_element_type=F32)
            l_ref[rq, :], a_ref[rq, :] = l_new, a_new

    def advance(qi, j):
        last = j == qi
        return jnp.where(last, qi + 1, qi), jnp.where(last, 0, j + 1)

    buf_a, buf_b = (sa1, sa2), (sb1, sb2)

    units = [(qi, j) for qi in range(nq) for j in range(qi + 1)]

    def sweep(exact, group):
        if exact:
            for m_ref in (m1, m2):
                m_ref[...] = jnp.full(m_ref.shape, -jnp.inf, F32)
        scores(0, 0, buf_a)

        def trip(t, carry):
            cur = carry
            for g in range(group):
                nxt = advance(*cur)
                over = nxt[0] >= nq
                safe = (jnp.where(over, nq - 1, nxt[0]), jnp.where(over, 0, nxt[1]))
                mine, other = (buf_a, buf_b) if g % 2 == 0 else (buf_b, buf_a)
                scores(*safe, other)
                consume(*cur, mine, exact)
                cur = nxt
            return cur
        n_trips = n_units // group
        lax.fori_loop(0, n_trips, trip, (jnp.int32(0), jnp.int32(0)))
        rest = units[n_trips * group:]
        for g, (qi, j) in enumerate(rest):
            mine, other = (buf_a, buf_b) if g % 2 == 0 else (buf_b, buf_a)
            if g + 1 < len(rest):
                scores(*rest[g + 1], other)
            consume(qi, j, mine, exact)

    @pl.when(fast)
    def _():
        sweep(False, FAST_SWEEP_GROUP)

    @pl.when(jnp.logical_not(fast))
    def _():
        sweep(True, 2)

    def finish(qi, carry):
        rq = q_rows(qi)
        o = _diff_finish(a1[rq, :], l1[rq, :], a2[rq, :], l2[rq, :], lam, gain, out_scale)
        o_ref[rq, :] = o.astype(BF16)
        return carry
    lax.fori_loop(0, nq, finish, 0)


def _lam_inputs(lam_init, lams):
    specs = [pl.BlockSpec(memory_space=pltpu.SMEM)]
    args = [jnp.array([lam_init, 1.0 - lam_init], F32)]
    for v in lams:
        specs.append(pl.BlockSpec((1, v.shape[-1]), lambda *_: (0, 0)))
        args.append(v.reshape(1, -1))
    return specs, args


def _attn_prompt(dq, k16, v16, gain, lam_init, lams, *, batch, seq, heads, tq_pref=512):
    n, width = k16.shape
    hd = width // heads
    dh = hd // 2
    tq = _pick_tile(seq, tq_pref, V7X_LANES)
    lam_specs, lam_args = _lam_inputs(lam_init, lams)
    per_head = pl.BlockSpec((seq, hd), lambda b, h: (b, h))
    in_specs = lam_specs + [per_head, per_head, per_head,
                            pl.BlockSpec((1, hd), lambda b, h: (0, 0))]
    stat = pltpu.VMEM((seq, V7X_LANES), F32)
    accs = pltpu.VMEM((seq, hd), F32)
    sbuf = pltpu.VMEM((tq, tq), F32)
    wide_k = pltpu.VMEM((seq, hd), BF16)
    vmem = 8 * seq * hd * 2 + 2 * seq * hd * 2 + 2 * seq * hd * 4 + 4 * seq * V7X_LANES * 4 \
        + 12 * tq * tq * 4 + (4 << 20)
    return pl.pallas_call(
        functools.partial(_attn_prompt_kernel, dh=dh, tq=tq),
        grid=(batch, heads), in_specs=in_specs, out_specs=per_head,
        out_shape=jax.ShapeDtypeStruct((n, width), BF16),
        scratch_shapes=[stat, stat, accs, stat, stat, accs, sbuf, sbuf, sbuf, sbuf,
                        wide_k, wide_k, pltpu.VMEM((4 * tq, dh), BF16)],
        compiler_params=_params(vmem, 2), name="diff_attn_prompt",
    )(*lam_args, dq, k16, v16, gain.reshape(1, hd))


def _attn_decode_kernel(lamc_ref, lq1_ref, lk1_ref, lq2_ref, lk2_ref, q_ref, kn_ref, vn_ref,
                        kc_ref, vc_ref, g_ref, o_ref, qd, m_scr, l_scr, a_scr,
                        *, heads, dh, n_new):
    j = pl.program_id(1)
    hd = 2 * dh
    s_q = q_ref.shape[0]

    @pl.when(j == 0)
    def _():
        qd[...] = jnp.zeros(qd.shape, BF16)
        m_scr[...] = jnp.full(m_scr.shape, -jnp.inf, F32)
        l_scr[...] = jnp.zeros(l_scr.shape, F32)
        a_scr[...] = jnp.zeros(a_scr.shape, F32)
        new_mask = lax.broadcasted_iota(jnp.int32, (2 * s_q, kn_ref.shape[0]), 1) < n_new
        for h in range(heads):
            hs = slice(h * hd, (h + 1) * hd)
            qd[h, 0:s_q, 0:dh] = q_ref[:, h * hd:h * hd + dh]
            qd[h, s_q:2 * s_q, dh:hd] = q_ref[:, h * hd + dh:(h + 1) * hd]
            s = lax.dot_general(qd[h], kn_ref[:, hs], NT_DIMS, preferred_element_type=F32)
            s = jnp.where(new_mask, s, NEG_INF)
            _softmax_step(s, vn_ref[:, hs], m_scr.at[h], l_scr.at[h], a_scr.at[h])

    tk = kc_ref.shape[0]
    k_all = kc_ref[...].astype(BF16).reshape(tk, heads * hd)
    v_all = vc_ref[...].astype(BF16).reshape(tk, heads * hd)
    scores = [lax.dot_general(qd[h], k_all[:, h * hd:(h + 1) * hd], NT_DIMS,
                              preferred_element_type=F32) for h in range(heads)]
    for h in range(heads):
        _softmax_step(scores[h], v_all[:, h * hd:(h + 1) * hd], m_scr.at[h], l_scr.at[h],
                      a_scr.at[h])

    @pl.when(j == pl.num_programs(1) - 1)
    def _():
        lam = _lam_value(lamc_ref, lq1_ref, lk1_ref, lq2_ref, lk2_ref)
        for h in range(heads):
            a, l = a_scr[h], l_scr[h]
            o = _diff_finish(a[:s_q], l[:s_q], a[s_q:], l[s_q:], lam, g_ref[...], lamc_ref[1])
            o_ref[:, h * hd:(h + 1) * hd] = o.astype(BF16)


def _attn_decode(dq, k16, v16, cache_k, cache_v, layer, gain, lam_init, lams, *, batch, seq,
                 heads, tk_pref=512):
    n, width = k16.shape
    hd = width // heads
    dh = hd // 2
    past = cache_k.shape[2]
    tk = _pick_tile(past, tk_pref, V7X_LANES)
    pad = V7X_LANES
    kn = jnp.pad(k16.reshape(batch, seq, width), ((0, 0), (0, pad - seq), (0, 0)))
    vn = jnp.pad(v16.reshape(batch, seq, width), ((0, 0), (0, pad - seq), (0, 0)))
    lam_specs, lam_args = _lam_inputs(lam_init, lams)
    in_specs = lam_specs + [
        pl.BlockSpec((seq, width), lambda b, j: (b, 0)),
        pl.BlockSpec((None, pad, width), lambda b, j: (b, 0, 0)),
        pl.BlockSpec((None, pad, width), lambda b, j: (b, 0, 0)),
        pl.BlockSpec((None, None, tk, heads, hd), lambda b, j: (layer, b, j, 0, 0)),
        pl.BlockSpec((None, None, tk, heads, hd), lambda b, j: (layer, b, j, 0, 0)),
        pl.BlockSpec((1, hd), lambda b, j: (0, 0)),
    ]
    scratch = [pltpu.VMEM((heads, 2 * seq, hd), BF16),
               pltpu.VMEM((heads, 2 * seq, V7X_LANES), F32),
               pltpu.VMEM((heads, 2 * seq, V7X_LANES), F32),
               pltpu.VMEM((heads, 2 * seq, hd), F32)]
    vmem = 4 * tk * width * 4 + 2 * tk * width * 2 + 8 * pad * width * 2 + (8 << 20)
    return pl.pallas_call(
        functools.partial(_attn_decode_kernel, heads=heads, dh=dh, n_new=seq),
        grid=(batch, past // tk), in_specs=in_specs,
        out_specs=pl.BlockSpec((seq, width), lambda b, j: (b, 0)),
        out_shape=jax.ShapeDtypeStruct((n, width), BF16),
        scratch_shapes=scratch, compiler_params=_params(vmem, 2), name="diff_attn_decode",
    )(*lam_args, dq, kn, vn, cache_k, cache_v, gain.reshape(1, hd))


def _ret_tables(heads, chunk, dk, dv):
    log_gamma = jnp.log1p(-jnp.exp2(-5.0 - jnp.arange(heads, dtype=F32)))
    idx = jnp.arange(chunk, dtype=F32)
    diff = idx[:, None] - idx[None, :]
    decay = jnp.where(diff >= 0, jnp.exp(log_gamma[:, None, None] * jnp.maximum(diff, 0.0)), 0.0)
    cross = jnp.exp(log_gamma[:, None] * (idx + 1.0)[None, :])
    kdec = jnp.exp(log_gamma[:, None] * (chunk - 1.0 - idx)[None, :])
    carry = jnp.exp(log_gamma * chunk)
    return (decay,
            jnp.broadcast_to(cross[:, :, None], (heads, chunk, dv)),
            jnp.broadcast_to(kdec[:, :, None], (heads, chunk, dk)),
            carry)


def _ret_kernel(*refs, has_init):
    refs = list(refs)
    carry_ref, q_ref, k_ref, v_ref, rg_ref, dec_ref, cross_ref, kdec_ref, g_ref = refs[:9]
    refs = refs[9:]
    s0_ref = refs.pop(0) if has_init else None
    o_ref, sout_ref, s_scr = refs
    heads, dk, dv = s_scr.shape

    @pl.when(pl.program_id(1) == 0)
    def _():
        s_scr[...] = s0_ref[...] if has_init else jnp.zeros(s_scr.shape, F32)

    for h in range(heads):
        ks, vs = slice(h * dk, (h + 1) * dk), slice(h * dv, (h + 1) * dv)
        q, k, v = q_ref[:, ks], k_ref[:, ks], v_ref[:, vs]
        state = s_scr[h]
        scores = lax.dot_general(q, k, NT_DIMS, preferred_element_type=F32) * dec_ref[h]
        inner = jnp.dot(scores.astype(BF16), v, preferred_element_type=F32)
        cross = jnp.dot(q, state.astype(BF16), preferred_element_type=F32) * cross_ref[h]
        k_dec = (k.astype(F32) * kdec_ref[h]).astype(BF16)
        new_state = carry_ref[h] * state + lax.dot_general(k_dec, v, TN_DIMS,
                                                           preferred_element_type=F32)
        s_scr[h] = new_state
        sout_ref[h] = new_state
        gate = rg_ref[:, vs].astype(F32)
        y = _rms_rows(inner + cross, g_ref[...]) * (gate * jax.nn.sigmoid(gate))
        o_ref[:, vs] = y.astype(BF16)


def _retention(rqk, main, c_v, c_gate, dv, gain, state0, layer, *, batch, seq, heads,
               chunk_pref=256):
    n = rqk.shape[0]
    dk = rqk.shape[1] // (2 * heads)
    assert c_v % (heads * dv) == 0 and c_gate % (heads * dv) == 0
    vb, gb = c_v // (heads * dv), c_gate // (heads * dv)
    chunk = _pick_tile(seq, chunk_pref, 16)
    nc = seq // chunk
    decay, cross, kdec, carry = _ret_tables(heads, chunk, dk, dv)
    has_init = state0 is not None
    row = lambda b, c: b * nc + c
    whole = lambda shape: pl.BlockSpec(shape, lambda b, c: (0,) * len(shape))
    in_specs = [
        pl.BlockSpec(memory_space=pltpu.SMEM),
        pl.BlockSpec((chunk, heads * dk), lambda b, c: (row(b, c), 0)),
        pl.BlockSpec((chunk, heads * dk), lambda b, c: (row(b, c), 1)),
        pl.BlockSpec((chunk, heads * dv), lambda b, c: (row(b, c), vb)),
        pl.BlockSpec((chunk, heads * dv), lambda b, c: (row(b, c), gb)),
        whole((heads, chunk, chunk)), whole((heads, chunk, dv)), whole((heads, chunk, dk)),
        whole((1, dv)),
    ]
    args = [carry, rqk, rqk, main, main, decay, cross, kdec, gain.reshape(1, dv)]
    if has_init:
        in_specs.append(pl.BlockSpec((None, None, heads, dk, dv),
                                     lambda b, c: (layer, b, 0, 0, 0)))
        args.append(state0)
    vmem = (2 * heads * chunk * (chunk + dv + dk) * 4 + 8 * chunk * heads * (dk + dv) * 2
            + 6 * heads * dk * dv * 4 + (8 << 20))
    ob, st = pl.pallas_call(
        functools.partial(_ret_kernel, has_init=has_init),
        grid=(batch, nc), in_specs=in_specs,
        out_specs=[pl.BlockSpec((chunk, heads * dv), lambda b, c: (row(b, c), 0)),
                   pl.BlockSpec((None, heads, dk, dv), lambda b, c: (b, 0, 0, 0))],
        out_shape=[jax.ShapeDtypeStruct((n, heads * dv), BF16),
                   jax.ShapeDtypeStruct((batch, heads, dk, dv), F32)],
        scratch_shapes=[pltpu.VMEM((heads, dk, dv), F32)],
        compiler_params=_params(vmem, 2), name="retention",
    )(*args)
    return ob, st


def _merge_kernel(oa_ref, ob_ref, wa_ref, wb_ref, ga_ref, gb_ref, o_ref):
    a = jnp.dot(oa_ref[...], wa_ref[...], preferred_element_type=F32)
    b = jnp.dot(ob_ref[...], wb_ref[...], preferred_element_type=F32)
    ga = jax.nn.sigmoid(ga_ref[...].astype(F32))
    gb = jax.nn.sigmoid(gb_ref[...].astype(F32))
    o_ref[...] = (ga * a + gb * b).astype(BF16)


def _merge(oa, ob, main, c_ga, c_gb, w_a, w_b, layer, *, tm_pref=1024, tn_pref=512):
    n, k = oa.shape
    d = w_a.shape[2]
    tm = _pick_tile(n, tm_pref, 16)
    tn = _pick_tile(d, tn_pref, V7X_LANES)
    assert c_ga % tn == 0 and c_gb % tn == 0
    ga0, gb0 = c_ga // tn, c_gb // tn
    vmem = 4 * tm * k * 2 + 4 * k * tn * 2 + 6 * tm * tn * 2 + 4 * tm * tn * 4 + (4 << 20)
    return pl.pallas_call(
        _merge_kernel, grid=(n // tm, d // tn),
        in_specs=[pl.BlockSpec((tm, k), lambda i, j: (i, 0)),
                  pl.BlockSpec((tm, k), lambda i, j: (i, 0)),
                  pl.BlockSpec((None, k, tn), lambda i, j: (layer, 0, j)),
                  pl.BlockSpec((None, k, tn), lambda i, j: (layer, 0, j)),
                  pl.BlockSpec((tm, tn), lambda i, j: (i, ga0 + j)),
                  pl.BlockSpec((tm, tn), lambda i, j: (i, gb0 + j))],
        out_specs=pl.BlockSpec((tm, tn), lambda i, j: (i, j)),
        out_shape=jax.ShapeDtypeStruct((n, d), BF16),
        compiler_params=_params(vmem, 2), name="branch_merge",
    )(oa, ob, w_a, w_b, main, main)


def _ffn_kernel(*refs, row_chunk, final_norm):
    refs = list(refs)
    x_ref, g_ref, wu_ref, wd_ref = refs[:4]
    fg_ref = refs[4] if final_norm else None
    o_ref, h_scr = refs[-2:]
    f = pl.program_id(1)
    n_chunks = x_ref.shape[0] // row_chunk

    @pl.when(f == 0)
    def _():
        def rows(r, carry):
            rs = pl.ds(pl.multiple_of(r * row_chunk, row_chunk), row_chunk)
            x = x_ref[rs, :]
            h_scr[rs, :] = _rms_rows(x, g_ref[...]).astype(BF16)
            o_ref[rs, :] = x
            return carry
        lax.fori_loop(0, n_chunks, rows, 0, unroll=2)

    u = jnp.dot(h_scr[...], wu_ref[...], preferred_element_type=F32)
    u = jnp.square(jnp.maximum(u, 0.0)).astype(BF16)
    o_ref[...] += jnp.dot(u, wd_ref[...], preferred_element_type=F32)

    if final_norm:
        @pl.when(f == pl.num_programs(1) - 1)
        def _():
            def rows(r, carry):
                rs = pl.ds(pl.multiple_of(r * row_chunk, row_chunk), row_chunk)
                o_ref[rs, :] = _rms_rows(o_ref[rs, :], fg_ref[...])
                return carry
            lax.fori_loop(0, n_chunks, rows, 0)


def _ffn(x, gain, w_up, w_down, layer, final_gain=None, *, tm_pref=1024, tf_pref=512):
    n, d = x.shape
    d_ff = w_up.shape[2]
    tm = _pick_tile(n, tm_pref, 16)
    tf = _pick_tile(d_ff, tf_pref, V7X_LANES)
    final_norm = final_gain is not None
    in_specs = [pl.BlockSpec((tm, d), lambda i, f: (i, 0)),
                pl.BlockSpec((1, d), lambda i, f: (0, 0)),
                pl.BlockSpec((None, d, tf), lambda i, f: (layer, 0, f)),
                pl.BlockSpec((None, tf, d), lambda i, f: (layer, f, 0))]
    args = [x, gain.reshape(1, d), w_up, w_down]
    if final_norm:
        in_specs.append(pl.BlockSpec((1, d), lambda i, f: (0, 0)))
        args.append(final_gain.reshape(1, d))
    vmem = 4 * tm * d * 4 + tm * d * 2 + 4 * d * tf * 2 + 2 * tm * tf * 4 + (6 << 20)
    return pl.pallas_call(
        functools.partial(_ffn_kernel, row_chunk=min(tm, 64), final_norm=final_norm),
        grid=(n // tm, d_ff // tf), in_specs=in_specs,
        out_specs=pl.BlockSpec((tm, d), lambda i, f: (i, 0)),
        out_shape=jax.ShapeDtypeStruct((n, d), F32),
        scratch_shapes=[pltpu.VMEM((tm, d), BF16)],
        compiler_params=_params(vmem, 2), name="ffn",
    )(*args)


def _rope_tables(pos, dk, batch):
    inv = 1.0 / (ROPE_BASE ** jnp.linspace(0.0, 1.0, dk // 2, dtype=F32))
    th = pos.astype(F32)[:, None] * jnp.repeat(inv, 2)[None, :]
    sin, cos = jnp.sin(th), jnp.cos(th)
    even = (jnp.arange(dk) % 2) == 0
    sin_next = jnp.where(even, -sin, 0.0)
    sin_prev = jnp.where(even, 0.0, sin)
    return tuple(jnp.tile(t, (batch, 1)) for t in (cos, sin_next, sin_prev))


def _cast_w_in(w_in, diff_w, q_scale):
    col_scale = jnp.where(jnp.arange(w_in.shape[-1]) < diff_w, q_scale, 1.0).astype(F32)
    return (w_in * col_scale).astype(BF16)


def _mixer(x, layer, w, p, rope, past, kv_stack, *, depth, batch, seq, heads, ret_heads):
    d = x.shape[1]
    hd = p["subln_g"].shape[-1]
    diff_w = heads * hd
    dk = p["state"].shape[-2]
    dv = p["state"].shape[-1]
    qk_w = ret_heads * dk
    ret_w = ret_heads * dv
    c_k, c_v, c_rqk = diff_w, 2 * diff_w, 3 * diff_w
    skipped = 2 * diff_w + 2 * qk_w
    c_rv, c_rg, c_ga, c_gb = diff_w, diff_w + ret_w, diff_w + 2 * ret_w, diff_w + 2 * ret_w + d
    main_w = c_gb + d

    main, h = _proj(x, w["in"], layer, 0, main_w, gain=p["norm_mix_g"], emit_h=True,
                    gap=(diff_w, skipped))
    k32, k16 = _proj_heads(h, w["in"], layer, c_k, heads, kv_stack[0], depth)
    v32, v16 = _proj_heads(h, w["in"], layer, c_v, heads, kv_stack[1], depth)
    rqk = _proj(h, w["in"], layer, c_rqk, 2 * qk_w, epi="rope", extra=rope,
                rope_scale=dk ** -0.5, tn_pref=qk_w)

    lam_init = 0.8 - 0.6 * math.exp(-0.3 * layer)
    lams = (p["lq1"], p["lk1"], p["lq2"], p["lk2"])
    if past:
        oa = _attn_decode(main, k16, v16, p["cache_k"], p["cache_v"], layer, p["subln_g"],
                          lam_init, lams, batch=batch, seq=seq, heads=heads)
        state0 = p["state"]
    else:
        oa = _attn_prompt(main, k16, v16, p["subln_g"], lam_init, lams,
                          batch=batch, seq=seq, heads=heads)
        state0 = None
    ob, st = _retention(rqk, main, c_rv, c_rg, dv, p["ret_g"], state0, layer,
                        batch=batch, seq=seq, heads=ret_heads)
    merged = _merge(oa, ob, main, c_ga, c_gb, w["a"], w["b"], layer)
    x = _proj(merged, w["out"], layer, 0, d, epi="residual", extra=(x,), tm_pref=512, tn_pref=d)
    return x, (k32, v32), st


def kernel(x_prompt, x_sample, cache_diff_k, cache_diff_v, state_ret, norm_mix_g, w_in,
           lambda_q1, lambda_k1, lambda_q2, lambda_k2, diff_subln_g, ret_norm_g,
           w_branch_a, w_branch_b, w_out, norm_ffn_g, w_up, w_down, final_norm_g):
    depth, dec_batch, past_len, heads, hd = cache_diff_k.shape
    batch, seq, d = x_prompt.shape
    dec_seq = x_sample.shape[1]
    ret_heads, dk, dv = state_ret.shape[2:]

    w = {"in": _cast_w_in(w_in, heads * hd, (hd // 2) ** -0.5 * LOG2E),
         "a": w_branch_a.astype(BF16), "b": w_branch_b.astype(BF16),
         "out": w_out.astype(BF16), "up": w_up.astype(BF16), "down": w_down.astype(BF16)}
    cache_k, cache_v = cache_diff_k, cache_diff_v
    rope_p = _rope_tables(jnp.arange(seq), dk, batch)
    rope_s = _rope_tables(past_len + jnp.arange(dec_seq), dk, dec_batch)

    xp = x_prompt.reshape(batch * seq, d)
    xs = x_sample.reshape(dec_batch * dec_seq, d)
    kv_p, kv_s = (None, None), (None, None)
    states_p, states_s = [], []
    for l in range(depth):
        p = {"norm_mix_g": norm_mix_g[l], "lq1": lambda_q1[l], "lk1": lambda_k1[l],
             "lq2": lambda_q2[l], "lk2": lambda_k2[l], "subln_g": diff_subln_g[l],
             "ret_g": ret_norm_g[l], "cache_k": cache_k, "cache_v": cache_v, "state": state_ret}
        final_g = final_norm_g if l == depth - 1 else None
        xp, kv_p, sp = _mixer(xp, l, w, p, rope_p, False, kv_p, depth=depth, batch=batch,
                              seq=seq, heads=heads, ret_heads=ret_heads)
        xs, kv_s, ss = _mixer(xs, l, w, p, rope_s, True, kv_s, depth=depth, batch=dec_batch,
                              seq=dec_seq, heads=heads, ret_heads=ret_heads)
        xp = _ffn(xp, norm_ffn_g[l], w["up"], w["down"], l, final_g)
        xs = _ffn(xs, norm_ffn_g[l], w["up"], w["down"], l, final_g)
        states_p.append(sp)
        states_s.append(ss)
    return (xp.reshape(batch, seq, d), xs.reshape(dec_batch, dec_seq, d),
            kv_p[0].reshape(depth, batch, seq, heads, hd),
            kv_p[1].reshape(depth, batch, seq, heads, hd), jnp.stack(states_p),
            kv_s[0].reshape(depth, dec_batch, dec_seq, heads, hd),
            kv_s[1].reshape(depth, dec_batch, dec_seq, heads, hd), jnp.stack(states_s))
```

```python
import functools
import math

import jax
import jax.numpy as jnp
from jax import lax
from jax.experimental import pallas as pl
from jax.experimental.pallas import tpu as pltpu

F32 = jnp.float32
BF16 = jnp.bfloat16

NORM_EPS = 1e-6
ROPE_BASE = 10000.0
MASK_CHUNK = 64
NEG_INF = -1e30
SHIFT_LIMIT = 48.0
NORM_SLACK = 1.02
FAST_SWEEP_GROUP = 6
LOG2E = 1.4426950408889634

V7X_VMEM_BYTES = 64 * 1024 * 1024
V7X_LANES = 128
VMEM_REQUEST_CAP = V7X_VMEM_BYTES - 8 * 1024 * 1024

NT_DIMS = (((1,), (1,)), ((), ()))
TN_DIMS = (((0,), (0,)), ((), ()))


def _pick_tile(n, pref, align):
    if n <= pref:
        return n
    t = pref - pref % align
    while t >= align:
        if n % t == 0:
            return t
        t -= align
    raise ValueError(f"no tile for {n} (pref {pref}, align {align})")


def _params(vmem_bytes, n_grid):
    return pltpu.CompilerParams(
        dimension_semantics=("arbitrary",) * n_grid,
        vmem_limit_bytes=int(min(VMEM_REQUEST_CAP, vmem_bytes)))


def _rms_rows(x, gain):
    ms = jnp.mean(x * x, axis=-1, keepdims=True)
    return x * lax.rsqrt(ms + NORM_EPS) * gain


def _proj_kernel(*refs, has_norm, emit_h, epi, row_chunk, rope_scale):
    refs = list(refs)
    x_ref = refs.pop(0)
    g_ref = refs.pop(0) if has_norm else None
    w_ref = refs.pop(0)
    if epi == "rope":
        cos_ref, sina_ref, sinb_ref = refs.pop(0), refs.pop(0), refs.pop(0)
    if epi == "residual":
        res_ref = refs.pop(0)
    o_ref = refs.pop(0)
    hout_ref = refs.pop(0) if emit_h else None
    h_scr = refs.pop(0) if has_norm else None

    j = pl.program_id(1)
    if has_norm:
        @pl.when(j == 0)
        def _():
            def rows(r, carry):
                rs = pl.ds(pl.multiple_of(r * row_chunk, row_chunk), row_chunk)
                hrow = _rms_rows(x_ref[rs, :], g_ref[...]).astype(BF16)
                h_scr[rs, :] = hrow
                if emit_h:
                    hout_ref[rs, :] = hrow
                return carry
            lax.fori_loop(0, x_ref.shape[0] // row_chunk, rows, 0, unroll=2)
        lhs = h_scr[...]
    else:
        lhs = x_ref[...]

    acc = jnp.dot(lhs, w_ref[...], preferred_element_type=F32)
    if epi == "bf16":
        o_ref[...] = acc.astype(BF16)
    elif epi == "residual":
        o_ref[...] = res_ref[...] + acc
    elif epi == "rope":
        n_tiles = pl.num_programs(1)
        scale = jnp.where(j < n_tiles // 2, 1.0, rope_scale).astype(F32)
        cos, sina, sinb = cos_ref[...], sina_ref[...], sinb_ref[...]
        for c in range(acc.shape[1] // V7X_LANES):
            cs = slice(c * V7X_LANES, (c + 1) * V7X_LANES)
            xs = acc[:, cs]
            nxt = pltpu.roll(xs, V7X_LANES - 1, 1)
            prv = pltpu.roll(xs, 1, 1)
            o_ref[:, cs] = ((xs * cos + nxt * sina + prv * sinb) * scale).astype(BF16)
    else:
        raise ValueError(epi)


def _proj(lhs, w, layer, col0, ncols, *, gain=None, emit_h=False, epi="bf16", extra=(),
          rope_scale=1.0, gap=(0, 0), tm_pref=1024, tn_pref=1024):
    n, k = lhs.shape
    has_norm = gain is not None
    tm = _pick_tile(n, tm_pref, 16)
    tn = _pick_tile(ncols, tn_pref, V7X_LANES)
    assert col0 % tn == 0 and gap[0] % tn == 0 and gap[1] % tn == 0
    cb = col0 // tn
    lead_tiles, skip_tiles = gap[0] // tn, gap[1] // tn
    grid = (n // tm, ncols // tn)

    def w_tile(j):
        return cb + j + jnp.where(j >= lead_tiles, skip_tiles, 0) if skip_tiles else cb + j

    in_specs = [pl.BlockSpec((tm, k), lambda i, j: (i, 0))]
    args = [lhs]
    if has_norm:
        in_specs.append(pl.BlockSpec((1, k), lambda i, j: (0, 0)))
        args.append(gain.reshape(1, k))
    in_specs.append(pl.BlockSpec((None, k, tn), lambda i, j: (layer, 0, w_tile(j))))
    args.append(w)
    if epi == "rope":
        for t in extra:
            in_specs.append(pl.BlockSpec((tm, V7X_LANES), lambda i, j: (i, 0)))
            args.append(t)
    elif epi == "residual":
        in_specs.append(pl.BlockSpec((tm, tn), lambda i, j: (i, j)))
        args.append(extra[0])

    tile_spec = pl.BlockSpec((tm, tn), lambda i, j: (i, j))
    out_dtype = F32 if epi == "residual" else BF16
    out_shape = [jax.ShapeDtypeStruct((n, ncols), out_dtype)]
    out_specs = [tile_spec]
    out_bytes = tm * tn * jnp.dtype(out_dtype).itemsize
    if emit_h:
        out_shape.append(jax.ShapeDtypeStruct((n, k), BF16))
        out_specs.append(pl.BlockSpec((tm, k), lambda i, j: (i, 0)))
        out_bytes += tm * k * 2
    scratch = [pltpu.VMEM((tm, k), BF16)] if has_norm else []

    vmem = (2 * tm * k * lhs.dtype.itemsize + 2 * k * tn * 2 + 2 * out_bytes
            + (tm * k * 2 if has_norm else 0) + 3 * tm * tn * 4
            + (2 * tm * tn * 4 if epi == "residual" else 0) + (4 << 20))
    outs = pl.pallas_call(
        functools.partial(_proj_kernel, has_norm=has_norm, emit_h=emit_h, epi=epi,
                          row_chunk=min(tm, 64), rope_scale=rope_scale),
        grid=grid, in_specs=in_specs, out_specs=out_specs, out_shape=out_shape,
        scratch_shapes=scratch, compiler_params=_params(vmem, 2),
        name=f"proj_{epi}")(*args)
    return outs[0] if len(outs) == 1 else tuple(outs)


def _proj_heads_kernel(*refs, aliased):
    x_ref, w_ref = refs[0], refs[1]
    o32_ref, o16_ref = refs[-2], refs[-1]
    acc = jnp.dot(x_ref[...], w_ref[...], preferred_element_type=F32)
    o16_ref[...] = acc.astype(BF16)
    o32_ref[...] = acc.reshape(o32_ref.shape)


def _proj_heads(lhs, w, w_layer, col0, heads, stacked, depth, layer, *, tm_pref=512):
    n, k = lhs.shape
    ncols = k
    hd = ncols // heads
    tm = _pick_tile(n, tm_pref, 16)
    cb = col0 // ncols
    assert col0 % ncols == 0
    aliased = stacked is not None
    in_specs = [pl.BlockSpec((tm, k), lambda i: (i, 0)),
                pl.BlockSpec((None, k, ncols), lambda i: (w_layer, 0, cb))]
    args = [lhs, w]
    if aliased:
        in_specs.append(pl.BlockSpec(memory_space=pl.ANY))
        args.append(stacked)
    vmem = 4 * tm * k * 2 + 4 * k * ncols * 2 + 4 * tm * ncols * 4 + 4 * tm * ncols * 2 + (4 << 20)
    return pl.pallas_call(
        functools.partial(_proj_heads_kernel, aliased=aliased),
        grid=(n // tm,), in_specs=in_specs,
        out_specs=[pl.BlockSpec((None, tm, heads, hd), lambda i: (layer, i, 0, 0)),
                   pl.BlockSpec((tm, ncols), lambda i: (i, 0))],
        out_shape=[jax.ShapeDtypeStruct((depth, n, heads, hd), F32),
                   jax.ShapeDtypeStruct((n, ncols), BF16)],
        input_output_aliases={2: 0} if aliased else {},
        compiler_params=_params(vmem, 1), name="proj_heads")(*args)


def _lam_value(lamc_ref, lq1_ref, lk1_ref, lq2_ref, lk2_ref):
    a = jnp.exp(jnp.sum(lq1_ref[...] * lk1_ref[...], axis=-1, keepdims=True))
    b = jnp.exp(jnp.sum(lq2_ref[...] * lk2_ref[...], axis=-1, keepdims=True))
    return a - b + lamc_ref[0]


def _lane_repeat(x, reps):
    return x if reps == 1 else jnp.concatenate([x] * reps, axis=1)


def _softmax_update(s, v, m_prev, l_prev, acc_prev):
    keys = s.shape[1]
    m_new = jnp.maximum(m_prev, jnp.max(s, axis=1, keepdims=True))
    alpha = jnp.exp2(m_prev - m_new)
    p = jnp.exp2(s - _lane_repeat(m_new, keys // V7X_LANES))
    l_new = alpha * l_prev + jnp.sum(p, axis=1, keepdims=True)
    pv = jnp.dot(p.astype(BF16), v, preferred_element_type=F32)
    acc_new = acc_prev * _lane_repeat(alpha, acc_prev.shape[1] // V7X_LANES) + pv
    return m_new, l_new, acc_new


def _softmax_step(s, v, m_ref, l_ref, acc_ref):
    m_ref[...], l_ref[...], acc_ref[...] = _softmax_update(s, v, m_ref[...], l_ref[...],
                                                           acc_ref[...])


def _diff_finish(a1, l1, a2, l2, lam, gain, out_scale):
    rep = a1.shape[1] // V7X_LANES
    o = a1 * _lane_repeat(1.0 / l1, rep) - lam * (a2 * _lane_repeat(1.0 / l2, rep))
    return _rms_rows(o, gain) * out_scale


def _attn_prompt_kernel(lamc_ref, lq1_ref, lk1_ref, lq2_ref, lk2_ref, q_ref, k_ref, v_ref, g_ref,
                        o_ref, m1, l1, a1, m2, l2, a2, sa1, sa2, sb1, sb2, ka1, ka2, ext, *, dh, tq):
    seq = q_ref.shape[0]
    tk = tq
    nq = seq // tq
    n_units = nq * (nq + 1) // 2
    shift = MASK_CHUNK.bit_length() - 1
    assert tq // MASK_CHUNK < dh
    maps = ((0, m1, l1, a1, ka1), (dh, m2, l2, a2, ka2))

    lane = lax.broadcasted_iota(jnp.int32, (tk, dh), 1)
    row_chunk = lax.shift_right_logical(lax.broadcasted_iota(jnp.int32, (tk, dh), 0), shift)
    key_ext = jnp.where(lane == 0, -1.0, jnp.where(lane - 1 == row_chunk, 1.0, 0.0)).astype(BF16)
    hidden = (lane >= 1) & (lane <= tq // MASK_CHUNK) & (lane - 1 > row_chunk)

    ones = jnp.ones((dh, dh), BF16)

    def widen(c, carry):
        rs = pl.ds(pl.multiple_of(c * tk, tk), tk)
        out = []
        for idx, (lo, _, l_ref, a_ref, ka) in enumerate(maps):
            l_ref[rs, :] = jnp.zeros((tk, V7X_LANES), F32)
            a_ref[rs, :] = jnp.zeros((tk, 2 * dh), F32)
            kc = k_ref[rs, lo:lo + dh]
            qc = q_ref[rs, lo:lo + dh]
            ka[rs, 0:dh] = kc
            ka[rs, dh:2 * dh] = key_ext
            k2 = jnp.dot(kc * kc, ones, preferred_element_type=F32)
            q2 = jnp.dot(qc * qc, ones, preferred_element_type=F32)
            out += [jnp.maximum(carry[2 * idx], jnp.max(k2, axis=0, keepdims=True)),
                    jnp.maximum(carry[2 * idx + 1], jnp.max(q2, axis=0, keepdims=True))]
        return tuple(out)
    norms2 = lax.fori_loop(0, nq, widen, (jnp.zeros((1, dh), F32),) * 4)

    worst = jnp.zeros((1, 1), F32)
    for idx in range(2):
        bound = jnp.sqrt(norms2[2 * idx][:, 0:1] * norms2[2 * idx + 1][:, 0:1]) * NORM_SLACK
        worst = jnp.maximum(worst, bound)
        base = jnp.where(lane == 0, bound, 0.0)
        ext[(2 * idx) * tq:(2 * idx + 1) * tq, :] = base.astype(BF16)
        ext[(2 * idx + 1) * tq:(2 * idx + 2) * tq, :] = jnp.where(hidden, NEG_INF, base).astype(BF16)
    fast = jnp.max(worst) <= SHIFT_LIMIT

    lam = _lam_value(lamc_ref, lq1_ref, lk1_ref, lq2_ref, lk2_ref)
    gain, out_scale = g_ref[...], lamc_ref[1]

    def q_rows(qi):
        return pl.ds(pl.multiple_of(qi * tq, tq), tq)

    def k_rows(j):
        return pl.ds(pl.multiple_of(j * tk, tk), tk)

    def scores(qi, j, bufs):
        diag = jnp.where(qi == j, 1, 0)
        for idx, ((lo, _, _, _, ka), s_scr) in enumerate(zip(maps, bufs)):
            e_rows = pl.ds(pl.multiple_of((2 * idx + diag) * tq, tq), tq)
            qa = jnp.concatenate([q_ref[q_rows(qi), lo:lo + dh], ext[e_rows, :]], axis=1)
            s_scr[...] = lax.dot_general(qa, ka[k_rows(j), :], NT_DIMS,
                                         preferred_element_type=F32)

    def consume(qi, j, bufs, exact):
        v = v_ref[k_rows(j), :]
        rq = q_rows(qi)
        for (_, m_ref, l_ref, a_ref, _), s_scr in zip(maps, bufs):
            s = s_scr[...]
            l_prev, a_prev = l_ref[rq, :], a_ref[rq, :]
            if exact:
                m_ref[rq, :], l_new, a_new = _softmax_update(s, v, m_ref[rq, :], l_prev, a_prev)
            else:
                p = jnp.exp2(s)
                l_new = l_prev + jnp.sum(p, axis=1, keepdims=True)
                a_new = a_prev + jnp.dot(p.astype(BF16), v, preferred_element_type=F32)
            l_ref[rq, :], a_ref[rq, :] = l_new, a_new

    def advance(qi, j):
        last = j == qi
        return jnp.where(last, qi + 1, qi), jnp.where(last, 0, j + 1)

    buf_a, buf_b = (sa1, sa2), (sb1, sb2)

    units = [(qi, j) for qi in range(nq) for j in range(qi + 1)]

    def sweep(exact, group):
        if exact:
            for m_ref in (m1, m2):
                m_ref[...] = jnp.full(m_ref.shape, -jnp.inf, F32)
        scores(0, 0, buf_a)

        def trip(t, carry):
            cur = carry
            for g in range(group):
                nxt = advance(*cur)
                over = nxt[0] >= nq
                safe = (jnp.where(over, nq - 1, nxt[0]), jnp.where(over, 0, nxt[1]))
                mine, other = (buf_a, buf_b) if g % 2 == 0 else (buf_b, buf_a)
                scores(*safe, other)
                consume(*cur, mine, exact)
                cur = nxt
            return cur
        n_trips = n_units // group
        lax.fori_loop(0, n_trips, trip, (jnp.int32(0), jnp.int32(0)))
        rest = units[n_trips * group:]
        for g, (qi, j) in enumerate(rest):
            mine, other = (buf_a, buf_b) if g % 2 == 0 else (buf_b, buf_a)
            if g + 1 < len(rest):
                scores(*rest[g + 1], other)
            consume(qi, j, mine, exact)

    @pl.when(fast)
    def _():
        sweep(False, FAST_SWEEP_GROUP)

    @pl.when(jnp.logical_not(fast))
    def _():
        sweep(True, 2)

    def finish(qi, carry):
        rq = q_rows(qi)
        o = _diff_finish(a1[rq, :], l1[rq, :], a2[rq, :], l2[rq, :], lam, gain, out_scale)
        o_ref[rq, :] = o.astype(BF16)
        return carry
    lax.fori_loop(0, nq, finish, 0)


def _lam_inputs(lam_init, lams):
    specs = [pl.BlockSpec(memory_space=pltpu.SMEM)]
    args = [jnp.array([lam_init, 1.0 - lam_init], F32)]
    for v in lams:
        specs.append(pl.BlockSpec((1, v.shape[-1]), lambda *_: (0, 0)))
        args.append(v.reshape(1, -1))
    return specs, args


def _attn_prompt(dq, k16, v16, gain, lam_init, lams, *, batch, seq, heads, tq_pref=512):
    n, width = k16.shape
    hd = width // heads
    dh = hd // 2
    tq = _pick_tile(seq, tq_pref, V7X_LANES)
    lam_specs, lam_args = _lam_inputs(lam_init, lams)
    per_head = pl.BlockSpec((seq, hd), lambda b, h: (b, h))
    in_specs = lam_specs + [per_head, per_head, per_head,
                            pl.BlockSpec((1, hd), lambda b, h: (0, 0))]
    stat = pltpu.VMEM((seq, V7X_LANES), F32)
    accs = pltpu.VMEM((seq, hd), F32)
    sbuf = pltpu.VMEM((tq, tq), F32)
    wide_k = pltpu.VMEM((seq, hd), BF16)
    vmem = 8 * seq * hd * 2 + 2 * seq * hd * 2 + 2 * seq * hd * 4 + 4 * seq * V7X_LANES * 4 \
        + 12 * tq * tq * 4 + (4 << 20)
    return pl.pallas_call(
        functools.partial(_attn_prompt_kernel, dh=dh, tq=tq),
        grid=(batch, heads), in_specs=in_specs, out_specs=per_head,
        out_shape=jax.ShapeDtypeStruct((n, width), BF16),
        scratch_shapes=[stat, stat, accs, stat, stat, accs, sbuf, sbuf, sbuf, sbuf,
                        wide_k, wide_k, pltpu.VMEM((4 * tq, dh), BF16)],
        compiler_params=_params(vmem, 2), name="diff_attn_prompt",
    )(*lam_args, dq, k16, v16, gain.reshape(1, hd))


def _attn_decode_kernel(lamc_ref, lq1_ref, lk1_ref, lq2_ref, lk2_ref, q_ref, kn_ref, vn_ref,
                        kc_ref, vc_ref, g_ref, o_ref, qd, m_scr, l_scr, a_scr,
                        *, heads, dh, n_new):
    j = pl.program_id(1)
    hd = 2 * dh
    s_q = q_ref.shape[0]

    @pl.when(j == 0)
    def _():
        qd[...] = jnp.zeros(qd.shape, BF16)
        m_scr[...] = jnp.full(m_scr.shape, -jnp.inf, F32)
        l_scr[...] = jnp.zeros(l_scr.shape, F32)
        a_scr[...] = jnp.zeros(a_scr.shape, F32)
        new_mask = lax.broadcasted_iota(jnp.int32, (2 * s_q, kn_ref.shape[0]), 1) < n_new
        for h in range(heads):
            hs = slice(h * hd, (h + 1) * hd)
            qd[h, 0:s_q, 0:dh] = q_ref[:, h * hd:h * hd + dh]
            qd[h, s_q:2 * s_q, dh:hd] = q_ref[:, h * hd + dh:(h + 1) * hd]
            s = lax.dot_general(qd[h], kn_ref[:, hs], NT_DIMS, preferred_element_type=F32)
            s = jnp.where(new_mask, s, NEG_INF)
            _softmax_step(s, vn_ref[:, hs], m_scr.at[h], l_scr.at[h], a_scr.at[h])

    tk = kc_ref.shape[0]
    k_all = kc_ref[...].astype(BF16).reshape(tk, heads * hd)
    v_all = vc_ref[...].astype(BF16).reshape(tk, heads * hd)
    scores = [lax.dot_general(qd[h], k_all[:, h * hd:(h + 1) * hd], NT_DIMS,
                              preferred_element_type=F32) for h in range(heads)]
    for h in range(heads):
        _softmax_step(scores[h], v_all[:, h * hd:(h + 1) * hd], m_scr.at[h], l_scr.at[h],
                      a_scr.at[h])

    @pl.when(j == pl.num_programs(1) - 1)
    def _():
        lam = _lam_value(lamc_ref, lq1_ref, lk1_ref, lq2_ref, lk2_ref)
        for h in range(heads):
            a, l = a_scr[h], l_scr[h]
            o = _diff_finish(a[:s_q], l[:s_q], a[s_q:], l[s_q:], lam, g_ref[...], lamc_ref[1])
            o_ref[:, h * hd:(h + 1) * hd] = o.astype(BF16)


def _attn_decode(dq, k16, v16, cache_k, cache_v, layer, gain, lam_init, lams, *, batch, seq,
                 heads, tk_pref=512):
    n, width = k16.shape
    hd = width // heads
    dh = hd // 2
    past = cache_k.shape[2]
    tk = _pick_tile(past, tk_pref, V7X_LANES)
    pad = V7X_LANES
    kn = jnp.pad(k16.reshape(batch, seq, width), ((0, 0), (0, pad - seq), (0, 0)))
    vn = jnp.pad(v16.reshape(batch, seq, width), ((0, 0), (0, pad - seq), (0, 0)))
    lam_specs, lam_args = _lam_inputs(lam_init, lams)
    in_specs = lam_specs + [
        pl.BlockSpec((seq, width), lambda b, j: (b, 0)),
        pl.BlockSpec((None, pad, width), lambda b, j: (b, 0, 0)),
        pl.BlockSpec((None, pad, width), lambda b, j: (b, 0, 0)),
        pl.BlockSpec((None, None, tk, heads, hd), lambda b, j: (layer, b, j, 0, 0)),
        pl.BlockSpec((None, None, tk, heads, hd), lambda b, j: (layer, b, j, 0, 0)),
        pl.BlockSpec((1, hd), lambda b, j: (0, 0)),
    ]
    scratch = [pltpu.VMEM((heads, 2 * seq, hd), BF16),
               pltpu.VMEM((heads, 2 * seq, V7X_LANES), F32),
               pltpu.VMEM((heads, 2 * seq, V7X_LANES), F32),
               pltpu.VMEM((heads, 2 * seq, hd), F32)]
    vmem = 4 * tk * width * 4 + 2 * tk * width * 2 + 8 * pad * width * 2 + (8 << 20)
    return pl.pallas_call(
        functools.partial(_attn_decode_kernel, heads=heads, dh=dh, n_new=seq),
        grid=(batch, past // tk), in_specs=in_specs,
        out_specs=pl.BlockSpec((seq, width), lambda b, j: (b, 0)),
        out_shape=jax.ShapeDtypeStruct((n, width), BF16),
        scratch_shapes=scratch, compiler_params=_params(vmem, 2), name="diff_attn_decode",
    )(*lam_args, dq, kn, vn, cache_k, cache_v, gain.reshape(1, hd))


def _ret_tables(heads, chunk, dk, dv):
    log_gamma = jnp.log1p(-jnp.exp2(-5.0 - jnp.arange(heads, dtype=F32)))
    idx = jnp.arange(chunk, dtype=F32)
    diff = idx[:, None] - idx[None, :]
    decay = jnp.where(diff >= 0, jnp.exp(log_gamma[:, None, None] * jnp.maximum(diff, 0.0)), 0.0)
    cross = jnp.exp(log_gamma[:, None] * (idx + 1.0)[None, :])
    kdec = jnp.exp(log_gamma[:, None] * (chunk - 1.0 - idx)[None, :])
    carry = jnp.exp(log_gamma * chunk)
    return (decay,
            jnp.broadcast_to(cross[:, :, None], (heads, chunk, dv)),
            jnp.broadcast_to(kdec[:, :, None], (heads, chunk, dk)),
            carry)


def _ret_kernel(*refs, has_init, n_cast):
    refs = list(refs)
    carry_ref, q_ref, k_ref, v_ref, rg_ref, dec_ref, cross_ref, kdec_ref, g_ref = refs[:9]
    refs = refs[9:]
    s0_ref = refs.pop(0) if has_init else None
    cast_src = [refs.pop(0) for _ in range(n_cast)]
    o_ref, sout_ref = refs.pop(0), refs.pop(0)
    cast_dst = [refs.pop(0) for _ in range(n_cast)]
    (s_scr,) = refs
    heads, dk, dv = s_scr.shape
    for src, dst in zip(cast_src, cast_dst):
        dst[...] = src[...].astype(BF16)

    @pl.when(pl.program_id(1) == 0)
    def _():
        s_scr[...] = s0_ref[...] if has_init else jnp.zeros(s_scr.shape, F32)

    for h in range(heads):
        ks, vs = slice(h * dk, (h + 1) * dk), slice(h * dv, (h + 1) * dv)
        q, k, v = q_ref[:, ks], k_ref[:, ks], v_ref[:, vs]
        state = s_scr[h]
        scores = lax.dot_general(q, k, NT_DIMS, preferred_element_type=F32) * dec_ref[h]
        inner = jnp.dot(scores.astype(BF16), v, preferred_element_type=F32)
        cross = jnp.dot(q, state.astype(BF16), preferred_element_type=F32) * cross_ref[h]
        k_dec = (k.astype(F32) * kdec_ref[h]).astype(BF16)
        new_state = carry_ref[h] * state + lax.dot_general(k_dec, v, TN_DIMS,
                                                           preferred_element_type=F32)
        s_scr[h] = new_state
        sout_ref[h] = new_state
        gate = rg_ref[:, vs].astype(F32)
        y = _rms_rows(inner + cross, g_ref[...]) * (gate * jax.nn.sigmoid(gate))
        o_ref[:, vs] = y.astype(BF16)


def _cast_slices(weights, layer, n_steps, step_of):
    in_specs, out_specs, out_shapes, vmem = [], [], [], 0
    for w in weights:
        rows, cols = w.shape[1:]
        if rows % (16 * n_steps):
            return None
        r = rows // n_steps
        in_specs.append(pl.BlockSpec((None, r, cols), lambda *g: (layer, step_of(*g), 0)))
        out_specs.append(pl.BlockSpec((r, cols), lambda *g: (step_of(*g), 0)))
        out_shapes.append(jax.ShapeDtypeStruct((rows, cols), BF16))
        vmem += 2 * r * cols * (4 + 2)
    return in_specs, out_specs, out_shapes, vmem


def _retention(rqk, main, c_v, c_gate, dv, gain, state0, layer, *, batch, seq, heads,
               chunk_pref=256, cast=()):
    n = rqk.shape[0]
    dk = rqk.shape[1] // (2 * heads)
    assert c_v % (heads * dv) == 0 and c_gate % (heads * dv) == 0
    vb, gb = c_v // (heads * dv), c_gate // (heads * dv)
    chunk = _pick_tile(seq, chunk_pref, 16)
    nc = seq // chunk
    decay, cross, kdec, carry = _ret_tables(heads, chunk, dk, dv)
    has_init = state0 is not None
    row = lambda b, c: b * nc + c
    whole = lambda shape: pl.BlockSpec(shape, lambda b, c: (0,) * len(shape))
    in_specs = [
        pl.BlockSpec(memory_space=pltpu.SMEM),
        pl.BlockSpec((chunk, heads * dk), lambda b, c: (row(b, c), 0)),
        pl.BlockSpec((chunk, heads * dk), lambda b, c: (row(b, c), 1)),
        pl.BlockSpec((chunk, heads * dv), lambda b, c: (row(b, c), vb)),
        pl.BlockSpec((chunk, heads * dv), lambda b, c: (row(b, c), gb)),
        whole((heads, chunk, chunk)), whole((heads, chunk, dv)), whole((heads, chunk, dk)),
        whole((1, dv)),
    ]
    args = [carry, rqk, rqk, main, main, decay, cross, kdec, gain.reshape(1, dv)]
    if has_init:
        in_specs.append(pl.BlockSpec((None, None, heads, dk, dv),
                                     lambda b, c: (layer, b, 0, 0, 0)))
        args.append(state0)
    vmem = (2 * heads * chunk * (chunk + dv + dk) * 4 + 8 * chunk * heads * (dk + dv) * 2
            + 6 * heads * dk * dv * 4 + (8 << 20))
    out_specs = [pl.BlockSpec((chunk, heads * dv), lambda b, c: (row(b, c), 0)),
                 pl.BlockSpec((None, heads, dk, dv), lambda b, c: (b, 0, 0, 0))]
    out_shape = [jax.ShapeDtypeStruct((n, heads * dv), BF16),
                 jax.ShapeDtypeStruct((batch, heads, dk, dv), F32)]
    slices = _cast_slices(cast, layer, batch * nc, row) if cast else None
    if slices:
        in_specs += slices[0]
        args += list(cast)
        out_specs += slices[1]
        out_shape += slices[2]
        vmem += slices[3]
    outs = pl.pallas_call(
        functools.partial(_ret_kernel, has_init=has_init, n_cast=len(cast) if slices else 0),
        grid=(batch, nc), in_specs=in_specs, out_specs=out_specs, out_shape=out_shape,
        scratch_shapes=[pltpu.VMEM((heads, dk, dv), F32)],
        compiler_params=_params(vmem, 2), name="retention",
    )(*args)
    casted = tuple(outs[2:]) if slices else tuple(w[layer].astype(BF16) for w in cast)
    return outs[0], outs[1], casted


def _merge_kernel(oa_ref, ob_ref, wa_ref, wb_ref, ga_ref, gb_ref, *refs, w_scale):
    if w_scale is not None:
        wsrc_ref, o_ref, wdst_ref = refs
        col = lax.broadcasted_iota(jnp.int32, (1, wsrc_ref.shape[1]), 1)
        wdst_ref[...] = (wsrc_ref[...] * jnp.where(col < w_scale[0], w_scale[1], 1.0)).astype(BF16)
    else:
        (o_ref,) = refs
    a = jnp.dot(oa_ref[...], wa_ref[...], preferred_element_type=F32)
    b = jnp.dot(ob_ref[...], wb_ref[...], preferred_element_type=F32)
    ga = jax.nn.sigmoid(ga_ref[...].astype(F32))
    gb = jax.nn.sigmoid(gb_ref[...].astype(F32))
    o_ref[...] = (ga * a + gb * b).astype(BF16)


def _merge(oa, ob, main, c_ga, c_gb, w_a, w_b, layer, *, tm_pref=1024, tn_pref=512,
           cast_w_in=None):
    n, k = oa.shape
    d = w_a.shape[2]
    tm = _pick_tile(n, tm_pref, 16)
    tn = _pick_tile(d, tn_pref, V7X_LANES)
    assert c_ga % tn == 0 and c_gb % tn == 0
    ga0, gb0 = c_ga // tn, c_gb // tn
    n_j = d // tn
    in_specs = [pl.BlockSpec((tm, k), lambda i, j: (i, 0)),
                pl.BlockSpec((tm, k), lambda i, j: (i, 0)),
                pl.BlockSpec((None, k, tn), lambda i, j: (layer, 0, j)),
                pl.BlockSpec((None, k, tn), lambda i, j: (layer, 0, j)),
                pl.BlockSpec((tm, tn), lambda i, j: (i, ga0 + j)),
                pl.BlockSpec((tm, tn), lambda i, j: (i, gb0 + j))]
    args = [oa, ob, w_a, w_b, main, main]
    out_specs = [pl.BlockSpec((tm, tn), lambda i, j: (i, j))]
    out_shape = [jax.ShapeDtypeStruct((n, d), BF16)]
    vmem = 4 * tm * k * 2 + 4 * k * tn * 2 + 6 * tm * tn * 2 + 4 * tm * tn * 4 + (4 << 20)
    slices = None
    if cast_w_in is not None:
        slices = _cast_slices(cast_w_in[:1], cast_w_in[1], (n // tm) * n_j,
                              lambda i, j: i * n_j + j)
    if slices:
        in_specs += slices[0]
        args.append(cast_w_in[0])
        out_specs += slices[1]
        out_shape += slices[2]
        vmem += slices[3]
    outs = pl.pallas_call(
        functools.partial(_merge_kernel, w_scale=cast_w_in[2:] if slices else None),
        grid=(n // tm, n_j), in_specs=in_specs, out_specs=out_specs, out_shape=out_shape,
        compiler_params=_params(vmem, 2), name="branch_merge",
    )(*args)
    if cast_w_in is None:
        return outs[0], None
    if slices:
        return outs[0], outs[1]
    w, w_layer, q_cols, q_scale = cast_w_in
    return outs[0], _cast_w_in(w[w_layer:w_layer + 1], q_cols, q_scale)[0]


def _ffn_kernel(*refs, row_chunk, final_norm):
    refs = list(refs)
    x_ref, g_ref, wu_ref, wd_ref = refs[:4]
    fg_ref = refs[4] if final_norm else None
    o_ref, h_scr = refs[-2:]
    f = pl.program_id(1)
    n_chunks = x_ref.shape[0] // row_chunk

    @pl.when(f == 0)
    def _():
        def rows(r, carry):
            rs = pl.ds(pl.multiple_of(r * row_chunk, row_chunk), row_chunk)
            x = x_ref[rs, :]
            h_scr[rs, :] = _rms_rows(x, g_ref[...]).astype(BF16)
            o_ref[rs, :] = x
            return carry
        lax.fori_loop(0, n_chunks, rows, 0, unroll=2)

    u = jnp.dot(h_scr[...], wu_ref[...], preferred_element_type=F32)
    u = jnp.square(jnp.maximum(u, 0.0)).astype(BF16)
    o_ref[...] += jnp.dot(u, wd_ref[...], preferred_element_type=F32)

    if final_norm:
        @pl.when(f == pl.num_programs(1) - 1)
        def _():
            def rows(r, carry):
                rs = pl.ds(pl.multiple_of(r * row_chunk, row_chunk), row_chunk)
                o_ref[rs, :] = _rms_rows(o_ref[rs, :], fg_ref[...])
                return carry
            lax.fori_loop(0, n_chunks, rows, 0)


def _ffn(x, gain, w_up, w_down, layer, final_gain=None, *, tm_pref=1024, tf_pref=512):
    n, d = x.shape
    d_ff = w_up.shape[2]
    tm = _pick_tile(n, tm_pref, 16)
    tf = _pick_tile(d_ff, tf_pref, V7X_LANES)
    final_norm = final_gain is not None
    in_specs = [pl.BlockSpec((tm, d), lambda i, f: (i, 0)),
                pl.BlockSpec((1, d), lambda i, f: (0, 0)),
                pl.BlockSpec((None, d, tf), lambda i, f: (layer, 0, f)),
                pl.BlockSpec((None, tf, d), lambda i, f: (layer, f, 0))]
    args = [x, gain.reshape(1, d), w_up, w_down]
    if final_norm:
        in_specs.append(pl.BlockSpec((1, d), lambda i, f: (0, 0)))
        args.append(final_gain.reshape(1, d))
    vmem = 4 * tm * d * 4 + tm * d * 2 + 4 * d * tf * 2 + 2 * tm * tf * 4 + (6 << 20)
    return pl.pallas_call(
        functools.partial(_ffn_kernel, row_chunk=min(tm, 64), final_norm=final_norm),
        grid=(n // tm, d_ff // tf), in_specs=in_specs,
        out_specs=pl.BlockSpec((tm, d), lambda i, f: (i, 0)),
        out_shape=jax.ShapeDtypeStruct((n, d), F32),
        scratch_shapes=[pltpu.VMEM((tm, d), BF16)],
        compiler_params=_params(vmem, 2), name="ffn",
    )(*args)


def _rope_tables(pos, dk, batch):
    inv = 1.0 / (ROPE_BASE ** jnp.linspace(0.0, 1.0, dk // 2, dtype=F32))
    th = pos.astype(F32)[:, None] * jnp.repeat(inv, 2)[None, :]
    sin, cos = jnp.sin(th), jnp.cos(th)
    even = (jnp.arange(dk) % 2) == 0
    sin_next = jnp.where(even, -sin, 0.0)
    sin_prev = jnp.where(even, 0.0, sin)
    return tuple(jnp.tile(t, (batch, 1)) for t in (cos, sin_next, sin_prev))


def _cast_w_in(w_in, diff_w, q_scale):
    col_scale = jnp.where(jnp.arange(w_in.shape[-1]) < diff_w, q_scale, 1.0).astype(F32)
    return (w_in * col_scale).astype(BF16)


def _mixer(x, layer, w, w_in, p, rope, past, kv_stack, side, *, depth, batch, seq, heads,
           ret_heads):
    d = x.shape[1]
    hd = p["subln_g"].shape[-1]
    diff_w = heads * hd
    dk = p["state"].shape[-2]
    dv = p["state"].shape[-1]
    qk_w = ret_heads * dk
    ret_w = ret_heads * dv
    c_k, c_v, c_rqk = diff_w, 2 * diff_w, 3 * diff_w
    skipped = 2 * diff_w + 2 * qk_w
    c_rv, c_rg, c_ga, c_gb = diff_w, diff_w + ret_w, diff_w + 2 * ret_w, diff_w + 2 * ret_w + d
    main_w = c_gb + d

    main, h = _proj(x, *w_in, 0, main_w, gain=p["norm_mix_g"], emit_h=True,
                    gap=(diff_w, skipped))
    k32, k16 = _proj_heads(h, *w_in, c_k, heads, kv_stack[0], depth, layer)
    v32, v16 = _proj_heads(h, *w_in, c_v, heads, kv_stack[1], depth, layer)
    rqk = _proj(h, *w_in, c_rqk, 2 * qk_w, epi="rope", extra=rope,
                rope_scale=dk ** -0.5, tn_pref=qk_w)

    lam_init = 0.8 - 0.6 * math.exp(-0.3 * layer)
    lams = (p["lq1"], p["lk1"], p["lq2"], p["lk2"])
    cast_w_in, ffn_w = None, ()
    if past:
        oa = _attn_decode(main, k16, v16, p["cache_k"], p["cache_v"], layer, p["subln_g"],
                          lam_init, lams, batch=batch, seq=seq, heads=heads)
        ob, st, _ = _retention(rqk, main, c_rv, c_rg, dv, p["ret_g"], p["state"], layer,
                               batch=batch, seq=seq, heads=ret_heads)
    else:
        if layer + 1 < depth:
            cast_w_in = (side["in"], layer + 1, diff_w, side["q_scale"])
        oa = _attn_prompt(main, k16, v16, p["subln_g"], lam_init, lams, batch=batch,
                          seq=seq, heads=heads)
        ob, st, ffn_w = _retention(rqk, main, c_rv, c_rg, dv, p["ret_g"], None, layer,
                                   batch=batch, seq=seq, heads=ret_heads,
                                   cast=(side["up"], side["down"]))
    merged, w_in_next = _merge(oa, ob, main, c_ga, c_gb, w["a"], w["b"], layer,
                               cast_w_in=cast_w_in)
    x = _proj(merged, w["out"], layer, 0, d, epi="residual", extra=(x,), tm_pref=512, tn_pref=d)
    return x, (k32, v32), st, w_in_next, ffn_w


def kernel(x_prompt, x_sample, cache_diff_k, cache_diff_v, state_ret, norm_mix_g, w_in,
           lambda_q1, lambda_k1, lambda_q2, lambda_k2, diff_subln_g, ret_norm_g,
           w_branch_a, w_branch_b, w_out, norm_ffn_g, w_up, w_down, final_norm_g):
    depth, dec_batch, past_len, heads, hd = cache_diff_k.shape
    batch, seq, d = x_prompt.shape
    dec_seq = x_sample.shape[1]
    ret_heads, dk, dv = state_ret.shape[2:]

    q_scale = (hd // 2) ** -0.5 * LOG2E
    w = {"a": w_branch_a.astype(BF16), "b": w_branch_b.astype(BF16), "out": w_out.astype(BF16)}
    side = {"in": w_in, "up": w_up, "down": w_down, "q_scale": q_scale}
    w_in_cur = _cast_w_in(w_in[0:1], heads * hd, q_scale)
    cache_k, cache_v = cache_diff_k, cache_diff_v
    rope_p = _rope_tables(jnp.arange(seq), dk, batch)
    rope_s = _rope_tables(past_len + jnp.arange(dec_seq), dk, dec_batch)

    xp = x_prompt.reshape(batch * seq, d)
    xs = x_sample.reshape(dec_batch * dec_seq, d)
    kv_p, kv_s = (None, None), (None, None)
    states_p, states_s = [], []
    for l in range(depth):
        p = {"norm_mix_g": norm_mix_g[l], "lq1": lambda_q1[l], "lk1": lambda_k1[l],
             "lq2": lambda_q2[l], "lk2": lambda_k2[l], "subln_g": diff_subln_g[l],
             "ret_g": ret_norm_g[l], "cache_k": cache_k, "cache_v": cache_v, "state": state_ret}
        final_g = final_norm_g if l == depth - 1 else None
        xp, kv_p, sp, w_in_next, (wu, wd) = _mixer(
            xp, l, w, (w_in_cur, 0), p, rope_p, False, kv_p, side, depth=depth, batch=batch,
            seq=seq, heads=heads, ret_heads=ret_heads)
        xs, kv_s, ss, _, _ = _mixer(
            xs, l, w, (w_in_cur, 0), p, rope_s, True, kv_s, None, depth=depth, batch=dec_batch,
            seq=dec_seq, heads=heads, ret_heads=ret_heads)
        xp = _ffn(xp, norm_ffn_g[l], wu[None], wd[None], 0, final_g)
        xs = _ffn(xs, norm_ffn_g[l], wu[None], wd[None], 0, final_g)
        if w_in_next is not None:
            w_in_cur = w_in_next[None]
        states_p.append(sp)
        states_s.append(ss)
    return (xp.reshape(batch, seq, d), xs.reshape(dec_batch, dec_seq, d),
            kv_p[0].reshape(depth, batch, seq, heads, hd),
            kv_p[1].reshape(depth, batch, seq, heads, hd), jnp.stack(states_p),
            kv_s[0].reshape(depth, dec_batch, dec_seq, heads, hd),
            kv_s[1].reshape(depth, dec_batch, dec_seq, heads, hd), jnp.stack(states_s))
```

```python
import functools
import math

import jax
import jax.numpy as jnp
from jax import lax
from jax.experimental import pallas as pl
from jax.experimental.pallas import tpu as pltpu

F32 = jnp.float32
BF16 = jnp.bfloat16

NORM_EPS = 1e-6
ROPE_BASE = 10000.0
MASK_CHUNK = 64
NEG_INF = -1e30
SHIFT_LIMIT = 48.0
NORM_SLACK = 1.02
FAST_SWEEP_GROUP = 6
LOG2E = 1.4426950408889634

V7X_VMEM_BYTES = 64 * 1024 * 1024
V7X_LANES = 128
VMEM_REQUEST_CAP = V7X_VMEM_BYTES - 8 * 1024 * 1024

NT_DIMS = (((1,), (1,)), ((), ()))
TN_DIMS = (((0,), (0,)), ((), ()))


def _pick_tile(n, pref, align):
    if n <= pref:
        return n
    t = pref - pref % align
    while t >= align:
        if n % t == 0:
            return t
        t -= align
    raise ValueError(f"no tile for {n} (pref {pref}, align {align})")


def _params(vmem_bytes, n_grid):
    return pltpu.CompilerParams(
        dimension_semantics=("arbitrary",) * n_grid,
        vmem_limit_bytes=int(min(VMEM_REQUEST_CAP, vmem_bytes)))


def _rms_rows(x, gain):
    ms = jnp.mean(x * x, axis=-1, keepdims=True)
    return x * lax.rsqrt(ms + NORM_EPS) * gain


def _proj_kernel(*refs, has_norm, emit_h, epi, row_chunk, rope_scale):
    refs = list(refs)
    x_ref = refs.pop(0)
    g_ref = refs.pop(0) if has_norm else None
    w_ref = refs.pop(0)
    if epi == "rope":
        cos_ref, sina_ref, sinb_ref = refs.pop(0), refs.pop(0), refs.pop(0)
    if epi == "residual":
        res_ref = refs.pop(0)
    o_ref = refs.pop(0)
    hout_ref = refs.pop(0) if emit_h else None
    h_scr = refs.pop(0) if has_norm else None

    j = pl.program_id(1)
    if has_norm:
        @pl.when(j == 0)
        def _():
            def rows(r, carry):
                rs = pl.ds(pl.multiple_of(r * row_chunk, row_chunk), row_chunk)
                hrow = _rms_rows(x_ref[rs, :], g_ref[...]).astype(BF16)
                h_scr[rs, :] = hrow
                if emit_h:
                    hout_ref[rs, :] = hrow
                return carry
            lax.fori_loop(0, x_ref.shape[0] // row_chunk, rows, 0, unroll=2)
        lhs = h_scr[...]
    else:
        lhs = x_ref[...]

    acc = jnp.dot(lhs, w_ref[...], preferred_element_type=F32)
    if epi == "bf16":
        o_ref[...] = acc.astype(BF16)
    elif epi == "residual":
        o_ref[...] = res_ref[...] + acc
    elif epi == "rope":
        n_tiles = pl.num_programs(1)
        scale = jnp.where(j < n_tiles // 2, 1.0, rope_scale).astype(F32)
        cos, sina, sinb = cos_ref[...], sina_ref[...], sinb_ref[...]
        for c in range(acc.shape[1] // V7X_LANES):
            cs = slice(c * V7X_LANES, (c + 1) * V7X_LANES)
            xs = acc[:, cs]
            nxt = pltpu.roll(xs, V7X_LANES - 1, 1)
            prv = pltpu.roll(xs, 1, 1)
            o_ref[:, cs] = ((xs * cos + nxt * sina + prv * sinb) * scale).astype(BF16)
    else:
        raise ValueError(epi)


def _proj(lhs, w, layer, col0, ncols, *, gain=None, emit_h=False, epi="bf16", extra=(),
          rope_scale=1.0, gap=(0, 0), tm_pref=1024, tn_pref=1024):
    n, k = lhs.shape
    has_norm = gain is not None
    tm = _pick_tile(n, tm_pref, 16)
    tn = _pick_tile(ncols, tn_pref, V7X_LANES)
    assert col0 % tn == 0 and gap[0] % tn == 0 and gap[1] % tn == 0
    cb = col0 // tn
    lead_tiles, skip_tiles = gap[0] // tn, gap[1] // tn
    grid = (n // tm, ncols // tn)

    def w_tile(j):
        return cb + j + jnp.where(j >= lead_tiles, skip_tiles, 0) if skip_tiles else cb + j

    in_specs = [pl.BlockSpec((tm, k), lambda i, j: (i, 0))]
    args = [lhs]
    if has_norm:
        in_specs.append(pl.BlockSpec((1, k), lambda i, j: (0, 0)))
        args.append(gain.reshape(1, k))
    in_specs.append(pl.BlockSpec((None, k, tn), lambda i, j: (layer, 0, w_tile(j))))
    args.append(w)
    if epi == "rope":
        for t in extra:
            in_specs.append(pl.BlockSpec((tm, V7X_LANES), lambda i, j: (i, 0)))
            args.append(t)
    elif epi == "residual":
        in_specs.append(pl.BlockSpec((tm, tn), lambda i, j: (i, j)))
        args.append(extra[0])

    tile_spec = pl.BlockSpec((tm, tn), lambda i, j: (i, j))
    out_dtype = F32 if epi == "residual" else BF16
    out_shape = [jax.ShapeDtypeStruct((n, ncols), out_dtype)]
    out_specs = [tile_spec]
    out_bytes = tm * tn * jnp.dtype(out_dtype).itemsize
    if emit_h:
        out_shape.append(jax.ShapeDtypeStruct((n, k), BF16))
        out_specs.append(pl.BlockSpec((tm, k), lambda i, j: (i, 0)))
        out_bytes += tm * k * 2
    scratch = [pltpu.VMEM((tm, k), BF16)] if has_norm else []

    vmem = (2 * tm * k * lhs.dtype.itemsize + 2 * k * tn * 2 + 2 * out_bytes
            + (tm * k * 2 if has_norm else 0) + 3 * tm * tn * 4
            + (2 * tm * tn * 4 if epi == "residual" else 0) + (4 << 20))
    outs = pl.pallas_call(
        functools.partial(_proj_kernel, has_norm=has_norm, emit_h=emit_h, epi=epi,
                          row_chunk=min(tm, 64), rope_scale=rope_scale),
        grid=grid, in_specs=in_specs, out_specs=out_specs, out_shape=out_shape,
        scratch_shapes=scratch, compiler_params=_params(vmem, 2),
        name=f"proj_{epi}")(*args)
    return outs[0] if len(outs) == 1 else tuple(outs)


def _proj_heads_kernel(*refs, aliased, has_cast):
    x_ref, w_ref = refs[0], refs[1]
    if has_cast:
        src_ref, o32_ref, o16_ref, dst_ref = refs[-4:]
        dst_ref[...] = src_ref[...].astype(BF16)
    else:
        o32_ref, o16_ref = refs[-2], refs[-1]
    acc = jnp.dot(x_ref[...], w_ref[...], preferred_element_type=F32)
    o16_ref[...] = acc.astype(BF16)
    o32_ref[...] = acc.reshape(o32_ref.shape)


def _proj_heads(lhs, w, w_layer, col0, heads, stacked, depth, layer, *, tm_pref=512, cast=None):
    n, k = lhs.shape
    ncols = k
    hd = ncols // heads
    tm = _pick_tile(n, tm_pref, 16)
    cb = col0 // ncols
    assert col0 % ncols == 0
    aliased = stacked is not None
    in_specs = [pl.BlockSpec((tm, k), lambda i: (i, 0)),
                pl.BlockSpec((None, k, ncols), lambda i: (w_layer, 0, cb))]
    args = [lhs, w]
    if aliased:
        in_specs.append(pl.BlockSpec(memory_space=pl.ANY))
        args.append(stacked)
    vmem = 4 * tm * k * 2 + 4 * k * ncols * 2 + 4 * tm * ncols * 4 + 4 * tm * ncols * 2 + (4 << 20)
    out_specs = [pl.BlockSpec((None, tm, heads, hd), lambda i: (layer, i, 0, 0)),
                 pl.BlockSpec((tm, ncols), lambda i: (i, 0))]
    out_shape = [jax.ShapeDtypeStruct((depth, n, heads, hd), F32),
                 jax.ShapeDtypeStruct((n, ncols), BF16)]
    slices = _cast_slices((cast,), layer, n // tm, lambda i: i) if cast is not None else None
    if slices:
        in_specs += slices[0]
        args.append(cast)
        out_specs += slices[1]
        out_shape += slices[2]
        vmem += slices[3]
    outs = pl.pallas_call(
        functools.partial(_proj_heads_kernel, aliased=aliased, has_cast=bool(slices)),
        grid=(n // tm,), in_specs=in_specs, out_specs=out_specs, out_shape=out_shape,
        input_output_aliases={2: 0} if aliased else {},
        compiler_params=_params(vmem, 1), name="proj_heads")(*args)
    if cast is None:
        return outs[0], outs[1], None
    return outs[0], outs[1], outs[2] if slices else cast[layer].astype(BF16)


def _lam_value(lamc_ref, lq1_ref, lk1_ref, lq2_ref, lk2_ref):
    a = jnp.exp(jnp.sum(lq1_ref[...] * lk1_ref[...], axis=-1, keepdims=True))
    b = jnp.exp(jnp.sum(lq2_ref[...] * lk2_ref[...], axis=-1, keepdims=True))
    return a - b + lamc_ref[0]


def _lane_repeat(x, reps):
    return x if reps == 1 else jnp.concatenate([x] * reps, axis=1)


def _softmax_update(s, v, m_prev, l_prev, acc_prev):
    keys = s.shape[1]
    m_new = jnp.maximum(m_prev, jnp.max(s, axis=1, keepdims=True))
    alpha = jnp.exp2(m_prev - m_new)
    p = jnp.exp2(s - _lane_repeat(m_new, keys // V7X_LANES))
    l_new = alpha * l_prev + jnp.sum(p, axis=1, keepdims=True)
    pv = jnp.dot(p.astype(BF16), v, preferred_element_type=F32)
    acc_new = acc_prev * _lane_repeat(alpha, acc_prev.shape[1] // V7X_LANES) + pv
    return m_new, l_new, acc_new


def _softmax_step(s, v, m_ref, l_ref, acc_ref):
    m_ref[...], l_ref[...], acc_ref[...] = _softmax_update(s, v, m_ref[...], l_ref[...],
                                                           acc_ref[...])


def _diff_finish(a1, l1, a2, l2, lam, gain, out_scale):
    rep = a1.shape[1] // V7X_LANES
    o = a1 * _lane_repeat(1.0 / l1, rep) - lam * (a2 * _lane_repeat(1.0 / l2, rep))
    return _rms_rows(o, gain) * out_scale


def _attn_prompt_kernel(lamc_ref, lq1_ref, lk1_ref, lq2_ref, lk2_ref, q_ref, k_ref, v_ref, g_ref,
                        o_ref, m1, l1, a1, m2, l2, a2, sa1, sa2, sb1, sb2, ka1, ka2, ext, *, dh, tq):
    seq = q_ref.shape[0]
    tk = tq
    nq = seq // tq
    n_units = nq * (nq + 1) // 2
    shift = MASK_CHUNK.bit_length() - 1
    assert tq // MASK_CHUNK < dh
    maps = ((0, m1, l1, a1, ka1), (dh, m2, l2, a2, ka2))

    lane = lax.broadcasted_iota(jnp.int32, (tk, dh), 1)
    row_chunk = lax.shift_right_logical(lax.broadcasted_iota(jnp.int32, (tk, dh), 0), shift)
    key_ext = jnp.where(lane == 0, -1.0, jnp.where(lane - 1 == row_chunk, 1.0, 0.0)).astype(BF16)
    hidden = (lane >= 1) & (lane <= tq // MASK_CHUNK) & (lane - 1 > row_chunk)

    ones = jnp.ones((dh, dh), BF16)

    def widen(c, carry):
        rs = pl.ds(pl.multiple_of(c * tk, tk), tk)
        out = []
        for idx, (lo, _, l_ref, a_ref, ka) in enumerate(maps):
            l_ref[rs, :] = jnp.zeros((tk, V7X_LANES), F32)
            a_ref[rs, :] = jnp.zeros((tk, 2 * dh), F32)
            kc = k_ref[rs, lo:lo + dh]
            qc = q_ref[rs, lo:lo + dh]
            ka[rs, 0:dh] = kc
            ka[rs, dh:2 * dh] = key_ext
            k2 = jnp.dot(kc * kc, ones, preferred_element_type=F32)
            q2 = jnp.dot(qc * qc, ones, preferred_element_type=F32)
            out += [jnp.maximum(carry[2 * idx], jnp.max(k2, axis=0, keepdims=True)),
                    jnp.maximum(carry[2 * idx + 1], jnp.max(q2, axis=0, keepdims=True))]
        return tuple(out)
    norms2 = lax.fori_loop(0, nq, widen, (jnp.zeros((1, dh), F32),) * 4)

    worst = jnp.zeros((1, 1), F32)
    for idx in range(2):
        bound = jnp.sqrt(norms2[2 * idx][:, 0:1] * norms2[2 * idx + 1][:, 0:1]) * NORM_SLACK
        worst = jnp.maximum(worst, bound)
        base = jnp.where(lane == 0, bound, 0.0)
        ext[(2 * idx) * tq:(2 * idx + 1) * tq, :] = base.astype(BF16)
        ext[(2 * idx + 1) * tq:(2 * idx + 2) * tq, :] = jnp.where(hidden, NEG_INF, base).astype(BF16)
    fast = jnp.max(worst) <= SHIFT_LIMIT

    lam = _lam_value(lamc_ref, lq1_ref, lk1_ref, lq2_ref, lk2_ref)
    gain, out_scale = g_ref[...], lamc_ref[1]

    def q_rows(qi):
        return pl.ds(pl.multiple_of(qi * tq, tq), tq)

    def k_rows(j):
        return pl.ds(pl.multiple_of(j * tk, tk), tk)

    def scores(qi, j, bufs):
        diag = jnp.where(qi == j, 1, 0)
        for idx, ((lo, _, _, _, ka), s_scr) in enumerate(zip(maps, bufs)):
            e_rows = pl.ds(pl.multiple_of((2 * idx + diag) * tq, tq), tq)
            qa = jnp.concatenate([q_ref[q_rows(qi), lo:lo + dh], ext[e_rows, :]], axis=1)
            s_scr[...] = lax.dot_general(qa, ka[k_rows(j), :], NT_DIMS,
                                         preferred_element_type=F32)

    def consume(qi, j, bufs, exact):
        v = v_ref[k_rows(j), :]
        rq = q_rows(qi)
        for (_, m_ref, l_ref, a_ref, _), s_scr in zip(maps, bufs):
            s = s_scr[...]
            l_prev, a_prev = l_ref[rq, :], a_ref[rq, :]
            if exact:
                m_ref[rq, :], l_new, a_new = _softmax_update(s, v, m_ref[rq, :], l_prev, a_prev)
            else:
                p = jnp.exp2(s)
                l_new = l_prev + jnp.sum(p, axis=1, keepdims=True)
                a_new = a_prev + jnp.dot(p.astype(BF16), v, preferred_element_type=F32)
            l_ref[rq, :], a_ref[rq, :] = l_new, a_new

    def advance(qi, j):
        last = j == qi
        return jnp.where(last, qi + 1, qi), jnp.where(last, 0, j + 1)

    buf_a, buf_b = (sa1, sa2), (sb1, sb2)

    units = [(qi, j) for qi in range(nq) for j in range(qi + 1)]

    def sweep(exact, group):
        if exact:
            for m_ref in (m1, m2):
                m_ref[...] = jnp.full(m_ref.shape, -jnp.inf, F32)
        scores(0, 0, buf_a)

        def trip(t, carry):
            cur = carry
            for g in range(group):
                nxt = advance(*cur)
                over = nxt[0] >= nq
                safe = (jnp.where(over, nq - 1, nxt[0]), jnp.where(over, 0, nxt[1]))
                mine, other = (buf_a, buf_b) if g % 2 == 0 else (buf_b, buf_a)
                scores(*safe, other)
                consume(*cur, mine, exact)
                cur = nxt
            return cur
        n_trips = n_units // group
        lax.fori_loop(0, n_trips, trip, (jnp.int32(0), jnp.int32(0)))
        rest = units[n_trips * group:]
        for g, (qi, j) in enumerate(rest):
            mine, other = (buf_a, buf_b) if g % 2 == 0 else (buf_b, buf_a)
            if g + 1 < len(rest):
                scores(*rest[g + 1], other)
            consume(qi, j, mine, exact)

    @pl.when(fast)
    def _():
        sweep(False, FAST_SWEEP_GROUP)

    @pl.when(jnp.logical_not(fast))
    def _():
        sweep(True, 2)

    def finish(qi, carry):
        rq = q_rows(qi)
        o = _diff_finish(a1[rq, :], l1[rq, :], a2[rq, :], l2[rq, :], lam, gain, out_scale)
        o_ref[rq, :] = o.astype(BF16)
        return carry
    lax.fori_loop(0, nq, finish, 0)


def _lam_inputs(lam_init, lams):
    specs = [pl.BlockSpec(memory_space=pltpu.SMEM)]
    args = [jnp.array([lam_init, 1.0 - lam_init], F32)]
    for v in lams:
        specs.append(pl.BlockSpec((1, v.shape[-1]), lambda *_: (0, 0)))
        args.append(v.reshape(1, -1))
    return specs, args


def _attn_prompt(dq, k16, v16, gain, lam_init, lams, *, batch, seq, heads, tq_pref=512):
    n, width = k16.shape
    hd = width // heads
    dh = hd // 2
    tq = _pick_tile(seq, tq_pref, V7X_LANES)
    lam_specs, lam_args = _lam_inputs(lam_init, lams)
    per_head = pl.BlockSpec((seq, hd), lambda b, h: (b, h))
    in_specs = lam_specs + [per_head, per_head, per_head,
                            pl.BlockSpec((1, hd), lambda b, h: (0, 0))]
    stat = pltpu.VMEM((seq, V7X_LANES), F32)
    accs = pltpu.VMEM((seq, hd), F32)
    sbuf = pltpu.VMEM((tq, tq), F32)
    wide_k = pltpu.VMEM((seq, hd), BF16)
    vmem = 8 * seq * hd * 2 + 2 * seq * hd * 2 + 2 * seq * hd * 4 + 4 * seq * V7X_LANES * 4 \
        + 12 * tq * tq * 4 + (4 << 20)
    return pl.pallas_call(
        functools.partial(_attn_prompt_kernel, dh=dh, tq=tq),
        grid=(batch, heads), in_specs=in_specs, out_specs=per_head,
        out_shape=jax.ShapeDtypeStruct((n, width), BF16),
        scratch_shapes=[stat, stat, accs, stat, stat, accs, sbuf, sbuf, sbuf, sbuf,
                        wide_k, wide_k, pltpu.VMEM((4 * tq, dh), BF16)],
        compiler_params=_params(vmem, 2), name="diff_attn_prompt",
    )(*lam_args, dq, k16, v16, gain.reshape(1, hd))


def _attn_decode_kernel(lamc_ref, lq1_ref, lk1_ref, lq2_ref, lk2_ref, q_ref, kn_ref, vn_ref,
                        kc_ref, vc_ref, g_ref, o_ref, qd, m_scr, l_scr, a_scr,
                        *, heads, dh, n_new):
    j = pl.program_id(1)
    hd = 2 * dh
    s_q = q_ref.shape[0]

    @pl.when(j == 0)
    def _():
        qd[...] = jnp.zeros(qd.shape, BF16)
        m_scr[...] = jnp.full(m_scr.shape, -jnp.inf, F32)
        l_scr[...] = jnp.zeros(l_scr.shape, F32)
        a_scr[...] = jnp.zeros(a_scr.shape, F32)
        new_mask = lax.broadcasted_iota(jnp.int32, (2 * s_q, kn_ref.shape[0]), 1) < n_new
        for h in range(heads):
            hs = slice(h * hd, (h + 1) * hd)
            qd[h, 0:s_q, 0:dh] = q_ref[:, h * hd:h * hd + dh]
            qd[h, s_q:2 * s_q, dh:hd] = q_ref[:, h * hd + dh:(h + 1) * hd]
            s = lax.dot_general(qd[h], kn_ref[:, hs], NT_DIMS, preferred_element_type=F32)
            s = jnp.where(new_mask, s, NEG_INF)
            _softmax_step(s, vn_ref[:, hs], m_scr.at[h], l_scr.at[h], a_scr.at[h])

    tk = kc_ref.shape[0]
    k_all = kc_ref[...].astype(BF16).reshape(tk, heads * hd)
    v_all = vc_ref[...].astype(BF16).reshape(tk, heads * hd)
    scores = [lax.dot_general(qd[h], k_all[:, h * hd:(h + 1) * hd], NT_DIMS,
                              preferred_element_type=F32) for h in range(heads)]
    for h in range(heads):
        _softmax_step(scores[h], v_all[:, h * hd:(h + 1) * hd], m_scr.at[h], l_scr.at[h],
                      a_scr.at[h])

    @pl.when(j == pl.num_programs(1) - 1)
    def _():
        lam = _lam_value(lamc_ref, lq1_ref, lk1_ref, lq2_ref, lk2_ref)
        for h in range(heads):
            a, l = a_scr[h], l_scr[h]
            o = _diff_finish(a[:s_q], l[:s_q], a[s_q:], l[s_q:], lam, g_ref[...], lamc_ref[1])
            o_ref[:, h * hd:(h + 1) * hd] = o.astype(BF16)


def _attn_decode(dq, k16, v16, cache_k, cache_v, layer, gain, lam_init, lams, *, batch, seq,
                 heads, tk_pref=512):
    n, width = k16.shape
    hd = width // heads
    dh = hd // 2
    past = cache_k.shape[2]
    tk = _pick_tile(past, tk_pref, V7X_LANES)
    pad = V7X_LANES
    kn = jnp.pad(k16.reshape(batch, seq, width), ((0, 0), (0, pad - seq), (0, 0)))
    vn = jnp.pad(v16.reshape(batch, seq, width), ((0, 0), (0, pad - seq), (0, 0)))
    lam_specs, lam_args = _lam_inputs(lam_init, lams)
    in_specs = lam_specs + [
        pl.BlockSpec((seq, width), lambda b, j: (b, 0)),
        pl.BlockSpec((None, pad, width), lambda b, j: (b, 0, 0)),
        pl.BlockSpec((None, pad, width), lambda b, j: (b, 0, 0)),
        pl.BlockSpec((None, None, tk, heads, hd), lambda b, j: (layer, b, j, 0, 0)),
        pl.BlockSpec((None, None, tk, heads, hd), lambda b, j: (layer, b, j, 0, 0)),
        pl.BlockSpec((1, hd), lambda b, j: (0, 0)),
    ]
    scratch = [pltpu.VMEM((heads, 2 * seq, hd), BF16),
               pltpu.VMEM((heads, 2 * seq, V7X_LANES), F32),
               pltpu.VMEM((heads, 2 * seq, V7X_LANES), F32),
               pltpu.VMEM((heads, 2 * seq, hd), F32)]
    vmem = 4 * tk * width * 4 + 2 * tk * width * 2 + 8 * pad * width * 2 + (8 << 20)
    return pl.pallas_call(
        functools.partial(_attn_decode_kernel, heads=heads, dh=dh, n_new=seq),
        grid=(batch, past // tk), in_specs=in_specs,
        out_specs=pl.BlockSpec((seq, width), lambda b, j: (b, 0)),
        out_shape=jax.ShapeDtypeStruct((n, width), BF16),
        scratch_shapes=scratch, compiler_params=_params(vmem, 2), name="diff_attn_decode",
    )(*lam_args, dq, kn, vn, cache_k, cache_v, gain.reshape(1, hd))


def _ret_tables(heads, chunk, dk, dv):
    log_gamma = jnp.log1p(-jnp.exp2(-5.0 - jnp.arange(heads, dtype=F32)))
    idx = jnp.arange(chunk, dtype=F32)
    diff = idx[:, None] - idx[None, :]
    decay = jnp.where(diff >= 0, jnp.exp(log_gamma[:, None, None] * jnp.maximum(diff, 0.0)), 0.0)
    cross = jnp.exp(log_gamma[:, None] * (idx + 1.0)[None, :])
    kdec = jnp.exp(log_gamma[:, None] * (chunk - 1.0 - idx)[None, :])
    carry = jnp.exp(log_gamma * chunk)
    return (decay,
            jnp.broadcast_to(cross[:, :, None], (heads, chunk, dv)),
            jnp.broadcast_to(kdec[:, :, None], (heads, chunk, dk)),
            carry)


def _ret_kernel(*refs, has_init):
    refs = list(refs)
    carry_ref, q_ref, k_ref, v_ref, rg_ref, dec_ref, cross_ref, kdec_ref, g_ref = refs[:9]
    refs = refs[9:]
    s0_ref = refs.pop(0) if has_init else None
    o_ref, sout_ref, s_scr = refs
    heads, dk, dv = s_scr.shape

    @pl.when(pl.program_id(1) == 0)
    def _():
        s_scr[...] = s0_ref[...] if has_init else jnp.zeros(s_scr.shape, F32)

    for h in range(heads):
        ks, vs = slice(h * dk, (h + 1) * dk), slice(h * dv, (h + 1) * dv)
        q, k, v = q_ref[:, ks], k_ref[:, ks], v_ref[:, vs]
        state = s_scr[h]
        scores = lax.dot_general(q, k, NT_DIMS, preferred_element_type=F32) * dec_ref[h]
        inner = jnp.dot(scores.astype(BF16), v, preferred_element_type=F32)
        cross = jnp.dot(q, state.astype(BF16), preferred_element_type=F32) * cross_ref[h]
        k_dec = (k.astype(F32) * kdec_ref[h]).astype(BF16)
        new_state = carry_ref[h] * state + lax.dot_general(k_dec, v, TN_DIMS,
                                                           preferred_element_type=F32)
        s_scr[h] = new_state
        sout_ref[h] = new_state
        gate = rg_ref[:, vs].astype(F32)
        y = _rms_rows(inner + cross, g_ref[...]) * (gate * jax.nn.sigmoid(gate))
        o_ref[:, vs] = y.astype(BF16)


def _cast_slices(weights, layer, n_steps, step_of):
    in_specs, out_specs, out_shapes, vmem = [], [], [], 0
    for w in weights:
        rows, cols = w.shape[1:]
        if rows % (16 * n_steps):
            return None
        r = rows // n_steps
        in_specs.append(pl.BlockSpec((None, r, cols), lambda *g: (layer, step_of(*g), 0)))
        out_specs.append(pl.BlockSpec((r, cols), lambda *g: (step_of(*g), 0)))
        out_shapes.append(jax.ShapeDtypeStruct((rows, cols), BF16))
        vmem += 2 * r * cols * (4 + 2)
    return in_specs, out_specs, out_shapes, vmem


def _retention(rqk, main, c_v, c_gate, dv, gain, state0, layer, *, batch, seq, heads,
               chunk_pref=256):
    n = rqk.shape[0]
    dk = rqk.shape[1] // (2 * heads)
    assert c_v % (heads * dv) == 0 and c_gate % (heads * dv) == 0
    vb, gb = c_v // (heads * dv), c_gate // (heads * dv)
    chunk = _pick_tile(seq, chunk_pref, 16)
    nc = seq // chunk
    decay, cross, kdec, carry = _ret_tables(heads, chunk, dk, dv)
    has_init = state0 is not None
    row = lambda b, c: b * nc + c
    whole = lambda shape: pl.BlockSpec(shape, lambda b, c: (0,) * len(shape))
    in_specs = [
        pl.BlockSpec(memory_space=pltpu.SMEM),
        pl.BlockSpec((chunk, heads * dk), lambda b, c: (row(b, c), 0)),
        pl.BlockSpec((chunk, heads * dk), lambda b, c: (row(b, c), 1)),
        pl.BlockSpec((chunk, heads * dv), lambda b, c: (row(b, c), vb)),
        pl.BlockSpec((chunk, heads * dv), lambda b, c: (row(b, c), gb)),
        whole((heads, chunk, chunk)), whole((heads, chunk, dv)), whole((heads, chunk, dk)),
        whole((1, dv)),
    ]
    args = [carry, rqk, rqk, main, main, decay, cross, kdec, gain.reshape(1, dv)]
    if has_init:
        in_specs.append(pl.BlockSpec((None, None, heads, dk, dv),
                                     lambda b, c: (layer, b, 0, 0, 0)))
        args.append(state0)
    vmem = (2 * heads * chunk * (chunk + dv + dk) * 4 + 8 * chunk * heads * (dk + dv) * 2
            + 6 * heads * dk * dv * 4 + (8 << 20))
    ob, st = pl.pallas_call(
        functools.partial(_ret_kernel, has_init=has_init),
        grid=(batch, nc), in_specs=in_specs,
        out_specs=[pl.BlockSpec((chunk, heads * dv), lambda b, c: (row(b, c), 0)),
                   pl.BlockSpec((None, heads, dk, dv), lambda b, c: (b, 0, 0, 0))],
        out_shape=[jax.ShapeDtypeStruct((n, heads * dv), BF16),
                   jax.ShapeDtypeStruct((batch, heads, dk, dv), F32)],
        scratch_shapes=[pltpu.VMEM((heads, dk, dv), F32)],
        compiler_params=_params(vmem, 2), name="retention",
    )(*args)
    return ob, st


def _merge_kernel(oa_ref, ob_ref, wa_ref, wb_ref, ga_ref, gb_ref, *refs, w_scale):
    if w_scale is not None:
        wsrc_ref, o_ref, wdst_ref = refs
        col = lax.broadcasted_iota(jnp.int32, (1, wsrc_ref.shape[1]), 1)
        wdst_ref[...] = (wsrc_ref[...] * jnp.where(col < w_scale[0], w_scale[1], 1.0)).astype(BF16)
    else:
        (o_ref,) = refs
    a = jnp.dot(oa_ref[...], wa_ref[...], preferred_element_type=F32)
    b = jnp.dot(ob_ref[...], wb_ref[...], preferred_element_type=F32)
    ga = jax.nn.sigmoid(ga_ref[...].astype(F32))
    gb = jax.nn.sigmoid(gb_ref[...].astype(F32))
    o_ref[...] = (ga * a + gb * b).astype(BF16)


def _merge(oa, ob, main, c_ga, c_gb, w_a, w_b, layer, *, tm_pref=1024, tn_pref=512,
           cast_w_in=None):
    n, k = oa.shape
    d = w_a.shape[2]
    tm = _pick_tile(n, tm_pref, 16)
    tn = _pick_tile(d, tn_pref, V7X_LANES)
    assert c_ga % tn == 0 and c_gb % tn == 0
    ga0, gb0 = c_ga // tn, c_gb // tn
    n_j = d // tn
    in_specs = [pl.BlockSpec((tm, k), lambda i, j: (i, 0)),
                pl.BlockSpec((tm, k), lambda i, j: (i, 0)),
                pl.BlockSpec((None, k, tn), lambda i, j: (layer, 0, j)),
                pl.BlockSpec((None, k, tn), lambda i, j: (layer, 0, j)),
                pl.BlockSpec((tm, tn), lambda i, j: (i, ga0 + j)),
                pl.BlockSpec((tm, tn), lambda i, j: (i, gb0 + j))]
    args = [oa, ob, w_a, w_b, main, main]
    out_specs = [pl.BlockSpec((tm, tn), lambda i, j: (i, j))]
    out_shape = [jax.ShapeDtypeStruct((n, d), BF16)]
    vmem = 4 * tm * k * 2 + 4 * k * tn * 2 + 6 * tm * tn * 2 + 4 * tm * tn * 4 + (4 << 20)
    slices = None
    if cast_w_in is not None:
        slices = _cast_slices(cast_w_in[:1], cast_w_in[1], (n // tm) * n_j,
                              lambda i, j: i * n_j + j)
    if slices:
        in_specs += slices[0]
        args.append(cast_w_in[0])
        out_specs += slices[1]
        out_shape += slices[2]
        vmem += slices[3]
    outs = pl.pallas_call(
        functools.partial(_merge_kernel, w_scale=cast_w_in[2:] if slices else None),
        grid=(n // tm, n_j), in_specs=in_specs, out_specs=out_specs, out_shape=out_shape,
        compiler_params=_params(vmem, 2), name="branch_merge",
    )(*args)
    if cast_w_in is None:
        return outs[0], None
    if slices:
        return outs[0], outs[1]
    w, w_layer, q_cols, q_scale = cast_w_in
    return outs[0], _cast_w_in(w[w_layer:w_layer + 1], q_cols, q_scale)[0]


def _ffn_kernel(*refs, row_chunk, final_norm):
    refs = list(refs)
    x_ref, g_ref, wu_ref, wd_ref = refs[:4]
    fg_ref = refs[4] if final_norm else None
    o_ref, h_scr = refs[-2:]
    f = pl.program_id(1)
    n_chunks = x_ref.shape[0] // row_chunk

    @pl.when(f == 0)
    def _():
        def rows(r, carry):
            rs = pl.ds(pl.multiple_of(r * row_chunk, row_chunk), row_chunk)
            x = x_ref[rs, :]
            h_scr[rs, :] = _rms_rows(x, g_ref[...]).astype(BF16)
            o_ref[rs, :] = x
            return carry
        lax.fori_loop(0, n_chunks, rows, 0, unroll=2)

    u = jnp.dot(h_scr[...], wu_ref[...], preferred_element_type=F32)
    u = jnp.square(jnp.maximum(u, 0.0)).astype(BF16)
    o_ref[...] += jnp.dot(u, wd_ref[...], preferred_element_type=F32)

    if final_norm:
        @pl.when(f == pl.num_programs(1) - 1)
        def _():
            def rows(r, carry):
                rs = pl.ds(pl.multiple_of(r * row_chunk, row_chunk), row_chunk)
                o_ref[rs, :] = _rms_rows(o_ref[rs, :], fg_ref[...])
                return carry
            lax.fori_loop(0, n_chunks, rows, 0)


def _ffn(x, gain, w_up, w_down, layer, final_gain=None, *, tm_pref=1024, tf_pref=512):
    n, d = x.shape
    d_ff = w_up.shape[2]
    tm = _pick_tile(n, tm_pref, 16)
    tf = _pick_tile(d_ff, tf_pref, V7X_LANES)
    final_norm = final_gain is not None
    in_specs = [pl.BlockSpec((tm, d), lambda i, f: (i, 0)),
                pl.BlockSpec((1, d), lambda i, f: (0, 0)),
                pl.BlockSpec((None, d, tf), lambda i, f: (layer, 0, f)),
                pl.BlockSpec((None, tf, d), lambda i, f: (layer, f, 0))]
    args = [x, gain.reshape(1, d), w_up, w_down]
    if final_norm:
        in_specs.append(pl.BlockSpec((1, d), lambda i, f: (0, 0)))
        args.append(final_gain.reshape(1, d))
    vmem = 4 * tm * d * 4 + tm * d * 2 + 4 * d * tf * 2 + 2 * tm * tf * 4 + (6 << 20)
    return pl.pallas_call(
        functools.partial(_ffn_kernel, row_chunk=min(tm, 64), final_norm=final_norm),
        grid=(n // tm, d_ff // tf), in_specs=in_specs,
        out_specs=pl.BlockSpec((tm, d), lambda i, f: (i, 0)),
        out_shape=jax.ShapeDtypeStruct((n, d), F32),
        scratch_shapes=[pltpu.VMEM((tm, d), BF16)],
        compiler_params=_params(vmem, 2), name="ffn",
    )(*args)


def _rope_tables(pos, dk, batch):
    inv = 1.0 / (ROPE_BASE ** jnp.linspace(0.0, 1.0, dk // 2, dtype=F32))
    th = pos.astype(F32)[:, None] * jnp.repeat(inv, 2)[None, :]
    sin, cos = jnp.sin(th), jnp.cos(th)
    even = (jnp.arange(dk) % 2) == 0
    sin_next = jnp.where(even, -sin, 0.0)
    sin_prev = jnp.where(even, 0.0, sin)
    return tuple(jnp.tile(t, (batch, 1)) for t in (cos, sin_next, sin_prev))


def _cast_w_in(w_in, diff_w, q_scale):
    col_scale = jnp.where(jnp.arange(w_in.shape[-1]) < diff_w, q_scale, 1.0).astype(F32)
    return (w_in * col_scale).astype(BF16)


def _mixer(x, layer, w, w_in, p, rope, past, kv_stack, side, *, depth, batch, seq, heads,
           ret_heads):
    d = x.shape[1]
    hd = p["subln_g"].shape[-1]
    diff_w = heads * hd
    dk = p["state"].shape[-2]
    dv = p["state"].shape[-1]
    qk_w = ret_heads * dk
    ret_w = ret_heads * dv
    c_k, c_v, c_rqk = diff_w, 2 * diff_w, 3 * diff_w
    skipped = 2 * diff_w + 2 * qk_w
    c_rv, c_rg, c_ga, c_gb = diff_w, diff_w + ret_w, diff_w + 2 * ret_w, diff_w + 2 * ret_w + d
    main_w = c_gb + d

    main, h = _proj(x, *w_in, 0, main_w, gain=p["norm_mix_g"], emit_h=True,
                    gap=(diff_w, skipped))
    k32, k16, w_up_l = _proj_heads(h, *w_in, c_k, heads, kv_stack[0], depth, layer,
                                   cast=None if past else side["up"])
    v32, v16, w_down_l = _proj_heads(h, *w_in, c_v, heads, kv_stack[1], depth, layer,
                                     cast=None if past else side["down"])
    rqk = _proj(h, *w_in, c_rqk, 2 * qk_w, epi="rope", extra=rope,
                rope_scale=dk ** -0.5, tn_pref=qk_w)

    lam_init = 0.8 - 0.6 * math.exp(-0.3 * layer)
    lams = (p["lq1"], p["lk1"], p["lq2"], p["lk2"])
    cast_w_in, ffn_w = None, (w_up_l, w_down_l)
    if past:
        oa = _attn_decode(main, k16, v16, p["cache_k"], p["cache_v"], layer, p["subln_g"],
                          lam_init, lams, batch=batch, seq=seq, heads=heads)
        state0 = p["state"]
    else:
        if layer + 1 < depth:
            cast_w_in = (side["in"], layer + 1, diff_w, side["q_scale"])
        oa = _attn_prompt(main, k16, v16, p["subln_g"], lam_init, lams, batch=batch,
                          seq=seq, heads=heads)
        state0 = None
    ob, st = _retention(rqk, main, c_rv, c_rg, dv, p["ret_g"], state0, layer,
                        batch=batch, seq=seq, heads=ret_heads)
    merged, w_in_next = _merge(oa, ob, main, c_ga, c_gb, w["a"], w["b"], layer,
                               cast_w_in=cast_w_in)
    x = _proj(merged, w["out"], layer, 0, d, epi="residual", extra=(x,), tm_pref=512, tn_pref=d)
    return x, (k32, v32), st, w_in_next, ffn_w


def kernel(x_prompt, x_sample, cache_diff_k, cache_diff_v, state_ret, norm_mix_g, w_in,
           lambda_q1, lambda_k1, lambda_q2, lambda_k2, diff_subln_g, ret_norm_g,
           w_branch_a, w_branch_b, w_out, norm_ffn_g, w_up, w_down, final_norm_g):
    depth, dec_batch, past_len, heads, hd = cache_diff_k.shape
    batch, seq, d = x_prompt.shape
    dec_seq = x_sample.shape[1]
    ret_heads, dk, dv = state_ret.shape[2:]

    q_scale = (hd // 2) ** -0.5 * LOG2E
    w = {"a": w_branch_a.astype(BF16), "b": w_branch_b.astype(BF16), "out": w_out.astype(BF16)}
    side = {"in": w_in, "up": w_up, "down": w_down, "q_scale": q_scale}
    w_in_cur = _cast_w_in(w_in[0:1], heads * hd, q_scale)
    cache_k, cache_v = cache_diff_k, cache_diff_v
    rope_p = _rope_tables(jnp.arange(seq), dk, batch)
    rope_s = _rope_tables(past_len + jnp.arange(dec_seq), dk, dec_batch)

    xp = x_prompt.reshape(batch * seq, d)
    xs = x_sample.reshape(dec_batch * dec_seq, d)
    kv_p, kv_s = (None, None), (None, None)
    states_p, states_s = [], []
    for l in range(depth):
        p = {"norm_mix_g": norm_mix_g[l], "lq1": lambda_q1[l], "lk1": lambda_k1[l],
             "lq2": lambda_q2[l], "lk2": lambda_k2[l], "subln_g": diff_subln_g[l],
             "ret_g": ret_norm_g[l], "cache_k": cache_k, "cache_v": cache_v, "state": state_ret}
        final_g = final_norm_g if l == depth - 1 else None
        xp, kv_p, sp, w_in_next, (wu, wd) = _mixer(
            xp, l, w, (w_in_cur, 0), p, rope_p, False, kv_p, side, depth=depth, batch=batch,
            seq=seq, heads=heads, ret_heads=ret_heads)
        xs, kv_s, ss, _, _ = _mixer(
            xs, l, w, (w_in_cur, 0), p, rope_s, True, kv_s, None, depth=depth, batch=dec_batch,
            seq=dec_seq, heads=heads, ret_heads=ret_heads)
        xp = _ffn(xp, norm_ffn_g[l], wu[None], wd[None], 0, final_g)
        xs = _ffn(xs, norm_ffn_g[l], wu[None], wd[None], 0, final_g)
        if w_in_next is not None:
            w_in_cur = w_in_next[None]
        states_p.append(sp)
        states_s.append(ss)
    return (xp.reshape(batch, seq, d), xs.reshape(dec_batch, dec_seq, d),
            kv_p[0].reshape(depth, batch, seq, heads, hd),
            kv_p[1].reshape(depth, batch, seq, heads, hd), jnp.stack(states_p),
            kv_s[0].reshape(depth, dec_batch, dec_seq, heads, hd),
            kv_s[1].reshape(depth, dec_batch, dec_seq, heads, hd), jnp.stack(states_s))
```

```python
import functools
import math

import jax
import jax.numpy as jnp
from jax import lax
from jax.experimental import pallas as pl
from jax.experimental.pallas import tpu as pltpu

F32 = jnp.float32
BF16 = jnp.bfloat16

NORM_EPS = 1e-6
ROPE_BASE = 10000.0
MASK_CHUNK = 64
NEG_INF = -1e30
SHIFT_LIMIT = 48.0
NORM_SLACK = 1.02
FAST_SWEEP_GROUP = 12
LOG2E = 1.4426950408889634

V7X_VMEM_BYTES = 64 * 1024 * 1024
V7X_LANES = 128
VMEM_REQUEST_CAP = V7X_VMEM_BYTES - 8 * 1024 * 1024

NT_DIMS = (((1,), (1,)), ((), ()))
TN_DIMS = (((0,), (0,)), ((), ()))


def _pick_tile(n, pref, align):
    if n <= pref:
        return n
    t = pref - pref % align
    while t >= align:
        if n % t == 0:
            return t
        t -= align
    raise ValueError(f"no tile for {n} (pref {pref}, align {align})")


def _params(vmem_bytes, n_grid):
    return pltpu.CompilerParams(
        dimension_semantics=("arbitrary",) * n_grid,
        vmem_limit_bytes=int(min(VMEM_REQUEST_CAP, vmem_bytes)))


def _rms_rows(x, gain):
    ms = jnp.mean(x * x, axis=-1, keepdims=True)
    return x * lax.rsqrt(ms + NORM_EPS) * gain


def _proj_kernel(*refs, has_norm, emit_h, epi, row_chunk, rope_scale):
    refs = list(refs)
    x_ref = refs.pop(0)
    g_ref = refs.pop(0) if has_norm else None
    w_ref = refs.pop(0)
    if epi == "rope":
        cos_ref, sina_ref, sinb_ref = refs.pop(0), refs.pop(0), refs.pop(0)
    if epi == "residual":
        res_ref = refs.pop(0)
    o_ref = refs.pop(0)
    hout_ref = refs.pop(0) if emit_h else None
    h_scr = refs.pop(0) if has_norm else None

    j = pl.program_id(1)
    if has_norm:
        @pl.when(j == 0)
        def _():
            def rows(r, carry):
                rs = pl.ds(pl.multiple_of(r * row_chunk, row_chunk), row_chunk)
                hrow = _rms_rows(x_ref[rs, :], g_ref[...]).astype(BF16)
                h_scr[rs, :] = hrow
                if emit_h:
                    hout_ref[rs, :] = hrow
                return carry
            lax.fori_loop(0, x_ref.shape[0] // row_chunk, rows, 0, unroll=2)
        lhs = h_scr[...]
    else:
        lhs = x_ref[...]

    acc = jnp.dot(lhs, w_ref[...], preferred_element_type=F32)
    if epi == "bf16":
        o_ref[...] = acc.astype(BF16)
    elif epi == "residual":
        o_ref[...] = res_ref[...] + acc
    elif epi == "rope":
        n_tiles = pl.num_programs(1)
        scale = jnp.where(j < n_tiles // 2, 1.0, rope_scale).astype(F32)
        cos, sina, sinb = cos_ref[...], sina_ref[...], sinb_ref[...]
        for c in range(acc.shape[1] // V7X_LANES):
            cs = slice(c * V7X_LANES, (c + 1) * V7X_LANES)
            xs = acc[:, cs]
            nxt = pltpu.roll(xs, V7X_LANES - 1, 1)
            prv = pltpu.roll(xs, 1, 1)
            o_ref[:, cs] = ((xs * cos + nxt * sina + prv * sinb) * scale).astype(BF16)
    else:
        raise ValueError(epi)


def _proj(lhs, w, layer, col0, ncols, *, gain=None, emit_h=False, epi="bf16", extra=(),
          rope_scale=1.0, gap=(0, 0), tm_pref=1024, tn_pref=1024):
    n, k = lhs.shape
    has_norm = gain is not None
    tm = _pick_tile(n, tm_pref, 16)
    tn = _pick_tile(ncols, tn_pref, V7X_LANES)
    assert col0 % tn == 0 and gap[0] % tn == 0 and gap[1] % tn == 0
    cb = col0 // tn
    lead_tiles, skip_tiles = gap[0] // tn, gap[1] // tn
    grid = (n // tm, ncols // tn)

    def w_tile(j):
        return cb + j + jnp.where(j >= lead_tiles, skip_tiles, 0) if skip_tiles else cb + j

    in_specs = [pl.BlockSpec((tm, k), lambda i, j: (i, 0))]
    args = [lhs]
    if has_norm:
        in_specs.append(pl.BlockSpec((1, k), lambda i, j: (0, 0)))
        args.append(gain.reshape(1, k))
    in_specs.append(pl.BlockSpec((None, k, tn), lambda i, j: (layer, 0, w_tile(j))))
    args.append(w)
    if epi == "rope":
        for t in extra:
            in_specs.append(pl.BlockSpec((tm, V7X_LANES), lambda i, j: (i, 0)))
            args.append(t)
    elif epi == "residual":
        in_specs.append(pl.BlockSpec((tm, tn), lambda i, j: (i, j)))
        args.append(extra[0])

    tile_spec = pl.BlockSpec((tm, tn), lambda i, j: (i, j))
    out_dtype = F32 if epi == "residual" else BF16
    out_shape = [jax.ShapeDtypeStruct((n, ncols), out_dtype)]
    out_specs = [tile_spec]
    out_bytes = tm * tn * jnp.dtype(out_dtype).itemsize
    if emit_h:
        out_shape.append(jax.ShapeDtypeStruct((n, k), BF16))
        out_specs.append(pl.BlockSpec((tm, k), lambda i, j: (i, 0)))
        out_bytes += tm * k * 2
    scratch = [pltpu.VMEM((tm, k), BF16)] if has_norm else []

    vmem = (2 * tm * k * lhs.dtype.itemsize + 2 * k * tn * 2 + 2 * out_bytes
            + (tm * k * 2 if has_norm else 0) + 3 * tm * tn * 4
            + (2 * tm * tn * 4 if epi == "residual" else 0) + (4 << 20))
    outs = pl.pallas_call(
        functools.partial(_proj_kernel, has_norm=has_norm, emit_h=emit_h, epi=epi,
                          row_chunk=min(tm, 64), rope_scale=rope_scale),
        grid=grid, in_specs=in_specs, out_specs=out_specs, out_shape=out_shape,
        scratch_shapes=scratch, compiler_params=_params(vmem, 2),
        name=f"proj_{epi}")(*args)
    return outs[0] if len(outs) == 1 else tuple(outs)


def _proj_heads_kernel(*refs, aliased, has_cast):
    x_ref, w_ref = refs[0], refs[1]
    if has_cast:
        src_ref, o32_ref, o16_ref, dst_ref = refs[-4:]
        dst_ref[...] = src_ref[...].astype(BF16)
    else:
        o32_ref, o16_ref = refs[-2], refs[-1]
    acc = jnp.dot(x_ref[...], w_ref[...], preferred_element_type=F32)
    o16_ref[...] = acc.astype(BF16)
    o32_ref[...] = acc.reshape(o32_ref.shape)


def _proj_heads(lhs, w, w_layer, col0, heads, stacked, depth, layer, *, tm_pref=512, cast=None):
    n, k = lhs.shape
    ncols = k
    hd = ncols // heads
    tm = _pick_tile(n, tm_pref, 16)
    cb = col0 // ncols
    assert col0 % ncols == 0
    aliased = stacked is not None
    in_specs = [pl.BlockSpec((tm, k), lambda i: (i, 0)),
                pl.BlockSpec((None, k, ncols), lambda i: (w_layer, 0, cb))]
    args = [lhs, w]
    if aliased:
        in_specs.append(pl.BlockSpec(memory_space=pl.ANY))
        args.append(stacked)
    vmem = 4 * tm * k * 2 + 4 * k * ncols * 2 + 4 * tm * ncols * 4 + 4 * tm * ncols * 2 + (4 << 20)
    out_specs = [pl.BlockSpec((None, tm, heads, hd), lambda i: (layer, i, 0, 0)),
                 pl.BlockSpec((tm, ncols), lambda i: (i, 0))]
    out_shape = [jax.ShapeDtypeStruct((depth, n, heads, hd), F32),
                 jax.ShapeDtypeStruct((n, ncols), BF16)]
    slices = _cast_slices((cast,), layer, n // tm, lambda i: i) if cast is not None else None
    if slices:
        in_specs += slices[0]
        args.append(cast)
        out_specs += slices[1]
        out_shape += slices[2]
        vmem += slices[3]
    outs = pl.pallas_call(
        functools.partial(_proj_heads_kernel, aliased=aliased, has_cast=bool(slices)),
        grid=(n // tm,), in_specs=in_specs, out_specs=out_specs, out_shape=out_shape,
        input_output_aliases={2: 0} if aliased else {},
        compiler_params=_params(vmem, 1), name="proj_heads")(*args)
    if cast is None:
        return outs[0], outs[1], None
    return outs[0], outs[1], outs[2] if slices else cast[layer].astype(BF16)


def _lam_value(lamc_ref, lq1_ref, lk1_ref, lq2_ref, lk2_ref):
    a = jnp.exp(jnp.sum(lq1_ref[...] * lk1_ref[...], axis=-1, keepdims=True))
    b = jnp.exp(jnp.sum(lq2_ref[...] * lk2_ref[...], axis=-1, keepdims=True))
    return a - b + lamc_ref[0]


def _lane_repeat(x, reps):
    return x if reps == 1 else jnp.concatenate([x] * reps, axis=1)


def _softmax_update(s, v, m_prev, l_prev, acc_prev):
    keys = s.shape[1]
    m_new = jnp.maximum(m_prev, jnp.max(s, axis=1, keepdims=True))
    alpha = jnp.exp2(m_prev - m_new)
    p = jnp.exp2(s - _lane_repeat(m_new, keys // V7X_LANES))
    l_new = alpha * l_prev + jnp.sum(p, axis=1, keepdims=True)
    pv = jnp.dot(p.astype(BF16), v, preferred_element_type=F32)
    acc_new = acc_prev * _lane_repeat(alpha, acc_prev.shape[1] // V7X_LANES) + pv
    return m_new, l_new, acc_new


def _softmax_step(s, v, m_ref, l_ref, acc_ref):
    m_ref[...], l_ref[...], acc_ref[...] = _softmax_update(s, v, m_ref[...], l_ref[...],
                                                           acc_ref[...])


def _diff_finish(a1, l1, a2, l2, lam, gain, out_scale):
    rep = a1.shape[1] // V7X_LANES
    o = a1 * _lane_repeat(1.0 / l1, rep) - a2 * _lane_repeat(lam / l2, rep)
    return _rms_rows(o, gain * out_scale)


def _attn_prompt_kernel(lamc_ref, lq1_ref, lk1_ref, lq2_ref, lk2_ref, q_ref, k_ref, v_ref, g_ref,
                        o_ref, m1, l1, a1, m2, l2, a2, sa1, sa2, sb1, sb2, ka1, ka2, ext, *, dh, tq):
    seq = q_ref.shape[0]
    tk = tq
    nq = seq // tq
    n_units = nq * (nq + 1) // 2
    shift = MASK_CHUNK.bit_length() - 1
    assert tq // MASK_CHUNK < dh
    maps = ((0, m1, l1, a1, ka1), (dh, m2, l2, a2, ka2))

    lane = lax.broadcasted_iota(jnp.int32, (tk, dh), 1)
    row_chunk = lax.shift_right_logical(lax.broadcasted_iota(jnp.int32, (tk, dh), 0), shift)
    key_ext = jnp.where(lane == 0, -1.0, jnp.where(lane - 1 == row_chunk, 1.0, 0.0)).astype(BF16)
    hidden = (lane >= 1) & (lane <= tq // MASK_CHUNK) & (lane - 1 > row_chunk)

    ones = jnp.ones((dh, dh), BF16)

    def widen(c, carry):
        rs = pl.ds(pl.multiple_of(c * tk, tk), tk)
        out = []
        for idx, (lo, _, l_ref, a_ref, ka) in enumerate(maps):
            l_ref[rs, :] = jnp.zeros((tk, V7X_LANES), F32)
            a_ref[rs, :] = jnp.zeros((tk, 2 * dh), F32)
            kc = k_ref[rs, lo:lo + dh]
            qc = q_ref[rs, lo:lo + dh]
            ka[rs, 0:dh] = kc
            ka[rs, dh:2 * dh] = key_ext
            k2 = jnp.dot(kc * kc, ones, preferred_element_type=F32)
            q2 = jnp.dot(qc * qc, ones, preferred_element_type=F32)
            out += [jnp.maximum(carry[2 * idx], jnp.max(k2, axis=0, keepdims=True)),
                    jnp.maximum(carry[2 * idx + 1], jnp.max(q2, axis=0, keepdims=True))]
        return tuple(out)
    norms2 = lax.fori_loop(0, nq, widen, (jnp.zeros((1, dh), F32),) * 4)

    worst = jnp.zeros((1, 1), F32)
    for idx in range(2):
        bound = jnp.sqrt(norms2[2 * idx][:, 0:1] * norms2[2 * idx + 1][:, 0:1]) * NORM_SLACK
        worst = jnp.maximum(worst, bound)
        base = jnp.where(lane == 0, bound, 0.0)
        ext[(2 * idx) * tq:(2 * idx + 1) * tq, :] = base.astype(BF16)
        ext[(2 * idx + 1) * tq:(2 * idx + 2) * tq, :] = jnp.where(hidden, NEG_INF, base).astype(BF16)
    fast = jnp.max(worst) <= SHIFT_LIMIT

    lam = _lam_value(lamc_ref, lq1_ref, lk1_ref, lq2_ref, lk2_ref)
    gain, out_scale = g_ref[...], lamc_ref[1]

    def q_rows(qi):
        return pl.ds(pl.multiple_of(qi * tq, tq), tq)

    def k_rows(j):
        return pl.ds(pl.multiple_of(j * tk, tk), tk)

    def scores(qi, j, bufs):
        diag = jnp.where(qi == j, 1, 0)
        for idx, ((lo, _, _, _, ka), s_scr) in enumerate(zip(maps, bufs)):
            e_rows = pl.ds(pl.multiple_of((2 * idx + diag) * tq, tq), tq)
            qa = jnp.concatenate([q_ref[q_rows(qi), lo:lo + dh], ext[e_rows, :]], axis=1)
            s_scr[...] = lax.dot_general(qa, ka[k_rows(j), :], NT_DIMS,
                                         preferred_element_type=F32)

    def consume(qi, j, bufs, exact):
        v = v_ref[k_rows(j), :]
        rq = q_rows(qi)
        for (_, m_ref, l_ref, a_ref, _), s_scr in zip(maps, bufs):
            s = s_scr[...]
            l_prev, a_prev = l_ref[rq, :], a_ref[rq, :]
            if exact:
                m_ref[rq, :], l_new, a_new = _softmax_update(s, v, m_ref[rq, :], l_prev, a_prev)
            else:
                p = jnp.exp2(s)
                l_new = l_prev + jnp.sum(p, axis=1, keepdims=True)
                a_new = a_prev + jnp.dot(p.astype(BF16), v, preferred_element_type=F32)
            l_ref[rq, :], a_ref[rq, :] = l_new, a_new

    def advance(qi, j):
        last = j == qi
        return jnp.where(last, qi + 1, qi), jnp.where(last, 0, j + 1)

    buf_a, buf_b = (sa1, sa2), (sb1, sb2)

    units = [(qi, j) for qi in range(nq) for j in range(qi + 1)]

    def sweep(exact, group):
        if exact:
            for m_ref in (m1, m2):
                m_ref[...] = jnp.full(m_ref.shape, -jnp.inf, F32)
        scores(0, 0, buf_a)

        def trip(t, carry):
            cur = carry
            for g in range(group):
                nxt = advance(*cur)
                over = nxt[0] >= nq
                safe = (jnp.where(over, nq - 1, nxt[0]), jnp.where(over, 0, nxt[1]))
                mine, other = (buf_a, buf_b) if g % 2 == 0 else (buf_b, buf_a)
                scores(*safe, other)
                consume(*cur, mine, exact)
                cur = nxt
            return cur
        n_trips = n_units // group
        lax.fori_loop(0, n_trips, trip, (jnp.int32(0), jnp.int32(0)))
        rest = units[n_trips * group:]
        for g, (qi, j) in enumerate(rest):
            mine, other = (buf_a, buf_b) if g % 2 == 0 else (buf_b, buf_a)
            if g + 1 < len(rest):
                scores(*rest[g + 1], other)
            consume(qi, j, mine, exact)

    @pl.when(fast)
    def _():
        sweep(False, FAST_SWEEP_GROUP)

    @pl.when(jnp.logical_not(fast))
    def _():
        sweep(True, 2)

    def finish(qi, carry):
        rq = q_rows(qi)
        o = _diff_finish(a1[rq, :], l1[rq, :], a2[rq, :], l2[rq, :], lam, gain, out_scale)
        o_ref[rq, :] = o.astype(BF16)
        return carry
    lax.fori_loop(0, nq, finish, 0)


def _lam_inputs(lam_init, lams):
    specs = [pl.BlockSpec(memory_space=pltpu.SMEM)]
    args = [jnp.array([lam_init, 1.0 - lam_init], F32)]
    for v in lams:
        specs.append(pl.BlockSpec((1, v.shape[-1]), lambda *_: (0, 0)))
        args.append(v.reshape(1, -1))
    return specs, args


def _attn_prompt(dq, k16, v16, gain, lam_init, lams, *, batch, seq, heads, tq_pref=512):
    n, width = k16.shape
    hd = width // heads
    dh = hd // 2
    tq = _pick_tile(seq, tq_pref, V7X_LANES)
    lam_specs, lam_args = _lam_inputs(lam_init, lams)
    per_head = pl.BlockSpec((seq, hd), lambda b, h: (b, h))
    in_specs = lam_specs + [per_head, per_head, per_head,
                            pl.BlockSpec((1, hd), lambda b, h: (0, 0))]
    stat = pltpu.VMEM((seq, V7X_LANES), F32)
    accs = pltpu.VMEM((seq, hd), F32)
    sbuf = pltpu.VMEM((tq, tq), F32)
    wide_k = pltpu.VMEM((seq, hd), BF16)
    vmem = 8 * seq * hd * 2 + 2 * seq * hd * 2 + 2 * seq * hd * 4 + 4 * seq * V7X_LANES * 4 \
        + 12 * tq * tq * 4 + (4 << 20)
    return pl.pallas_call(
        functools.partial(_attn_prompt_kernel, dh=dh, tq=tq),
        grid=(batch, heads), in_specs=in_specs, out_specs=per_head,
        out_shape=jax.ShapeDtypeStruct((n, width), BF16),
        scratch_shapes=[stat, stat, accs, stat, stat, accs, sbuf, sbuf, sbuf, sbuf,
                        wide_k, wide_k, pltpu.VMEM((4 * tq, dh), BF16)],
        compiler_params=_params(vmem, 2), name="diff_attn_prompt",
    )(*lam_args, dq, k16, v16, gain.reshape(1, hd))


def _attn_decode_kernel(lamc_ref, lq1_ref, lk1_ref, lq2_ref, lk2_ref, q_ref, kn_ref, vn_ref,
                        kc_ref, vc_ref, g_ref, o_ref, qd, m_scr, l_scr, a_scr,
                        *, heads, dh, n_new):
    j = pl.program_id(1)
    hd = 2 * dh
    s_q = q_ref.shape[0]

    @pl.when(j == 0)
    def _():
        qd[...] = jnp.zeros(qd.shape, BF16)
        m_scr[...] = jnp.full(m_scr.shape, -jnp.inf, F32)
        l_scr[...] = jnp.zeros(l_scr.shape, F32)
        a_scr[...] = jnp.zeros(a_scr.shape, F32)
        new_mask = lax.broadcasted_iota(jnp.int32, (2 * s_q, kn_ref.shape[0]), 1) < n_new
        for h in range(heads):
            hs = slice(h * hd, (h + 1) * hd)
            qd[h, 0:s_q, 0:dh] = q_ref[:, h * hd:h * hd + dh]
            qd[h, s_q:2 * s_q, dh:hd] = q_ref[:, h * hd + dh:(h + 1) * hd]
            s = lax.dot_general(qd[h], kn_ref[:, hs], NT_DIMS, preferred_element_type=F32)
            s = jnp.where(new_mask, s, NEG_INF)
            _softmax_step(s, vn_ref[:, hs], m_scr.at[h], l_scr.at[h], a_scr.at[h])

    tk = kc_ref.shape[0]
    k_all = kc_ref[...].astype(BF16).reshape(tk, heads * hd)
    v_all = vc_ref[...].astype(BF16).reshape(tk, heads * hd)
    scores = [lax.dot_general(qd[h], k_all[:, h * hd:(h + 1) * hd], NT_DIMS,
                              preferred_element_type=F32) for h in range(heads)]
    for h in range(heads):
        _softmax_step(scores[h], v_all[:, h * hd:(h + 1) * hd], m_scr.at[h], l_scr.at[h],
                      a_scr.at[h])

    @pl.when(j == pl.num_programs(1) - 1)
    def _():
        lam = _lam_value(lamc_ref, lq1_ref, lk1_ref, lq2_ref, lk2_ref)
        for h in range(heads):
            a, l = a_scr[h], l_scr[h]
            o = _diff_finish(a[:s_q], l[:s_q], a[s_q:], l[s_q:], lam, g_ref[...], lamc_ref[1])
            o_ref[:, h * hd:(h + 1) * hd] = o.astype(BF16)


def _attn_decode(dq, k16, v16, cache_k, cache_v, layer, gain, lam_init, lams, *, batch, seq,
                 heads, tk_pref=1024):
    n, width = k16.shape
    hd = width // heads
    dh = hd // 2
    past = cache_k.shape[2]
    tk = _pick_tile(past, tk_pref, V7X_LANES)
    pad = V7X_LANES
    kn = jnp.pad(k16.reshape(batch, seq, width), ((0, 0), (0, pad - seq), (0, 0)))
    vn = jnp.pad(v16.reshape(batch, seq, width), ((0, 0), (0, pad - seq), (0, 0)))
    lam_specs, lam_args = _lam_inputs(lam_init, lams)
    in_specs = lam_specs + [
        pl.BlockSpec((seq, width), lambda b, j: (b, 0)),
        pl.BlockSpec((None, pad, width), lambda b, j: (b, 0, 0)),
        pl.BlockSpec((None, pad, width), lambda b, j: (b, 0, 0)),
        pl.BlockSpec((None, None, tk, heads, hd), lambda b, j: (layer, b, j, 0, 0)),
        pl.BlockSpec((None, None, tk, heads, hd), lambda b, j: (layer, b, j, 0, 0)),
        pl.BlockSpec((1, hd), lambda b, j: (0, 0)),
    ]
    scratch = [pltpu.VMEM((heads, 2 * seq, hd), BF16),
               pltpu.VMEM((heads, 2 * seq, V7X_LANES), F32),
               pltpu.VMEM((heads, 2 * seq, V7X_LANES), F32),
               pltpu.VMEM((heads, 2 * seq, hd), F32)]
    vmem = 4 * tk * width * 4 + 2 * tk * width * 2 + 8 * pad * width * 2 + (8 << 20)
    return pl.pallas_call(
        functools.partial(_attn_decode_kernel, heads=heads, dh=dh, n_new=seq),
        grid=(batch, past // tk), in_specs=in_specs,
        out_specs=pl.BlockSpec((seq, width), lambda b, j: (b, 0)),
        out_shape=jax.ShapeDtypeStruct((n, width), BF16),
        scratch_shapes=scratch, compiler_params=_params(vmem, 2), name="diff_attn_decode",
    )(*lam_args, dq, kn, vn, cache_k, cache_v, gain.reshape(1, hd))


def _ret_tables(heads, chunk, dk, dv):
    log_gamma = jnp.log1p(-jnp.exp2(-5.0 - jnp.arange(heads, dtype=F32)))
    idx = jnp.arange(chunk, dtype=F32)
    diff = idx[:, None] - idx[None, :]
    decay = jnp.where(diff >= 0, jnp.exp(log_gamma[:, None, None] * jnp.maximum(diff, 0.0)), 0.0)
    cross = jnp.exp(log_gamma[:, None] * (idx + 1.0)[None, :])
    kdec = jnp.exp(log_gamma[:, None] * (chunk - 1.0 - idx)[None, :])
    carry = jnp.exp(log_gamma * chunk)
    return (decay,
            jnp.broadcast_to(cross[:, :, None], (heads, chunk, dv)),
            jnp.broadcast_to(kdec[:, :, None], (heads, chunk, dk)),
            carry)


def _ret_kernel(*refs, has_init):
    refs = list(refs)
    carry_ref, q_ref, k_ref, v_ref, rg_ref, dec_ref, cross_ref, kdec_ref, g_ref = refs[:9]
    refs = refs[9:]
    s0_ref = refs.pop(0) if has_init else None
    o_ref, sout_ref, s_scr = refs
    heads, dk, dv = s_scr.shape

    @pl.when(pl.program_id(1) == 0)
    def _():
        s_scr[...] = s0_ref[...] if has_init else jnp.zeros(s_scr.shape, F32)

    for h in range(heads):
        ks, vs = slice(h * dk, (h + 1) * dk), slice(h * dv, (h + 1) * dv)
        q, k, v = q_ref[:, ks], k_ref[:, ks], v_ref[:, vs]
        state = s_scr[h]
        scores = lax.dot_general(q, k, NT_DIMS, preferred_element_type=F32) * dec_ref[h]
        inner = jnp.dot(scores.astype(BF16), v, preferred_element_type=F32)
        cross = jnp.dot(q, state.astype(BF16), preferred_element_type=F32) * cross_ref[h]
        k_dec = (k.astype(F32) * kdec_ref[h]).astype(BF16)
        new_state = carry_ref[h] * state + lax.dot_general(k_dec, v, TN_DIMS,
                                                           preferred_element_type=F32)
        s_scr[h] = new_state
        sout_ref[h] = new_state
        gate = rg_ref[:, vs].astype(F32)
        y = _rms_rows(inner + cross, g_ref[...]) * (gate * jax.nn.sigmoid(gate))
        o_ref[:, vs] = y.astype(BF16)


def _cast_slices(weights, layer, n_steps, step_of):
    in_specs, out_specs, out_shapes, vmem = [], [], [], 0
    for w in weights:
        rows, cols = w.shape[1:]
        if rows % (16 * n_steps):
            return None
        r = rows // n_steps
        in_specs.append(pl.BlockSpec((None, r, cols), lambda *g: (layer, step_of(*g), 0)))
        out_specs.append(pl.BlockSpec((r, cols), lambda *g: (step_of(*g), 0)))
        out_shapes.append(jax.ShapeDtypeStruct((rows, cols), BF16))
        vmem += 2 * r * cols * (4 + 2)
    return in_specs, out_specs, out_shapes, vmem


def _retention(rqk, main, c_v, c_gate, dv, gain, state0, layer, *, batch, seq, heads,
               chunk_pref=256):
    n = rqk.shape[0]
    dk = rqk.shape[1] // (2 * heads)
    assert c_v % (heads * dv) == 0 and c_gate % (heads * dv) == 0
    vb, gb = c_v // (heads * dv), c_gate // (heads * dv)
    chunk = _pick_tile(seq, chunk_pref, 16)
    nc = seq // chunk
    decay, cross, kdec, carry = _ret_tables(heads, chunk, dk, dv)
    has_init = state0 is not None
    row = lambda b, c: b * nc + c
    whole = lambda shape: pl.BlockSpec(shape, lambda b, c: (0,) * len(shape))
    in_specs = [
        pl.BlockSpec(memory_space=pltpu.SMEM),
        pl.BlockSpec((chunk, heads * dk), lambda b, c: (row(b, c), 0)),
        pl.BlockSpec((chunk, heads * dk), lambda b, c: (row(b, c), 1)),
        pl.BlockSpec((chunk, heads * dv), lambda b, c: (row(b, c), vb)),
        pl.BlockSpec((chunk, heads * dv), lambda b, c: (row(b, c), gb)),
        whole((heads, chunk, chunk)), whole((heads, chunk, dv)), whole((heads, chunk, dk)),
        whole((1, dv)),
    ]
    args = [carry, rqk, rqk, main, main, decay, cross, kdec, gain.reshape(1, dv)]
    if has_init:
        in_specs.append(pl.BlockSpec((None, None, heads, dk, dv),
                                     lambda b, c: (layer, b, 0, 0, 0)))
        args.append(state0)
    vmem = (2 * heads * chunk * (chunk + dv + dk) * 4 + 8 * chunk * heads * (dk + dv) * 2
            + 6 * heads * dk * dv * 4 + (8 << 20))
    ob, st = pl.pallas_call(
        functools.partial(_ret_kernel, has_init=has_init),
        grid=(batch, nc), in_specs=in_specs,
        out_specs=[pl.BlockSpec((chunk, heads * dv), lambda b, c: (row(b, c), 0)),
                   pl.BlockSpec((None, heads, dk, dv), lambda b, c: (b, 0, 0, 0))],
        out_shape=[jax.ShapeDtypeStruct((n, heads * dv), BF16),
                   jax.ShapeDtypeStruct((batch, heads, dk, dv), F32)],
        scratch_shapes=[pltpu.VMEM((heads, dk, dv), F32)],
        compiler_params=_params(vmem, 2), name="retention",
    )(*args)
    return ob, st


def _merge_kernel(oa_ref, ob_ref, wa_ref, wb_ref, ga_ref, gb_ref, *refs, w_scale):
    if w_scale is not None:
        wsrc_ref, o_ref, wdst_ref = refs
        col = lax.broadcasted_iota(jnp.int32, (1, wsrc_ref.shape[1]), 1)
        wdst_ref[...] = (wsrc_ref[...] * jnp.where(col < w_scale[0], w_scale[1], 1.0)).astype(BF16)
    else:
        (o_ref,) = refs
    a = jnp.dot(oa_ref[...], wa_ref[...], preferred_element_type=F32)
    b = jnp.dot(ob_ref[...], wb_ref[...], preferred_element_type=F32)
    ga = jax.nn.sigmoid(ga_ref[...].astype(F32))
    gb = jax.nn.sigmoid(gb_ref[...].astype(F32))
    o_ref[...] = (ga * a + gb * b).astype(BF16)


def _merge(oa, ob, main, c_ga, c_gb, w_a, w_b, layer, *, tm_pref=1024, tn_pref=512,
           cast_w_in=None):
    n, k = oa.shape
    d = w_a.shape[2]
    tm = _pick_tile(n, tm_pref, 16)
    tn = _pick_tile(d, tn_pref, V7X_LANES)
    assert c_ga % tn == 0 and c_gb % tn == 0
    ga0, gb0 = c_ga // tn, c_gb // tn
    n_j = d // tn
    in_specs = [pl.BlockSpec((tm, k), lambda i, j: (i, 0)),
                pl.BlockSpec((tm, k), lambda i, j: (i, 0)),
                pl.BlockSpec((None, k, tn), lambda i, j: (layer, 0, j)),
                pl.BlockSpec((None, k, tn), lambda i, j: (layer, 0, j)),
                pl.BlockSpec((tm, tn), lambda i, j: (i, ga0 + j)),
                pl.BlockSpec((tm, tn), lambda i, j: (i, gb0 + j))]
    args = [oa, ob, w_a, w_b, main, main]
    out_specs = [pl.BlockSpec((tm, tn), lambda i, j: (i, j))]
    out_shape = [jax.ShapeDtypeStruct((n, d), BF16)]
    vmem = 4 * tm * k * 2 + 4 * k * tn * 2 + 6 * tm * tn * 2 + 4 * tm * tn * 4 + (4 << 20)
    slices = None
    if cast_w_in is not None:
        slices = _cast_slices(cast_w_in[:1], cast_w_in[1], (n // tm) * n_j,
                              lambda i, j: i * n_j + j)
    if slices:
        in_specs += slices[0]
        args.append(cast_w_in[0])
        out_specs += slices[1]
        out_shape += slices[2]
        vmem += slices[3]
    outs = pl.pallas_call(
        functools.partial(_merge_kernel, w_scale=cast_w_in[2:] if slices else None),
        grid=(n // tm, n_j), in_specs=in_specs, out_specs=out_specs, out_shape=out_shape,
        compiler_params=_params(vmem, 2), name="branch_merge",
    )(*args)
    if cast_w_in is None:
        return outs[0], None
    if slices:
        return outs[0], outs[1]
    w, w_layer, q_cols, q_scale = cast_w_in
    return outs[0], _cast_w_in(w[w_layer:w_layer + 1], q_cols, q_scale)[0]


def _ffn_kernel(*refs, row_chunk, final_norm):
    refs = list(refs)
    x_ref, g_ref, wu_ref, wd_ref = refs[:4]
    fg_ref = refs[4] if final_norm else None
    o_ref, h_scr = refs[-2:]
    f = pl.program_id(1)
    n_chunks = x_ref.shape[0] // row_chunk

    @pl.when(f == 0)
    def _():
        def rows(r, carry):
            rs = pl.ds(pl.multiple_of(r * row_chunk, row_chunk), row_chunk)
            x = x_ref[rs, :]
            h_scr[rs, :] = _rms_rows(x, g_ref[...]).astype(BF16)
            o_ref[rs, :] = x
            return carry
        lax.fori_loop(0, n_chunks, rows, 0, unroll=2)

    u = jnp.dot(h_scr[...], wu_ref[...], preferred_element_type=F32)
    u = jnp.square(jnp.maximum(u, 0.0)).astype(BF16)
    o_ref[...] += jnp.dot(u, wd_ref[...], preferred_element_type=F32)

    if final_norm:
        @pl.when(f == pl.num_programs(1) - 1)
        def _():
            def rows(r, carry):
                rs = pl.ds(pl.multiple_of(r * row_chunk, row_chunk), row_chunk)
                o_ref[rs, :] = _rms_rows(o_ref[rs, :], fg_ref[...])
                return carry
            lax.fori_loop(0, n_chunks, rows, 0)


def _ffn(x, gain, w_up, w_down, layer, final_gain=None, *, tm_pref=1024, tf_pref=512):
    n, d = x.shape
    d_ff = w_up.shape[2]
    tm = _pick_tile(n, tm_pref, 16)
    tf = _pick_tile(d_ff, tf_pref, V7X_LANES)
    final_norm = final_gain is not None
    in_specs = [pl.BlockSpec((tm, d), lambda i, f: (i, 0)),
                pl.BlockSpec((1, d), lambda i, f: (0, 0)),
                pl.BlockSpec((None, d, tf), lambda i, f: (layer, 0, f)),
                pl.BlockSpec((None, tf, d), lambda i, f: (layer, f, 0))]
    args = [x, gain.reshape(1, d), w_up, w_down]
    if final_norm:
        in_specs.append(pl.BlockSpec((1, d), lambda i, f: (0, 0)))
        args.append(final_gain.reshape(1, d))
    vmem = 4 * tm * d * 4 + tm * d * 2 + 4 * d * tf * 2 + 2 * tm * tf * 4 + (6 << 20)
    return pl.pallas_call(
        functools.partial(_ffn_kernel, row_chunk=min(tm, 64), final_norm=final_norm),
        grid=(n // tm, d_ff // tf), in_specs=in_specs,
        out_specs=pl.BlockSpec((tm, d), lambda i, f: (i, 0)),
        out_shape=jax.ShapeDtypeStruct((n, d), F32),
        scratch_shapes=[pltpu.VMEM((tm, d), BF16)],
        compiler_params=_params(vmem, 2), name="ffn",
    )(*args)


def _rope_tables(pos, dk, batch):
    inv = 1.0 / (ROPE_BASE ** jnp.linspace(0.0, 1.0, dk // 2, dtype=F32))
    th = pos.astype(F32)[:, None] * jnp.repeat(inv, 2)[None, :]
    sin, cos = jnp.sin(th), jnp.cos(th)
    even = (jnp.arange(dk) % 2) == 0
    sin_next = jnp.where(even, -sin, 0.0)
    sin_prev = jnp.where(even, 0.0, sin)
    return tuple(jnp.tile(t, (batch, 1)) for t in (cos, sin_next, sin_prev))


def _cast_w_in(w_in, diff_w, q_scale):
    col_scale = jnp.where(jnp.arange(w_in.shape[-1]) < diff_w, q_scale, 1.0).astype(F32)
    return (w_in * col_scale).astype(BF16)


def _mixer(x, layer, w, w_in, p, rope, past, kv_stack, side, *, depth, batch, seq, heads,
           ret_heads):
    d = x.shape[1]
    hd = p["subln_g"].shape[-1]
    diff_w = heads * hd
    dk = p["state"].shape[-2]
    dv = p["state"].shape[-1]
    qk_w = ret_heads * dk
    ret_w = ret_heads * dv
    c_k, c_v, c_rqk = diff_w, 2 * diff_w, 3 * diff_w
    skipped = 2 * diff_w + 2 * qk_w
    c_rv, c_rg, c_ga, c_gb = diff_w, diff_w + ret_w, diff_w + 2 * ret_w, diff_w + 2 * ret_w + d
    main_w = c_gb + d

    main, h = _proj(x, *w_in, 0, main_w, gain=p["norm_mix_g"], emit_h=True,
                    gap=(diff_w, skipped))
    k32, k16, w_up_l = _proj_heads(h, *w_in, c_k, heads, kv_stack[0], depth, layer,
                                   cast=None if past else side["up"])
    v32, v16, w_down_l = _proj_heads(h, *w_in, c_v, heads, kv_stack[1], depth, layer,
                                     cast=None if past else side["down"])
    rqk = _proj(h, *w_in, c_rqk, 2 * qk_w, epi="rope", extra=rope,
                rope_scale=dk ** -0.5, tn_pref=qk_w)

    lam_init = 0.8 - 0.6 * math.exp(-0.3 * layer)
    lams = (p["lq1"], p["lk1"], p["lq2"], p["lk2"])
    cast_w_in, ffn_w = None, (w_up_l, w_down_l)
    if past:
        oa = _attn_decode(main, k16, v16, p["cache_k"], p["cache_v"], layer, p["subln_g"],
                          lam_init, lams, batch=batch, seq=seq, heads=heads)
        state0 = p["state"]
    else:
        if layer + 1 < depth:
            cast_w_in = (side["in"], layer + 1, diff_w, side["q_scale"])
        oa = _attn_prompt(main, k16, v16, p["subln_g"], lam_init, lams, batch=batch,
                          seq=seq, heads=heads)
        state0 = None
    ob, st = _retention(rqk, main, c_rv, c_rg, dv, p["ret_g"], state0, layer,
                        batch=batch, seq=seq, heads=ret_heads)
    merged, w_in_next = _merge(oa, ob, main, c_ga, c_gb, w["a"], w["b"], layer,
                               cast_w_in=cast_w_in)
    x = _proj(merged, w["out"], layer, 0, d, epi="residual", extra=(x,), tm_pref=512, tn_pref=d)
    return x, (k32, v32), st, w_in_next, ffn_w


def kernel(x_prompt, x_sample, cache_diff_k, cache_diff_v, state_ret, norm_mix_g, w_in,
           lambda_q1, lambda_k1, lambda_q2, lambda_k2, diff_subln_g, ret_norm_g,
           w_branch_a, w_branch_b, w_out, norm_ffn_g, w_up, w_down, final_norm_g):
    depth, dec_batch, past_len, heads, hd = cache_diff_k.shape
    batch, seq, d = x_prompt.shape
    dec_seq = x_sample.shape[1]
    ret_heads, dk, dv = state_ret.shape[2:]

    q_scale = (hd // 2) ** -0.5 * LOG2E
    w = {"a": w_branch_a.astype(BF16), "b": w_branch_b.astype(BF16), "out": w_out.astype(BF16)}
    side = {"in": w_in, "up": w_up, "down": w_down, "q_scale": q_scale}
    w_in_cur = _cast_w_in(w_in[0:1], heads * hd, q_scale)
    cache_k, cache_v = cache_diff_k, cache_diff_v
    rope_p = _rope_tables(jnp.arange(seq), dk, batch)
    rope_s = _rope_tables(past_len + jnp.arange(dec_seq), dk, dec_batch)

    xp = x_prompt.reshape(batch * seq, d)
    xs = x_sample.reshape(dec_batch * dec_seq, d)
    kv_p, kv_s = (None, None), (None, None)
    states_p, states_s = [], []
    for l in range(depth):
        p = {"norm_mix_g": norm_mix_g[l], "lq1": lambda_q1[l], "lk1": lambda_k1[l],
             "lq2": lambda_q2[l], "lk2": lambda_k2[l], "subln_g": diff_subln_g[l],
             "ret_g": ret_norm_g[l], "cache_k": cache_k, "cache_v": cache_v, "state": state_ret}
        final_g = final_norm_g if l == depth - 1 else None
        xp, kv_p, sp, w_in_next, (wu, wd) = _mixer(
            xp, l, w, (w_in_cur, 0), p, rope_p, False, kv_p, side, depth=depth, batch=batch,
            seq=seq, heads=heads, ret_heads=ret_heads)
        xs, kv_s, ss, _, _ = _mixer(
            xs, l, w, (w_in_cur, 0), p, rope_s, True, kv_s, None, depth=depth, batch=dec_batch,
            seq=dec_seq, heads=heads, ret_heads=ret_heads)
        xp = _ffn(xp, norm_ffn_g[l], wu[None], wd[None], 0, final_g)
        xs = _ffn(xs, norm_ffn_g[l], wu[None], wd[None], 0, final_g)
        if w_in_next is not None:
            w_in_cur = w_in_next[None]
        states_p.append(sp)
        states_s.append(ss)
    return (xp.reshape(batch, seq, d), xs.reshape(dec_batch, dec_seq, d),
            kv_p[0].reshape(depth, batch, seq, heads, hd),
            kv_p[1].reshape(depth, batch, seq, heads, hd), jnp.stack(states_p),
            kv_s[0].reshape(depth, dec_batch, dec_seq, heads, hd),
            kv_s[1].reshape(depth, dec_batch, dec_seq, heads, hd), jnp.stack(states_s))
```

```python
import functools
import math

import jax
import jax.numpy as jnp
from jax import lax
from jax.experimental import pallas as pl
from jax.experimental.pallas import tpu as pltpu

F32 = jnp.float32
BF16 = jnp.bfloat16

NORM_EPS = 1e-6
ROPE_BASE = 10000.0
MASK_CHUNK = 64
NEG_INF = -1e30
SHIFT_LIMIT = 48.0
NORM_SLACK = 1.02
FAST_SWEEP_GROUP = 12
LOG2E = 1.4426950408889634

V7X_VMEM_BYTES = 64 * 1024 * 1024
V7X_LANES = 128
VMEM_REQUEST_CAP = V7X_VMEM_BYTES - 8 * 1024 * 1024

NT_DIMS = (((1,), (1,)), ((), ()))
TN_DIMS = (((0,), (0,)), ((), ()))


def _pick_tile(n, pref, align):
    if n <= pref:
        return n
    t = pref - pref % align
    while t >= align:
        if n % t == 0:
            return t
        t -= align
    raise ValueError(f"no tile for {n} (pref {pref}, align {align})")


def _params(vmem_bytes, n_grid):
    return pltpu.CompilerParams(
        dimension_semantics=("arbitrary",) * n_grid,
        vmem_limit_bytes=int(min(VMEM_REQUEST_CAP, vmem_bytes)))


def _rms_rows(x, gain):
    ms = jnp.mean(x * x, axis=-1, keepdims=True)
    return x * lax.rsqrt(ms + NORM_EPS) * gain


def _proj_kernel(*refs, has_norm, emit_h, epi, row_chunk, rope_scale):
    refs = list(refs)
    x_ref = refs.pop(0)
    g_ref = refs.pop(0) if has_norm else None
    w_ref = refs.pop(0)
    if epi == "rope":
        cos_ref, sina_ref, sinb_ref = refs.pop(0), refs.pop(0), refs.pop(0)
    if epi == "residual":
        res_ref = refs.pop(0)
    o_ref = refs.pop(0)
    hout_ref = refs.pop(0) if emit_h else None
    h_scr = refs.pop(0) if has_norm else None

    j = pl.program_id(1)
    if has_norm:
        @pl.when(j == 0)
        def _():
            def rows(r, carry):
                rs = pl.ds(pl.multiple_of(r * row_chunk, row_chunk), row_chunk)
                hrow = _rms_rows(x_ref[rs, :], g_ref[...]).astype(BF16)
                h_scr[rs, :] = hrow
                if emit_h:
                    hout_ref[rs, :] = hrow
                return carry
            lax.fori_loop(0, x_ref.shape[0] // row_chunk, rows, 0, unroll=2)
        lhs = h_scr[...]
    else:
        lhs = x_ref[...]

    acc = jnp.dot(lhs, w_ref[...], preferred_element_type=F32)
    if epi == "bf16":
        o_ref[...] = acc.astype(BF16)
    elif epi == "residual":
        o_ref[...] = res_ref[...] + acc
    elif epi == "rope":
        n_tiles = pl.num_programs(1)
        scale = jnp.where(j < n_tiles // 2, 1.0, rope_scale).astype(F32)
        cos, sina, sinb = cos_ref[...], sina_ref[...], sinb_ref[...]
        for c in range(acc.shape[1] // V7X_LANES):
            cs = slice(c * V7X_LANES, (c + 1) * V7X_LANES)
            xs = acc[:, cs]
            nxt = pltpu.roll(xs, V7X_LANES - 1, 1)
            prv = pltpu.roll(xs, 1, 1)
            o_ref[:, cs] = ((xs * cos + nxt * sina + prv * sinb) * scale).astype(BF16)
    else:
        raise ValueError(epi)


def _proj(lhs, w, layer, col0, ncols, *, gain=None, emit_h=False, epi="bf16", extra=(),
          rope_scale=1.0, gap=(0, 0), tm_pref=1024, tn_pref=1024):
    n, k = lhs.shape
    has_norm = gain is not None
    tm = _pick_tile(n, tm_pref, 16)
    tn = _pick_tile(ncols, tn_pref, V7X_LANES)
    assert col0 % tn == 0 and gap[0] % tn == 0 and gap[1] % tn == 0
    cb = col0 // tn
    lead_tiles, skip_tiles = gap[0] // tn, gap[1] // tn
    grid = (n // tm, ncols // tn)

    def w_tile(j):
        return cb + j + jnp.where(j >= lead_tiles, skip_tiles, 0) if skip_tiles else cb + j

    in_specs = [pl.BlockSpec((tm, k), lambda i, j: (i, 0))]
    args = [lhs]
    if has_norm:
        in_specs.append(pl.BlockSpec((1, k), lambda i, j: (0, 0)))
        args.append(gain.reshape(1, k))
    in_specs.append(pl.BlockSpec((None, k, tn), lambda i, j: (layer, 0, w_tile(j))))
    args.append(w)
    if epi == "rope":
        for t in extra:
            in_specs.append(pl.BlockSpec((tm, V7X_LANES), lambda i, j: (i, 0)))
            args.append(t)
    elif epi == "residual":
        in_specs.append(pl.BlockSpec((tm, tn), lambda i, j: (i, j)))
        args.append(extra[0])

    tile_spec = pl.BlockSpec((tm, tn), lambda i, j: (i, j))
    out_dtype = F32 if epi == "residual" else BF16
    out_shape = [jax.ShapeDtypeStruct((n, ncols), out_dtype)]
    out_specs = [tile_spec]
    out_bytes = tm * tn * jnp.dtype(out_dtype).itemsize
    if emit_h:
        out_shape.append(jax.ShapeDtypeStruct((n, k), BF16))
        out_specs.append(pl.BlockSpec((tm, k), lambda i, j: (i, 0)))
        out_bytes += tm * k * 2
    scratch = [pltpu.VMEM((tm, k), BF16)] if has_norm else []

    vmem = (2 * tm * k * lhs.dtype.itemsize + 2 * k * tn * 2 + 2 * out_bytes
            + (tm * k * 2 if has_norm else 0) + 3 * tm * tn * 4
            + (2 * tm * tn * 4 if epi == "residual" else 0) + (4 << 20))
    outs = pl.pallas_call(
        functools.partial(_proj_kernel, has_norm=has_norm, emit_h=emit_h, epi=epi,
                          row_chunk=min(tm, 64), rope_scale=rope_scale),
        grid=grid, in_specs=in_specs, out_specs=out_specs, out_shape=out_shape,
        scratch_shapes=scratch, compiler_params=_params(vmem, 2),
        name=f"proj_{epi}")(*args)
    return outs[0] if len(outs) == 1 else tuple(outs)


def _proj_heads_kernel(*refs, aliased, n_cast):
    x_ref, w_ref = refs[0], refs[1]
    n_in = 2 + aliased + n_cast
    o32_ref, o16_ref = refs[n_in], refs[n_in + 1]
    for c in range(n_cast):
        refs[n_in + 2 + c][...] = refs[n_in - n_cast + c][...].astype(BF16)
    acc = jnp.dot(x_ref[...], w_ref[...], preferred_element_type=F32)
    o16_ref[...] = acc.astype(BF16)
    o32_ref[...] = acc.reshape(o32_ref.shape)


def _proj_heads(lhs, w, w_layer, col0, heads, stacked, depth, layer, *, tm_pref=512, cast=()):
    n, k = lhs.shape
    ncols = k
    hd = ncols // heads
    tm = _pick_tile(n, tm_pref, 16)
    cb = col0 // ncols
    assert col0 % ncols == 0
    aliased = stacked is not None
    in_specs = [pl.BlockSpec((tm, k), lambda i: (i, 0)),
                pl.BlockSpec((None, k, ncols), lambda i: (w_layer, 0, cb))]
    args = [lhs, w]
    if aliased:
        in_specs.append(pl.BlockSpec(memory_space=pl.ANY))
        args.append(stacked)
    vmem = 4 * tm * k * 2 + 4 * k * ncols * 2 + 4 * tm * ncols * 4 + 4 * tm * ncols * 2 + (4 << 20)
    out_specs = [pl.BlockSpec((None, tm, heads, hd), lambda i: (layer, i, 0, 0)),
                 pl.BlockSpec((tm, ncols), lambda i: (i, 0))]
    out_shape = [jax.ShapeDtypeStruct((depth, n, heads, hd), F32),
                 jax.ShapeDtypeStruct((n, ncols), BF16)]
    slices = _cast_slices(cast, layer, n // tm, lambda i: i) if cast else None
    if slices:
        in_specs += slices[0]
        args += list(cast)
        out_specs += slices[1]
        out_shape += slices[2]
        vmem += slices[3]
    outs = pl.pallas_call(
        functools.partial(_proj_heads_kernel, aliased=aliased, n_cast=len(cast) if slices else 0),
        grid=(n // tm,), in_specs=in_specs, out_specs=out_specs, out_shape=out_shape,
        input_output_aliases={2: 0} if aliased else {},
        compiler_params=_params(vmem, 1), name="proj_heads")(*args)
    casted = tuple(outs[2:]) if slices else tuple(c[layer].astype(BF16) for c in cast)
    return outs[0], outs[1], casted


def _lam_value(lamc_ref, lq1_ref, lk1_ref, lq2_ref, lk2_ref):
    a = jnp.exp(jnp.sum(lq1_ref[...] * lk1_ref[...], axis=-1, keepdims=True))
    b = jnp.exp(jnp.sum(lq2_ref[...] * lk2_ref[...], axis=-1, keepdims=True))
    return a - b + lamc_ref[0]


def _lane_repeat(x, reps):
    return x if reps == 1 else jnp.concatenate([x] * reps, axis=1)


def _softmax_update(s, v, m_prev, l_prev, acc_prev):
    keys = s.shape[1]
    m_new = jnp.maximum(m_prev, jnp.max(s, axis=1, keepdims=True))
    alpha = jnp.exp2(m_prev - m_new)
    p = jnp.exp2(s - _lane_repeat(m_new, keys // V7X_LANES))
    l_new = alpha * l_prev + jnp.sum(p, axis=1, keepdims=True)
    pv = jnp.dot(p.astype(BF16), v, preferred_element_type=F32)
    acc_new = acc_prev * _lane_repeat(alpha, acc_prev.shape[1] // V7X_LANES) + pv
    return m_new, l_new, acc_new


def _softmax_step(s, v, m_ref, l_ref, acc_ref):
    m_ref[...], l_ref[...], acc_ref[...] = _softmax_update(s, v, m_ref[...], l_ref[...],
                                                           acc_ref[...])


def _diff_finish(a1, l1, a2, l2, lam, gain, out_scale):
    rep = a1.shape[1] // V7X_LANES
    o = a1 * _lane_repeat(1.0 / l1, rep) - a2 * _lane_repeat(lam / l2, rep)
    return _rms_rows(o, gain * out_scale)


def _attn_prompt_kernel(lamc_ref, lq1_ref, lk1_ref, lq2_ref, lk2_ref, q_ref, k_ref, v_ref, g_ref,
                        o_ref, m1, l1, a1, m2, l2, a2, sa1, sa2, sb1, sb2, ka1, ka2, ext, *, dh, tq):
    seq = q_ref.shape[0]
    tk = tq
    nq = seq // tq
    n_units = nq * (nq + 1) // 2
    shift = MASK_CHUNK.bit_length() - 1
    assert tq // MASK_CHUNK < dh
    maps = ((0, m1, l1, a1, ka1), (dh, m2, l2, a2, ka2))

    lane = lax.broadcasted_iota(jnp.int32, (tk, dh), 1)
    row_chunk = lax.shift_right_logical(lax.broadcasted_iota(jnp.int32, (tk, dh), 0), shift)
    key_ext = jnp.where(lane == 0, -1.0, jnp.where(lane - 1 == row_chunk, 1.0, 0.0)).astype(BF16)
    hidden = (lane >= 1) & (lane <= tq // MASK_CHUNK) & (lane - 1 > row_chunk)

    ones = jnp.ones((dh, dh), BF16)

    def widen(c, carry):
        rs = pl.ds(pl.multiple_of(c * tk, tk), tk)
        out = []
        for idx, (lo, _, l_ref, a_ref, ka) in enumerate(maps):
            l_ref[rs, :] = jnp.zeros((tk, V7X_LANES), F32)
            a_ref[rs, :] = jnp.zeros((tk, 2 * dh), F32)
            kc = k_ref[rs, lo:lo + dh]
            qc = q_ref[rs, lo:lo + dh]
            ka[rs, 0:dh] = kc
            ka[rs, dh:2 * dh] = key_ext
            k2 = jnp.dot(kc * kc, ones, preferred_element_type=F32)
            q2 = jnp.dot(qc * qc, ones, preferred_element_type=F32)
            out += [jnp.maximum(carry[2 * idx], jnp.max(k2, axis=0, keepdims=True)),
                    jnp.maximum(carry[2 * idx + 1], jnp.max(q2, axis=0, keepdims=True))]
        return tuple(out)
    norms2 = lax.fori_loop(0, nq, widen, (jnp.zeros((1, dh), F32),) * 4)

    worst = jnp.zeros((1, 1), F32)
    for idx in range(2):
        bound = jnp.sqrt(norms2[2 * idx][:, 0:1] * norms2[2 * idx + 1][:, 0:1]) * NORM_SLACK
        worst = jnp.maximum(worst, bound)
        base = jnp.where(lane == 0, bound, 0.0)
        ext[(2 * idx) * tq:(2 * idx + 1) * tq, :] = base.astype(BF16)
        ext[(2 * idx + 1) * tq:(2 * idx + 2) * tq, :] = jnp.where(hidden, NEG_INF, base).astype(BF16)
    fast = jnp.max(worst) <= SHIFT_LIMIT

    lam = _lam_value(lamc_ref, lq1_ref, lk1_ref, lq2_ref, lk2_ref)
    gain, out_scale = g_ref[...], lamc_ref[1]

    def q_rows(qi):
        return pl.ds(pl.multiple_of(qi * tq, tq), tq)

    def k_rows(j):
        return pl.ds(pl.multiple_of(j * tk, tk), tk)

    def scores(qi, j, bufs):
        diag = jnp.where(qi == j, 1, 0)
        for idx, ((lo, _, _, _, ka), s_scr) in enumerate(zip(maps, bufs)):
            e_rows = pl.ds(pl.multiple_of((2 * idx + diag) * tq, tq), tq)
            qa = jnp.concatenate([q_ref[q_rows(qi), lo:lo + dh], ext[e_rows, :]], axis=1)
            s_scr[...] = lax.dot_general(qa, ka[k_rows(j), :], NT_DIMS,
                                         preferred_element_type=F32)

    def consume(qi, j, bufs, exact):
        v = v_ref[k_rows(j), :]
        rq = q_rows(qi)
        for (_, m_ref, l_ref, a_ref, _), s_scr in zip(maps, bufs):
            s = s_scr[...]
            l_prev, a_prev = l_ref[rq, :], a_ref[rq, :]
            if exact:
                m_ref[rq, :], l_new, a_new = _softmax_update(s, v, m_ref[rq, :], l_prev, a_prev)
            else:
                p = jnp.exp2(s)
                l_new = l_prev + jnp.sum(p, axis=1, keepdims=True)
                a_new = a_prev + jnp.dot(p.astype(BF16), v, preferred_element_type=F32)
            l_ref[rq, :], a_ref[rq, :] = l_new, a_new

    def advance(qi, j):
        last = j == qi
        return jnp.where(last, qi + 1, qi), jnp.where(last, 0, j + 1)

    buf_a, buf_b = (sa1, sa2), (sb1, sb2)

    units = [(qi, j) for qi in range(nq) for j in range(qi + 1)]

    def sweep(exact, group):
        if exact:
            for m_ref in (m1, m2):
                m_ref[...] = jnp.full(m_ref.shape, -jnp.inf, F32)
        scores(0, 0, buf_a)

        def trip(t, carry):
            cur = carry
            for g in range(group):
                nxt = advance(*cur)
                over = nxt[0] >= nq
                safe = (jnp.where(over, nq - 1, nxt[0]), jnp.where(over, 0, nxt[1]))
                mine, other = (buf_a, buf_b) if g % 2 == 0 else (buf_b, buf_a)
                scores(*safe, other)
                consume(*cur, mine, exact)
                cur = nxt
            return cur
        n_trips = n_units // group
        lax.fori_loop(0, n_trips, trip, (jnp.int32(0), jnp.int32(0)))
        rest = units[n_trips * group:]
        for g, (qi, j) in enumerate(rest):
            mine, other = (buf_a, buf_b) if g % 2 == 0 else (buf_b, buf_a)
            if g + 1 < len(rest):
                scores(*rest[g + 1], other)
            consume(qi, j, mine, exact)

    @pl.when(fast)
    def _():
        sweep(False, FAST_SWEEP_GROUP)

    @pl.when(jnp.logical_not(fast))
    def _():
        sweep(True, 2)

    def finish(qi, carry):
        rq = q_rows(qi)
        o = _diff_finish(a1[rq, :], l1[rq, :], a2[rq, :], l2[rq, :], lam, gain, out_scale)
        o_ref[rq, :] = o.astype(BF16)
        return carry
    lax.fori_loop(0, nq, finish, 0)


def _lam_inputs(lam_init, lams):
    specs = [pl.BlockSpec(memory_space=pltpu.SMEM)]
    args = [jnp.array([lam_init, 1.0 - lam_init], F32)]
    for v in lams:
        specs.append(pl.BlockSpec((1, v.shape[-1]), lambda *_: (0, 0)))
        args.append(v.reshape(1, -1))
    return specs, args


def _attn_prompt(dq, k16, v16, gain, lam_init, lams, *, batch, seq, heads, tq_pref=512):
    n, width = k16.shape
    hd = width // heads
    dh = hd // 2
    tq = _pick_tile(seq, tq_pref, V7X_LANES)
    lam_specs, lam_args = _lam_inputs(lam_init, lams)
    per_head = pl.BlockSpec((seq, hd), lambda b, h: (b, h))
    in_specs = lam_specs + [per_head, per_head, per_head,
                            pl.BlockSpec((1, hd), lambda b, h: (0, 0))]
    stat = pltpu.VMEM((seq, V7X_LANES), F32)
    accs = pltpu.VMEM((seq, hd), F32)
    sbuf = pltpu.VMEM((tq, tq), F32)
    wide_k = pltpu.VMEM((seq, hd), BF16)
    vmem = 8 * seq * hd * 2 + 2 * seq * hd * 2 + 2 * seq * hd * 4 + 4 * seq * V7X_LANES * 4 \
        + 12 * tq * tq * 4 + (4 << 20)
    return pl.pallas_call(
        functools.partial(_attn_prompt_kernel, dh=dh, tq=tq),
        grid=(batch, heads), in_specs=in_specs, out_specs=per_head,
        out_shape=jax.ShapeDtypeStruct((n, width), BF16),
        scratch_shapes=[stat, stat, accs, stat, stat, accs, sbuf, sbuf, sbuf, sbuf,
                        wide_k, wide_k, pltpu.VMEM((4 * tq, dh), BF16)],
        compiler_params=_params(vmem, 2), name="diff_attn_prompt",
    )(*lam_args, dq, k16, v16, gain.reshape(1, hd))


def _attn_decode_kernel(lamc_ref, lq1_ref, lk1_ref, lq2_ref, lk2_ref, q_ref, kn_ref, vn_ref,
                        kc_ref, vc_ref, g_ref, o_ref, qd, m_scr, l_scr, a_scr,
                        *, heads, dh, n_new):
    j = pl.program_id(1)
    hd = 2 * dh
    s_q = q_ref.shape[0]

    @pl.when(j == 0)
    def _():
        qd[...] = jnp.zeros(qd.shape, BF16)
        m_scr[...] = jnp.full(m_scr.shape, -jnp.inf, F32)
        l_scr[...] = jnp.zeros(l_scr.shape, F32)
        a_scr[...] = jnp.zeros(a_scr.shape, F32)
        new_mask = lax.broadcasted_iota(jnp.int32, (2 * s_q, kn_ref.shape[0]), 1) < n_new
        for h in range(heads):
            hs = slice(h * hd, (h + 1) * hd)
            qd[h, 0:s_q, 0:dh] = q_ref[:, h * hd:h * hd + dh]
            qd[h, s_q:2 * s_q, dh:hd] = q_ref[:, h * hd + dh:(h + 1) * hd]
            s = lax.dot_general(qd[h], kn_ref[:, hs], NT_DIMS, preferred_element_type=F32)
            s = jnp.where(new_mask, s, NEG_INF)
            _softmax_step(s, vn_ref[:, hs], m_scr.at[h], l_scr.at[h], a_scr.at[h])

    tk = kc_ref.shape[0]
    k_all = kc_ref[...].astype(BF16).reshape(tk, heads * hd)
    v_all = vc_ref[...].astype(BF16).reshape(tk, heads * hd)
    scores = [lax.dot_general(qd[h], k_all[:, h * hd:(h + 1) * hd], NT_DIMS,
                              preferred_element_type=F32) for h in range(heads)]
    for h in range(heads):
        _softmax_step(scores[h], v_all[:, h * hd:(h + 1) * hd], m_scr.at[h], l_scr.at[h],
                      a_scr.at[h])

    @pl.when(j == pl.num_programs(1) - 1)
    def _():
        lam = _lam_value(lamc_ref, lq1_ref, lk1_ref, lq2_ref, lk2_ref)
        for h in range(heads):
            a, l = a_scr[h], l_scr[h]
            o = _diff_finish(a[:s_q], l[:s_q], a[s_q:], l[s_q:], lam, g_ref[...], lamc_ref[1])
            o_ref[:, h * hd:(h + 1) * hd] = o.astype(BF16)


def _attn_decode(dq, k16, v16, cache_k, cache_v, layer, gain, lam_init, lams, *, batch, seq,
                 heads, tk_pref=1024):
    n, width = k16.shape
    hd = width // heads
    dh = hd // 2
    past = cache_k.shape[2]
    tk = _pick_tile(past, tk_pref, V7X_LANES)
    pad = V7X_LANES
    kn = jnp.pad(k16.reshape(batch, seq, width), ((0, 0), (0, pad - seq), (0, 0)))
    vn = jnp.pad(v16.reshape(batch, seq, width), ((0, 0), (0, pad - seq), (0, 0)))
    lam_specs, lam_args = _lam_inputs(lam_init, lams)
    in_specs = lam_specs + [
        pl.BlockSpec((seq, width), lambda b, j: (b, 0)),
        pl.BlockSpec((None, pad, width), lambda b, j: (b, 0, 0)),
        pl.BlockSpec((None, pad, width), lambda b, j: (b, 0, 0)),
        pl.BlockSpec((None, None, tk, heads, hd), lambda b, j: (layer, b, j, 0, 0)),
        pl.BlockSpec((None, None, tk, heads, hd), lambda b, j: (layer, b, j, 0, 0)),
        pl.BlockSpec((1, hd), lambda b, j: (0, 0)),
    ]
    scratch = [pltpu.VMEM((heads, 2 * seq, hd), BF16),
               pltpu.VMEM((heads, 2 * seq, V7X_LANES), F32),
               pltpu.VMEM((heads, 2 * seq, V7X_LANES), F32),
               pltpu.VMEM((heads, 2 * seq, hd), F32)]
    vmem = 4 * tk * width * 4 + 2 * tk * width * 2 + 8 * pad * width * 2 + (8 << 20)
    return pl.pallas_call(
        functools.partial(_attn_decode_kernel, heads=heads, dh=dh, n_new=seq),
        grid=(batch, past // tk), in_specs=in_specs,
        out_specs=pl.BlockSpec((seq, width), lambda b, j: (b, 0)),
        out_shape=jax.ShapeDtypeStruct((n, width), BF16),
        scratch_shapes=scratch, compiler_params=_params(vmem, 2), name="diff_attn_decode",
    )(*lam_args, dq, kn, vn, cache_k, cache_v, gain.reshape(1, hd))


def _ret_tables(heads, chunk, dk, dv):
    log_gamma = jnp.log1p(-jnp.exp2(-5.0 - jnp.arange(heads, dtype=F32)))
    idx = jnp.arange(chunk, dtype=F32)
    diff = idx[:, None] - idx[None, :]
    decay = jnp.where(diff >= 0, jnp.exp(log_gamma[:, None, None] * jnp.maximum(diff, 0.0)), 0.0)
    cross = jnp.exp(log_gamma[:, None] * (idx + 1.0)[None, :])
    kdec = jnp.exp(log_gamma[:, None] * (chunk - 1.0 - idx)[None, :])
    carry = jnp.exp(log_gamma * chunk)
    return (decay,
            jnp.broadcast_to(cross[:, :, None], (heads, chunk, dv)),
            jnp.broadcast_to(kdec[:, :, None], (heads, chunk, dk)),
            carry)


def _ret_kernel(*refs, has_init):
    refs = list(refs)
    carry_ref, q_ref, k_ref, v_ref, rg_ref, dec_ref, cross_ref, kdec_ref, g_ref = refs[:9]
    refs = refs[9:]
    s0_ref = refs.pop(0) if has_init else None
    o_ref, sout_ref, s_scr = refs[-3:]
    heads, dk, dv = s_scr.shape

    @pl.when(pl.program_id(1) == 0)
    def _():
        s_scr[...] = s0_ref[...] if has_init else jnp.zeros(s_scr.shape, F32)

    for h in range(heads):
        ks, vs = slice(h * dk, (h + 1) * dk), slice(h * dv, (h + 1) * dv)
        q, k, v = q_ref[:, ks], k_ref[:, ks], v_ref[:, vs]
        state = s_scr[h]
        scores = lax.dot_general(q, k, NT_DIMS, preferred_element_type=F32) * dec_ref[h]
        inner = jnp.dot(scores.astype(BF16), v, preferred_element_type=F32)
        cross = jnp.dot(q, state.astype(BF16), preferred_element_type=F32) * cross_ref[h]
        k_dec = (k.astype(F32) * kdec_ref[h]).astype(BF16)
        new_state = carry_ref[h] * state + lax.dot_general(k_dec, v, TN_DIMS,
                                                           preferred_element_type=F32)
        s_scr[h] = new_state
        sout_ref[h] = new_state
        gate = rg_ref[:, vs].astype(F32)
        y = _rms_rows(inner + cross, g_ref[...]) * (gate * jax.nn.sigmoid(gate))
        o_ref[:, vs] = y.astype(BF16)


def _cast_slices(weights, layer, n_steps, step_of):
    in_specs, out_specs, out_shapes, vmem = [], [], [], 0
    for w in weights:
        rows, cols = w.shape[1:]
        if rows % (16 * n_steps):
            return None
        r = rows // n_steps
        in_specs.append(pl.BlockSpec((None, r, cols), lambda *g: (layer, step_of(*g), 0)))
        out_specs.append(pl.BlockSpec((r, cols), lambda *g: (step_of(*g), 0)))
        out_shapes.append(jax.ShapeDtypeStruct((rows, cols), BF16))
        vmem += 2 * r * cols * (4 + 2)
    return in_specs, out_specs, out_shapes, vmem


def _retention(rqk, main, c_v, c_gate, dv, gain, state0, layer, state_stack, depth, *, batch, seq,
               heads, chunk_pref=256):
    n = rqk.shape[0]
    dk = rqk.shape[1] // (2 * heads)
    assert c_v % (heads * dv) == 0 and c_gate % (heads * dv) == 0
    vb, gb = c_v // (heads * dv), c_gate // (heads * dv)
    chunk = _pick_tile(seq, chunk_pref, 16)
    nc = seq // chunk
    decay, cross, kdec, carry = _ret_tables(heads, chunk, dk, dv)
    has_init = state0 is not None
    row = lambda b, c: b * nc + c
    whole = lambda shape: pl.BlockSpec(shape, lambda b, c: (0,) * len(shape))
    in_specs = [
        pl.BlockSpec(memory_space=pltpu.SMEM),
        pl.BlockSpec((chunk, heads * dk), lambda b, c: (row(b, c), 0)),
        pl.BlockSpec((chunk, heads * dk), lambda b, c: (row(b, c), 1)),
        pl.BlockSpec((chunk, heads * dv), lambda b, c: (row(b, c), vb)),
        pl.BlockSpec((chunk, heads * dv), lambda b, c: (row(b, c), gb)),
        whole((heads, chunk, chunk)), whole((heads, chunk, dv)), whole((heads, chunk, dk)),
        whole((1, dv)),
    ]
    args = [carry, rqk, rqk, main, main, decay, cross, kdec, gain.reshape(1, dv)]
    if has_init:
        in_specs.append(pl.BlockSpec((None, None, heads, dk, dv),
                                     lambda b, c: (layer, b, 0, 0, 0)))
        args.append(state0)
    aliases = {}
    if state_stack is not None:
        in_specs.append(pl.BlockSpec(memory_space=pl.ANY))
        args.append(state_stack)
        aliases = {len(args) - 1: 1}
    vmem = (2 * heads * chunk * (chunk + dv + dk) * 4 + 8 * chunk * heads * (dk + dv) * 2
            + 6 * heads * dk * dv * 4 + (8 << 20))
    ob, st = pl.pallas_call(
        functools.partial(_ret_kernel, has_init=has_init),
        grid=(batch, nc), in_specs=in_specs,
        out_specs=[pl.BlockSpec((chunk, heads * dv), lambda b, c: (row(b, c), 0)),
                   pl.BlockSpec((None, None, heads, dk, dv), lambda b, c: (layer, b, 0, 0, 0))],
        out_shape=[jax.ShapeDtypeStruct((n, heads * dv), BF16),
                   jax.ShapeDtypeStruct((depth, batch, heads, dk, dv), F32)],
        scratch_shapes=[pltpu.VMEM((heads, dk, dv), F32)],
        input_output_aliases=aliases,
        compiler_params=_params(vmem, 2), name="retention",
    )(*args)
    return ob, st


def _merge_kernel(oa_ref, ob_ref, wa_ref, wb_ref, ga_ref, gb_ref, *refs, w_scale):
    if w_scale is not None:
        wsrc_ref, o_ref, wdst_ref = refs
        col = lax.broadcasted_iota(jnp.int32, (1, wsrc_ref.shape[1]), 1)
        wdst_ref[...] = (wsrc_ref[...] * jnp.where(col < w_scale[0], w_scale[1], 1.0)).astype(BF16)
    else:
        (o_ref,) = refs
    a = jnp.dot(oa_ref[...], wa_ref[...], preferred_element_type=F32)
    b = jnp.dot(ob_ref[...], wb_ref[...], preferred_element_type=F32)
    ga = jax.nn.sigmoid(ga_ref[...].astype(F32))
    gb = jax.nn.sigmoid(gb_ref[...].astype(F32))
    o_ref[...] = (ga * a + gb * b).astype(BF16)


def _merge(oa, ob, main, c_ga, c_gb, w_a, w_b, layer, *, tm_pref=1024, tn_pref=512,
           cast_w_in=None):
    n, k = oa.shape
    d = w_a.shape[2]
    tm = _pick_tile(n, tm_pref, 16)
    tn = _pick_tile(d, tn_pref, V7X_LANES)
    assert c_ga % tn == 0 and c_gb % tn == 0
    ga0, gb0 = c_ga // tn, c_gb // tn
    n_j = d // tn
    in_specs = [pl.BlockSpec((tm, k), lambda i, j: (i, 0)),
                pl.BlockSpec((tm, k), lambda i, j: (i, 0)),
                pl.BlockSpec((None, k, tn), lambda i, j: (layer, 0, j)),
                pl.BlockSpec((None, k, tn), lambda i, j: (layer, 0, j)),
                pl.BlockSpec((tm, tn), lambda i, j: (i, ga0 + j)),
                pl.BlockSpec((tm, tn), lambda i, j: (i, gb0 + j))]
    args = [oa, ob, w_a, w_b, main, main]
    out_specs = [pl.BlockSpec((tm, tn), lambda i, j: (i, j))]
    out_shape = [jax.ShapeDtypeStruct((n, d), BF16)]
    vmem = 4 * tm * k * 2 + 4 * k * tn * 2 + 6 * tm * tn * 2 + 4 * tm * tn * 4 + (4 << 20)
    slices = None
    if cast_w_in is not None:
        slices = _cast_slices(cast_w_in[:1], cast_w_in[1], (n // tm) * n_j,
                              lambda i, j: i * n_j + j)
    if slices:
        in_specs += slices[0]
        args.append(cast_w_in[0])
        out_specs += slices[1]
        out_shape += slices[2]
        vmem += slices[3]
    outs = pl.pallas_call(
        functools.partial(_merge_kernel, w_scale=cast_w_in[2:] if slices else None),
        grid=(n // tm, n_j), in_specs=in_specs, out_specs=out_specs, out_shape=out_shape,
        compiler_params=_params(vmem, 2), name="branch_merge",
    )(*args)
    if cast_w_in is None:
        return outs[0], None
    if slices:
        return outs[0], outs[1]
    w, w_layer, q_cols, q_scale = cast_w_in
    return outs[0], _cast_w_in(w[w_layer:w_layer + 1], q_cols, q_scale)[0]


def _ffn_kernel(*refs, row_chunk, final_norm):
    refs = list(refs)
    x_ref, g_ref, wu_ref, wd_ref = refs[:4]
    fg_ref = refs[4] if final_norm else None
    o_ref, h_scr = refs[-2:]
    f = pl.program_id(1)
    n_chunks = x_ref.shape[0] // row_chunk

    @pl.when(f == 0)
    def _():
        def rows(r, carry):
            rs = pl.ds(pl.multiple_of(r * row_chunk, row_chunk), row_chunk)
            x = x_ref[rs, :]
            h_scr[rs, :] = _rms_rows(x, g_ref[...]).astype(BF16)
            o_ref[rs, :] = x
            return carry
        lax.fori_loop(0, n_chunks, rows, 0, unroll=2)

    u = jnp.dot(h_scr[...], wu_ref[...], preferred_element_type=F32)
    u = jnp.square(jnp.maximum(u, 0.0)).astype(BF16)
    o_ref[...] += jnp.dot(u, wd_ref[...], preferred_element_type=F32)

    if final_norm:
        @pl.when(f == pl.num_programs(1) - 1)
        def _():
            def rows(r, carry):
                rs = pl.ds(pl.multiple_of(r * row_chunk, row_chunk), row_chunk)
                o_ref[rs, :] = _rms_rows(o_ref[rs, :], fg_ref[...])
                return carry
            lax.fori_loop(0, n_chunks, rows, 0)


def _ffn(x, gain, w_up, w_down, layer, final_gain=None, *, tm_pref=1024, tf_pref=512):
    n, d = x.shape
    d_ff = w_up.shape[2]
    tm = _pick_tile(n, tm_pref, 16)
    tf = _pick_tile(d_ff, tf_pref, V7X_LANES)
    final_norm = final_gain is not None
    in_specs = [pl.BlockSpec((tm, d), lambda i, f: (i, 0)),
                pl.BlockSpec((1, d), lambda i, f: (0, 0)),
                pl.BlockSpec((None, d, tf), lambda i, f: (layer, 0, f)),
                pl.BlockSpec((None, tf, d), lambda i, f: (layer, f, 0))]
    args = [x, gain.reshape(1, d), w_up, w_down]
    if final_norm:
        in_specs.append(pl.BlockSpec((1, d), lambda i, f: (0, 0)))
        args.append(final_gain.reshape(1, d))
    vmem = 4 * tm * d * 4 + tm * d * 2 + 4 * d * tf * 2 + 2 * tm * tf * 4 + (6 << 20)
    return pl.pallas_call(
        functools.partial(_ffn_kernel, row_chunk=min(tm, 64), final_norm=final_norm),
        grid=(n // tm, d_ff // tf), in_specs=in_specs,
        out_specs=pl.BlockSpec((tm, d), lambda i, f: (i, 0)),
        out_shape=jax.ShapeDtypeStruct((n, d), F32),
        scratch_shapes=[pltpu.VMEM((tm, d), BF16)],
        compiler_params=_params(vmem, 2), name="ffn",
    )(*args)


def _rope_tables(pos, dk, batch):
    inv = 1.0 / (ROPE_BASE ** jnp.linspace(0.0, 1.0, dk // 2, dtype=F32))
    th = pos.astype(F32)[:, None] * jnp.repeat(inv, 2)[None, :]
    sin, cos = jnp.sin(th), jnp.cos(th)
    even = (jnp.arange(dk) % 2) == 0
    sin_next = jnp.where(even, -sin, 0.0)
    sin_prev = jnp.where(even, 0.0, sin)
    return tuple(jnp.tile(t, (batch, 1)) for t in (cos, sin_next, sin_prev))


def _cast_w_in(w_in, diff_w, q_scale):
    col_scale = jnp.where(jnp.arange(w_in.shape[-1]) < diff_w, q_scale, 1.0).astype(F32)
    return (w_in * col_scale).astype(BF16)


def _mixer(x, layer, w, w_in, p, rope, past, kv_stack, side, *, depth, batch, seq, heads,
           ret_heads):
    d = x.shape[1]
    hd = p["subln_g"].shape[-1]
    diff_w = heads * hd
    dk = p["state"].shape[-2]
    dv = p["state"].shape[-1]
    qk_w = ret_heads * dk
    ret_w = ret_heads * dv
    c_k, c_v, c_rqk = diff_w, 2 * diff_w, 3 * diff_w
    skipped = 2 * diff_w + 2 * qk_w
    c_rv, c_rg, c_ga, c_gb = diff_w, diff_w + ret_w, diff_w + 2 * ret_w, diff_w + 2 * ret_w + d
    main_w = c_gb + d

    main, h = _proj(x, *w_in, 0, main_w, gain=p["norm_mix_g"], emit_h=True,
                    gap=(diff_w, skipped))
    k32, k16, cast_k = _proj_heads(h, *w_in, c_k, heads, kv_stack[0], depth, layer,
                                   cast=() if past else (side["up"], side["a"], side["b"]))
    v32, v16, cast_v = _proj_heads(h, *w_in, c_v, heads, kv_stack[1], depth, layer,
                                   cast=() if past else (side["down"], side["out"]))
    if not past:
        w = dict(zip(("up", "a", "b", "down", "out"), cast_k + cast_v))
    rqk = _proj(h, *w_in, c_rqk, 2 * qk_w, epi="rope", extra=rope,
                rope_scale=dk ** -0.5, tn_pref=qk_w)

    lam_init = 0.8 - 0.6 * math.exp(-0.3 * layer)
    lams = (p["lq1"], p["lk1"], p["lq2"], p["lk2"])
    cast_w_in = None
    if past:
        oa = _attn_decode(main, k16, v16, p["cache_k"], p["cache_v"], layer, p["subln_g"],
                          lam_init, lams, batch=batch, seq=seq, heads=heads)
        state0 = p["state"]
    else:
        if layer + 1 < depth:
            cast_w_in = (side["in"], layer + 1, diff_w, side["q_scale"])
        oa = _attn_prompt(main, k16, v16, p["subln_g"], lam_init, lams, batch=batch,
                          seq=seq, heads=heads)
        state0 = None
    ob, st = _retention(rqk, main, c_rv, c_rg, dv, p["ret_g"], state0, layer, kv_stack[2], depth,
                        batch=batch, seq=seq, heads=ret_heads)
    merged, w_in_next = _merge(oa, ob, main, c_ga, c_gb, w["a"][None], w["b"][None], 0,
                               cast_w_in=cast_w_in)
    x = _proj(merged, w["out"][None], 0, 0, d, epi="residual", extra=(x,), tm_pref=512,
              tn_pref=d)
    return x, (k32, v32, st), w_in_next, w


def kernel(x_prompt, x_sample, cache_diff_k, cache_diff_v, state_ret, norm_mix_g, w_in,
           lambda_q1, lambda_k1, lambda_q2, lambda_k2, diff_subln_g, ret_norm_g,
           w_branch_a, w_branch_b, w_out, norm_ffn_g, w_up, w_down, final_norm_g):
    depth, dec_batch, past_len, heads, hd = cache_diff_k.shape
    batch, seq, d = x_prompt.shape
    dec_seq = x_sample.shape[1]
    ret_heads, dk, dv = state_ret.shape[2:]

    q_scale = (hd // 2) ** -0.5 * LOG2E
    side = {"in": w_in, "up": w_up, "down": w_down, "a": w_branch_a, "b": w_branch_b,
            "out": w_out, "q_scale": q_scale}
    w_in_cur = _cast_w_in(w_in[0:1], heads * hd, q_scale)
    cache_k, cache_v = cache_diff_k, cache_diff_v
    rope_p = _rope_tables(jnp.arange(seq), dk, batch)
    rope_s = _rope_tables(past_len + jnp.arange(dec_seq), dk, dec_batch)

    xp = x_prompt.reshape(batch * seq, d)
    xs = x_sample.reshape(dec_batch * dec_seq, d)
    kv_p, kv_s = (None,) * 3, (None,) * 3
    for l in range(depth):
        p = {"norm_mix_g": norm_mix_g[l], "lq1": lambda_q1[l], "lk1": lambda_k1[l],
             "lq2": lambda_q2[l], "lk2": lambda_k2[l], "subln_g": diff_subln_g[l],
             "ret_g": ret_norm_g[l], "cache_k": cache_k, "cache_v": cache_v, "state": state_ret}
        final_g = final_norm_g if l == depth - 1 else None
        xp, kv_p, w_in_next, w = _mixer(
            xp, l, None, (w_in_cur, 0), p, rope_p, False, kv_p, side, depth=depth, batch=batch,
            seq=seq, heads=heads, ret_heads=ret_heads)
        xs, kv_s, _, _ = _mixer(
            xs, l, w, (w_in_cur, 0), p, rope_s, True, kv_s, None, depth=depth, batch=dec_batch,
            seq=dec_seq, heads=heads, ret_heads=ret_heads)
        xp = _ffn(xp, norm_ffn_g[l], w["up"][None], w["down"][None], 0, final_g)
        xs = _ffn(xs, norm_ffn_g[l], w["up"][None], w["down"][None], 0, final_g)
        if w_in_next is not None:
            w_in_cur = w_in_next[None]
    return (xp.reshape(batch, seq, d), xs.reshape(dec_batch, dec_seq, d),
            kv_p[0].reshape(depth, batch, seq, heads, hd),
            kv_p[1].reshape(depth, batch, seq, heads, hd), kv_p[2],
            kv_s[0].reshape(depth, dec_batch, dec_seq, heads, hd),
            kv_s[1].reshape(depth, dec_batch, dec_seq, heads, hd), kv_s[2])
```

```python
import functools
import math

import jax
import jax.numpy as jnp
from jax import lax
from jax.experimental import pallas as pl
from jax.experimental.pallas import tpu as pltpu

F32 = jnp.float32
BF16 = jnp.bfloat16

NORM_EPS = 1e-6
ROPE_BASE = 10000.0
MASK_CHUNK = 64
NEG_INF = -1e30
SHIFT_LIMIT = 48.0
NORM_SLACK = 1.02
FAST_SWEEP_GROUP = 12
LOG2E = 1.4426950408889634

V7X_VMEM_BYTES = 64 * 1024 * 1024
V7X_LANES = 128
VMEM_REQUEST_CAP = V7X_VMEM_BYTES - 8 * 1024 * 1024

NT_DIMS = (((1,), (1,)), ((), ()))
TN_DIMS = (((0,), (0,)), ((), ()))


def _pick_tile(n, pref, align):
    if n <= pref:
        return n
    t = pref - pref % align
    while t >= align:
        if n % t == 0:
            return t
        t -= align
    raise ValueError(f"no tile for {n} (pref {pref}, align {align})")


def _params(vmem_bytes, n_grid):
    return pltpu.CompilerParams(
        dimension_semantics=("arbitrary",) * n_grid,
        vmem_limit_bytes=int(min(VMEM_REQUEST_CAP, vmem_bytes)))


def _rms_rows(x, gain):
    ms = jnp.mean(x * x, axis=-1, keepdims=True)
    return x * lax.rsqrt(ms + NORM_EPS) * gain


def _proj_kernel(*refs, has_norm, emit_h, epi, row_chunk, rope_scale):
    refs = list(refs)
    x_ref = refs.pop(0)
    g_ref = refs.pop(0) if has_norm else None
    w_ref = refs.pop(0)
    if epi == "rope":
        cos_ref, sina_ref, sinb_ref = refs.pop(0), refs.pop(0), refs.pop(0)
    if epi == "residual":
        res_ref = refs.pop(0)
    o_ref = refs.pop(0)
    hout_ref = refs.pop(0) if emit_h else None
    h_scr = refs.pop(0) if has_norm else None

    j = pl.program_id(1)
    if has_norm:
        @pl.when(j == 0)
        def _():
            def rows(r, carry):
                rs = pl.ds(pl.multiple_of(r * row_chunk, row_chunk), row_chunk)
                hrow = _rms_rows(x_ref[rs, :], g_ref[...]).astype(BF16)
                h_scr[rs, :] = hrow
                if emit_h:
                    hout_ref[rs, :] = hrow
                return carry
            lax.fori_loop(0, x_ref.shape[0] // row_chunk, rows, 0, unroll=2)
        lhs = h_scr[...]
    else:
        lhs = x_ref[...]

    acc = jnp.dot(lhs, w_ref[...], preferred_element_type=F32)
    if epi == "bf16":
        o_ref[...] = acc.astype(BF16)
    elif epi == "residual":
        o_ref[...] = res_ref[...] + acc
    elif epi == "rope":
        n_tiles = pl.num_programs(1)
        scale = jnp.where(j < n_tiles // 2, 1.0, rope_scale).astype(F32)
        cos, sina, sinb = cos_ref[...], sina_ref[...], sinb_ref[...]
        for c in range(acc.shape[1] // V7X_LANES):
            cs = slice(c * V7X_LANES, (c + 1) * V7X_LANES)
            xs = acc[:, cs]
            nxt = pltpu.roll(xs, V7X_LANES - 1, 1)
            prv = pltpu.roll(xs, 1, 1)
            o_ref[:, cs] = ((xs * cos + nxt * sina + prv * sinb) * scale).astype(BF16)
    else:
        raise ValueError(epi)


def _proj(lhs, w, layer, col0, ncols, *, gain=None, emit_h=False, epi="bf16", extra=(),
          rope_scale=1.0, gap=(0, 0), tm_pref=1024, tn_pref=1024):
    n, k = lhs.shape
    has_norm = gain is not None
    tm = _pick_tile(n, tm_pref, 16)
    tn = _pick_tile(ncols, tn_pref, V7X_LANES)
    assert col0 % tn == 0 and gap[0] % tn == 0 and gap[1] % tn == 0
    cb = col0 // tn
    lead_tiles, skip_tiles = gap[0] // tn, gap[1] // tn
    grid = (n // tm, ncols // tn)

    def w_tile(j):
        return cb + j + jnp.where(j >= lead_tiles, skip_tiles, 0) if skip_tiles else cb + j

    in_specs = [pl.BlockSpec((tm, k), lambda i, j: (i, 0))]
    args = [lhs]
    if has_norm:
        in_specs.append(pl.BlockSpec((1, k), lambda i, j: (0, 0)))
        args.append(gain.reshape(1, k))
    in_specs.append(pl.BlockSpec((None, k, tn), lambda i, j: (layer, 0, w_tile(j))))
    args.append(w)
    if epi == "rope":
        for t in extra:
            in_specs.append(pl.BlockSpec((tm, V7X_LANES), lambda i, j: (i, 0)))
            args.append(t)
    elif epi == "residual":
        in_specs.append(pl.BlockSpec((tm, tn), lambda i, j: (i, j)))
        args.append(extra[0])

    tile_spec = pl.BlockSpec((tm, tn), lambda i, j: (i, j))
    out_dtype = F32 if epi == "residual" else BF16
    out_shape = [jax.ShapeDtypeStruct((n, ncols), out_dtype)]
    out_specs = [tile_spec]
    out_bytes = tm * tn * jnp.dtype(out_dtype).itemsize
    if emit_h:
        out_shape.append(jax.ShapeDtypeStruct((n, k), BF16))
        out_specs.append(pl.BlockSpec((tm, k), lambda i, j: (i, 0)))
        out_bytes += tm * k * 2
    scratch = [pltpu.VMEM((tm, k), BF16)] if has_norm else []

    vmem = (2 * tm * k * lhs.dtype.itemsize + 2 * k * tn * 2 + 2 * out_bytes
            + (tm * k * 2 if has_norm else 0) + 3 * tm * tn * 4
            + (2 * tm * tn * 4 if epi == "residual" else 0) + (4 << 20))
    outs = pl.pallas_call(
        functools.partial(_proj_kernel, has_norm=has_norm, emit_h=emit_h, epi=epi,
                          row_chunk=min(tm, 64), rope_scale=rope_scale),
        grid=grid, in_specs=in_specs, out_specs=out_specs, out_shape=out_shape,
        scratch_shapes=scratch, compiler_params=_params(vmem, 2),
        name=f"proj_{epi}")(*args)
    return outs[0] if len(outs) == 1 else tuple(outs)


def _proj_heads_kernel(*refs, aliased, n_cast):
    x_ref, w_ref = refs[0], refs[1]
    n_in = 2 + aliased + n_cast
    o32_ref, o16_ref = refs[n_in], refs[n_in + 1]
    for c in range(n_cast):
        refs[n_in + 2 + c][...] = refs[n_in - n_cast + c][...].astype(BF16)
    acc = jnp.dot(x_ref[...], w_ref[...], preferred_element_type=F32)
    o16_ref[...] = acc.astype(BF16)
    o32_ref[...] = acc.reshape(o32_ref.shape)


def _proj_heads(lhs, w, w_layer, col0, heads, stacked, depth, layer, *, tm_pref=512, cast=()):
    n, k = lhs.shape
    ncols = k
    hd = ncols // heads
    tm = _pick_tile(n, tm_pref, 16)
    cb = col0 // ncols
    assert col0 % ncols == 0
    aliased = stacked is not None
    in_specs = [pl.BlockSpec((tm, k), lambda i: (i, 0)),
                pl.BlockSpec((None, k, ncols), lambda i: (w_layer, 0, cb))]
    args = [lhs, w]
    if aliased:
        in_specs.append(pl.BlockSpec(memory_space=pl.ANY))
        args.append(stacked)
    vmem = 4 * tm * k * 2 + 4 * k * ncols * 2 + 4 * tm * ncols * 4 + 4 * tm * ncols * 2 + (4 << 20)
    out_specs = [pl.BlockSpec((None, tm, heads, hd), lambda i: (layer, i, 0, 0)),
                 pl.BlockSpec((tm, ncols), lambda i: (i, 0))]
    out_shape = [jax.ShapeDtypeStruct((depth, n, heads, hd), F32),
                 jax.ShapeDtypeStruct((n, ncols), BF16)]
    slices = _cast_slices(cast, layer, n // tm, lambda i: i) if cast else None
    if slices:
        in_specs += slices[0]
        args += list(cast)
        out_specs += slices[1]
        out_shape += slices[2]
        vmem += slices[3]
    outs = pl.pallas_call(
        functools.partial(_proj_heads_kernel, aliased=aliased, n_cast=len(cast) if slices else 0),
        grid=(n // tm,), in_specs=in_specs, out_specs=out_specs, out_shape=out_shape,
        input_output_aliases={2: 0} if aliased else {},
        compiler_params=_params(vmem, 1), name="proj_heads")(*args)
    casted = tuple(outs[2:]) if slices else tuple(c[layer].astype(BF16) for c in cast)
    return outs[0], outs[1], casted


def _lam_value(lamc_ref, lq1_ref, lk1_ref, lq2_ref, lk2_ref):
    a = jnp.exp(jnp.sum(lq1_ref[...] * lk1_ref[...], axis=-1, keepdims=True))
    b = jnp.exp(jnp.sum(lq2_ref[...] * lk2_ref[...], axis=-1, keepdims=True))
    return a - b + lamc_ref[0]


def _lane_repeat(x, reps):
    return x if reps == 1 else jnp.concatenate([x] * reps, axis=1)


def _softmax_update(s, v, m_prev, l_prev, acc_prev):
    keys = s.shape[1]
    m_new = jnp.maximum(m_prev, jnp.max(s, axis=1, keepdims=True))
    alpha = jnp.exp2(m_prev - m_new)
    p = jnp.exp2(s - _lane_repeat(m_new, keys // V7X_LANES))
    l_new = alpha * l_prev + jnp.sum(p, axis=1, keepdims=True)
    pv = jnp.dot(p.astype(BF16), v, preferred_element_type=F32)
    acc_new = acc_prev * _lane_repeat(alpha, acc_prev.shape[1] // V7X_LANES) + pv
    return m_new, l_new, acc_new


def _softmax_step(s, v, m_ref, l_ref, acc_ref):
    m_ref[...], l_ref[...], acc_ref[...] = _softmax_update(s, v, m_ref[...], l_ref[...],
                                                           acc_ref[...])


def _diff_finish(a1, l1, a2, l2, lam, gain, out_scale):
    rep = a1.shape[1] // V7X_LANES
    o = a1 * _lane_repeat(1.0 / l1, rep) - a2 * _lane_repeat(lam / l2, rep)
    return _rms_rows(o, gain * out_scale)


def _attn_prompt_kernel(lamc_ref, lq1_ref, lk1_ref, lq2_ref, lk2_ref, q_ref, k_ref, v_ref, g_ref,
                        o_ref, m1, l1, a1, m2, l2, a2, sa1, sa2, sb1, sb2, ka1, ka2, ext, *, dh, tq):
    seq = q_ref.shape[0]
    tk = tq
    nq = seq // tq
    n_units = nq * (nq + 1) // 2
    shift = MASK_CHUNK.bit_length() - 1
    assert tq // MASK_CHUNK < dh
    maps = ((0, m1, l1, a1, ka1), (dh, m2, l2, a2, ka2))

    lane = lax.broadcasted_iota(jnp.int32, (tk, dh), 1)
    row_chunk = lax.shift_right_logical(lax.broadcasted_iota(jnp.int32, (tk, dh), 0), shift)
    key_ext = jnp.where(lane == 0, -1.0, jnp.where(lane - 1 == row_chunk, 1.0, 0.0)).astype(BF16)
    hidden = (lane >= 1) & (lane <= tq // MASK_CHUNK) & (lane - 1 > row_chunk)

    ones = jnp.ones((dh, dh), BF16)

    def widen(c, carry):
        rs = pl.ds(pl.multiple_of(c * tk, tk), tk)
        out = []
        for idx, (lo, _, l_ref, a_ref, ka) in enumerate(maps):
            l_ref[rs, :] = jnp.zeros((tk, V7X_LANES), F32)
            a_ref[rs, :] = jnp.zeros((tk, 2 * dh), F32)
            kc = k_ref[rs, lo:lo + dh]
            qc = q_ref[rs, lo:lo + dh]
            ka[rs, 0:dh] = kc
            ka[rs, dh:2 * dh] = key_ext
            k2 = jnp.dot(kc * kc, ones, preferred_element_type=F32)
            q2 = jnp.dot(qc * qc, ones, preferred_element_type=F32)
            out += [jnp.maximum(carry[2 * idx], jnp.max(k2, axis=0, keepdims=True)),
                    jnp.maximum(carry[2 * idx + 1], jnp.max(q2, axis=0, keepdims=True))]
        return tuple(out)
    norms2 = lax.fori_loop(0, nq, widen, (jnp.zeros((1, dh), F32),) * 4, unroll=True)

    worst = jnp.zeros((1, 1), F32)
    for idx in range(2):
        bound = jnp.sqrt(norms2[2 * idx][:, 0:1] * norms2[2 * idx + 1][:, 0:1]) * NORM_SLACK
        worst = jnp.maximum(worst, bound)
        base = jnp.where(lane == 0, bound, 0.0)
        ext[(2 * idx) * tq:(2 * idx + 1) * tq, :] = base.astype(BF16)
        ext[(2 * idx + 1) * tq:(2 * idx + 2) * tq, :] = jnp.where(hidden, NEG_INF, base).astype(BF16)
    fast = jnp.max(worst) <= SHIFT_LIMIT

    lam = _lam_value(lamc_ref, lq1_ref, lk1_ref, lq2_ref, lk2_ref)
    gain, out_scale = g_ref[...], lamc_ref[1]

    def q_rows(qi):
        return pl.ds(pl.multiple_of(qi * tq, tq), tq)

    def k_rows(j):
        return pl.ds(pl.multiple_of(j * tk, tk), tk)

    def scores(qi, j, bufs):
        diag = jnp.where(qi == j, 1, 0)
        for idx, ((lo, _, _, _, ka), s_scr) in enumerate(zip(maps, bufs)):
            e_rows = pl.ds(pl.multiple_of((2 * idx + diag) * tq, tq), tq)
            qa = jnp.concatenate([q_ref[q_rows(qi), lo:lo + dh], ext[e_rows, :]], axis=1)
            s_scr[...] = lax.dot_general(qa, ka[k_rows(j), :], NT_DIMS,
                                         preferred_element_type=F32)

    def consume(qi, j, bufs, exact):
        v = v_ref[k_rows(j), :]
        rq = q_rows(qi)
        for (_, m_ref, l_ref, a_ref, _), s_scr in zip(maps, bufs):
            s = s_scr[...]
            l_prev, a_prev = l_ref[rq, :], a_ref[rq, :]
            if exact:
                m_ref[rq, :], l_new, a_new = _softmax_update(s, v, m_ref[rq, :], l_prev, a_prev)
            else:
                p = jnp.exp2(s)
                l_new = l_prev + jnp.sum(p, axis=1, keepdims=True)
                a_new = a_prev + jnp.dot(p.astype(BF16), v, preferred_element_type=F32)
            l_ref[rq, :], a_ref[rq, :] = l_new, a_new

    def advance(qi, j):
        last = j == qi
        return jnp.where(last, qi + 1, qi), jnp.where(last, 0, j + 1)

    buf_a, buf_b = (sa1, sa2), (sb1, sb2)

    units = [(qi, j) for qi in range(nq) for j in range(qi + 1)]

    def sweep(exact, group):
        if exact:
            for m_ref in (m1, m2):
                m_ref[...] = jnp.full(m_ref.shape, -jnp.inf, F32)
        scores(0, 0, buf_a)

        def trip(t, carry):
            cur = carry
            for g in range(group):
                nxt = advance(*cur)
                over = nxt[0] >= nq
                safe = (jnp.where(over, nq - 1, nxt[0]), jnp.where(over, 0, nxt[1]))
                mine, other = (buf_a, buf_b) if g % 2 == 0 else (buf_b, buf_a)
                scores(*safe, other)
                consume(*cur, mine, exact)
                cur = nxt
            return cur
        n_trips = n_units // group
        lax.fori_loop(0, n_trips, trip, (jnp.int32(0), jnp.int32(0)))
        rest = units[n_trips * group:]
        for g, (qi, j) in enumerate(rest):
            mine, other = (buf_a, buf_b) if g % 2 == 0 else (buf_b, buf_a)
            if g + 1 < len(rest):
                scores(*rest[g + 1], other)
            consume(qi, j, mine, exact)

    @pl.when(fast)
    def _():
        sweep(False, FAST_SWEEP_GROUP)

    @pl.when(jnp.logical_not(fast))
    def _():
        sweep(True, 2)

    def finish(qi, carry):
        rq = q_rows(qi)
        o = _diff_finish(a1[rq, :], l1[rq, :], a2[rq, :], l2[rq, :], lam, gain, out_scale)
        o_ref[rq, :] = o.astype(BF16)
        return carry
    lax.fori_loop(0, nq, finish, 0)


def _lam_inputs(lam_init, lams):
    specs = [pl.BlockSpec(memory_space=pltpu.SMEM)]
    args = [jnp.array([lam_init, 1.0 - lam_init], F32)]
    for v in lams:
        specs.append(pl.BlockSpec((1, v.shape[-1]), lambda *_: (0, 0)))
        args.append(v.reshape(1, -1))
    return specs, args


def _attn_prompt(dq, k16, v16, gain, lam_init, lams, *, batch, seq, heads, tq_pref=512):
    n, width = k16.shape
    hd = width // heads
    dh = hd // 2
    tq = _pick_tile(seq, tq_pref, V7X_LANES)
    lam_specs, lam_args = _lam_inputs(lam_init, lams)
    per_head = pl.BlockSpec((seq, hd), lambda b, h: (b, h))
    in_specs = lam_specs + [per_head, per_head, per_head,
                            pl.BlockSpec((1, hd), lambda b, h: (0, 0))]
    stat = pltpu.VMEM((seq, V7X_LANES), F32)
    accs = pltpu.VMEM((seq, hd), F32)
    sbuf = pltpu.VMEM((tq, tq), F32)
    wide_k = pltpu.VMEM((seq, hd), BF16)
    vmem = 8 * seq * hd * 2 + 2 * seq * hd * 2 + 2 * seq * hd * 4 + 4 * seq * V7X_LANES * 4 \
        + 12 * tq * tq * 4 + (4 << 20)
    return pl.pallas_call(
        functools.partial(_attn_prompt_kernel, dh=dh, tq=tq),
        grid=(batch, heads), in_specs=in_specs, out_specs=per_head,
        out_shape=jax.ShapeDtypeStruct((n, width), BF16),
        scratch_shapes=[stat, stat, accs, stat, stat, accs, sbuf, sbuf, sbuf, sbuf,
                        wide_k, wide_k, pltpu.VMEM((4 * tq, dh), BF16)],
        compiler_params=_params(vmem, 2), name="diff_attn_prompt",
    )(*lam_args, dq, k16, v16, gain.reshape(1, hd))


def _attn_decode_kernel(lamc_ref, lq1_ref, lk1_ref, lq2_ref, lk2_ref, q_ref, kn_ref, vn_ref,
                        kc_ref, vc_ref, g_ref, o_ref, qd, m_scr, l_scr, a_scr,
                        *, heads, dh, n_new):
    j = pl.program_id(1)
    hd = 2 * dh
    s_q = q_ref.shape[0]

    @pl.when(j == 0)
    def _():
        qd[...] = jnp.zeros(qd.shape, BF16)
        m_scr[...] = jnp.full(m_scr.shape, -jnp.inf, F32)
        l_scr[...] = jnp.zeros(l_scr.shape, F32)
        a_scr[...] = jnp.zeros(a_scr.shape, F32)
        new_mask = lax.broadcasted_iota(jnp.int32, (2 * s_q, kn_ref.shape[0]), 1) < n_new
        for h in range(heads):
            hs = slice(h * hd, (h + 1) * hd)
            qd[h, 0:s_q, 0:dh] = q_ref[:, h * hd:h * hd + dh]
            qd[h, s_q:2 * s_q, dh:hd] = q_ref[:, h * hd + dh:(h + 1) * hd]
            s = lax.dot_general(qd[h], kn_ref[:, hs], NT_DIMS, preferred_element_type=F32)
            s = jnp.where(new_mask, s, NEG_INF)
            _softmax_step(s, vn_ref[:, hs], m_scr.at[h], l_scr.at[h], a_scr.at[h])

    tk = kc_ref.shape[0]
    k_all = kc_ref[...].astype(BF16).reshape(tk, heads * hd)
    v_all = vc_ref[...].astype(BF16).reshape(tk, heads * hd)
    scores = [lax.dot_general(qd[h], k_all[:, h * hd:(h + 1) * hd], NT_DIMS,
                              preferred_element_type=F32) for h in range(heads)]
    for h in range(heads):
        _softmax_step(scores[h], v_all[:, h * hd:(h + 1) * hd], m_scr.at[h], l_scr.at[h],
                      a_scr.at[h])

    @pl.when(j == pl.num_programs(1) - 1)
    def _():
        lam = _lam_value(lamc_ref, lq1_ref, lk1_ref, lq2_ref, lk2_ref)
        for h in range(heads):
            a, l = a_scr[h], l_scr[h]
            o = _diff_finish(a[:s_q], l[:s_q], a[s_q:], l[s_q:], lam, g_ref[...], lamc_ref[1])
            o_ref[:, h * hd:(h + 1) * hd] = o.astype(BF16)


def _attn_decode(dq, k16, v16, cache_k, cache_v, layer, gain, lam_init, lams, *, batch, seq,
                 heads, tk_pref=1024):
    n, width = k16.shape
    hd = width // heads
    dh = hd // 2
    past = cache_k.shape[2]
    tk = _pick_tile(past, tk_pref, V7X_LANES)
    pad = V7X_LANES
    kn = jnp.pad(k16.reshape(batch, seq, width), ((0, 0), (0, pad - seq), (0, 0)))
    vn = jnp.pad(v16.reshape(batch, seq, width), ((0, 0), (0, pad - seq), (0, 0)))
    lam_specs, lam_args = _lam_inputs(lam_init, lams)
    in_specs = lam_specs + [
        pl.BlockSpec((seq, width), lambda b, j: (b, 0)),
        pl.BlockSpec((None, pad, width), lambda b, j: (b, 0, 0)),
        pl.BlockSpec((None, pad, width), lambda b, j: (b, 0, 0)),
        pl.BlockSpec((None, None, tk, heads, hd), lambda b, j: (layer, b, j, 0, 0)),
        pl.BlockSpec((None, None, tk, heads, hd), lambda b, j: (layer, b, j, 0, 0)),
        pl.BlockSpec((1, hd), lambda b, j: (0, 0)),
    ]
    scratch = [pltpu.VMEM((heads, 2 * seq, hd), BF16),
               pltpu.VMEM((heads, 2 * seq, V7X_LANES), F32),
               pltpu.VMEM((heads, 2 * seq, V7X_LANES), F32),
               pltpu.VMEM((heads, 2 * seq, hd), F32)]
    vmem = 4 * tk * width * 4 + 2 * tk * width * 2 + 8 * pad * width * 2 + (8 << 20)
    return pl.pallas_call(
        functools.partial(_attn_decode_kernel, heads=heads, dh=dh, n_new=seq),
        grid=(batch, past // tk), in_specs=in_specs,
        out_specs=pl.BlockSpec((seq, width), lambda b, j: (b, 0)),
        out_shape=jax.ShapeDtypeStruct((n, width), BF16),
        scratch_shapes=scratch, compiler_params=_params(vmem, 2), name="diff_attn_decode",
    )(*lam_args, dq, kn, vn, cache_k, cache_v, gain.reshape(1, hd))


def _ret_tables(heads, chunk, dk, dv):
    log_gamma = jnp.log1p(-jnp.exp2(-5.0 - jnp.arange(heads, dtype=F32)))
    idx = jnp.arange(chunk, dtype=F32)
    diff = idx[:, None] - idx[None, :]
    decay = jnp.where(diff >= 0, jnp.exp(log_gamma[:, None, None] * jnp.maximum(diff, 0.0)), 0.0)
    cross = jnp.exp(log_gamma[:, None] * (idx + 1.0)[None, :])
    kdec = jnp.exp(log_gamma[:, None] * (chunk - 1.0 - idx)[None, :])
    carry = jnp.exp(log_gamma * chunk)
    return (decay,
            jnp.broadcast_to(cross[:, :, None], (heads, chunk, dv)),
            jnp.broadcast_to(kdec[:, :, None], (heads, chunk, dk)),
            carry)


def _ret_kernel(*refs, has_init):
    refs = list(refs)
    carry_ref, q_ref, k_ref, v_ref, rg_ref, dec_ref, cross_ref, kdec_ref, g_ref = refs[:9]
    refs = refs[9:]
    s0_ref = refs.pop(0) if has_init else None
    o_ref, sout_ref, s_scr = refs[-3:]
    heads, dk, dv = s_scr.shape

    @pl.when(pl.program_id(1) == 0)
    def _():
        s_scr[...] = s0_ref[...] if has_init else jnp.zeros(s_scr.shape, F32)

    for h in range(heads):
        ks, vs = slice(h * dk, (h + 1) * dk), slice(h * dv, (h + 1) * dv)
        q, k, v = q_ref[:, ks], k_ref[:, ks], v_ref[:, vs]
        state = s_scr[h]
        scores = lax.dot_general(q, k, NT_DIMS, preferred_element_type=F32) * dec_ref[h]
        inner = jnp.dot(scores.astype(BF16), v, preferred_element_type=F32)
        cross = jnp.dot(q, state.astype(BF16), preferred_element_type=F32) * cross_ref[h]
        k_dec = (k.astype(F32) * kdec_ref[h]).astype(BF16)
        new_state = carry_ref[h] * state + lax.dot_general(k_dec, v, TN_DIMS,
                                                           preferred_element_type=F32)
        s_scr[h] = new_state
        sout_ref[h] = new_state
        gate = rg_ref[:, vs].astype(F32)
        y = _rms_rows(inner + cross, g_ref[...]) * (gate * jax.nn.sigmoid(gate))
        o_ref[:, vs] = y.astype(BF16)


def _cast_slices(weights, layer, n_steps, step_of):
    in_specs, out_specs, out_shapes, vmem = [], [], [], 0
    for w in weights:
        rows, cols = w.shape[1:]
        if rows % (16 * n_steps):
            return None
        r = rows // n_steps
        in_specs.append(pl.BlockSpec((None, r, cols), lambda *g: (layer, step_of(*g), 0)))
        out_specs.append(pl.BlockSpec((r, cols), lambda *g: (step_of(*g), 0)))
        out_shapes.append(jax.ShapeDtypeStruct((rows, cols), BF16))
        vmem += 2 * r * cols * (4 + 2)
    return in_specs, out_specs, out_shapes, vmem


def _retention(rqk, main, c_v, c_gate, dv, gain, state0, layer, state_stack, depth, *, batch, seq,
               heads, chunk_pref=256):
    n = rqk.shape[0]
    dk = rqk.shape[1] // (2 * heads)
    assert c_v % (heads * dv) == 0 and c_gate % (heads * dv) == 0
    vb, gb = c_v // (heads * dv), c_gate // (heads * dv)
    chunk = _pick_tile(seq, chunk_pref, 16)
    nc = seq // chunk
    decay, cross, kdec, carry = _ret_tables(heads, chunk, dk, dv)
    has_init = state0 is not None
    row = lambda b, c: b * nc + c
    whole = lambda shape: pl.BlockSpec(shape, lambda b, c: (0,) * len(shape))
    in_specs = [
        pl.BlockSpec(memory_space=pltpu.SMEM),
        pl.BlockSpec((chunk, heads * dk), lambda b, c: (row(b, c), 0)),
        pl.BlockSpec((chunk, heads * dk), lambda b, c: (row(b, c), 1)),
        pl.BlockSpec((chunk, heads * dv), lambda b, c: (row(b, c), vb)),
        pl.BlockSpec((chunk, heads * dv), lambda b, c: (row(b, c), gb)),
        whole((heads, chunk, chunk)), whole((heads, chunk, dv)), whole((heads, chunk, dk)),
        whole((1, dv)),
    ]
    args = [carry, rqk, rqk, main, main, decay, cross, kdec, gain.reshape(1, dv)]
    if has_init:
        in_specs.append(pl.BlockSpec((None, None, heads, dk, dv),
                                     lambda b, c: (layer, b, 0, 0, 0)))
        args.append(state0)
    aliases = {}
    if state_stack is not None:
        in_specs.append(pl.BlockSpec(memory_space=pl.ANY))
        args.append(state_stack)
        aliases = {len(args) - 1: 1}
    vmem = (2 * heads * chunk * (chunk + dv + dk) * 4 + 8 * chunk * heads * (dk + dv) * 2
            + 6 * heads * dk * dv * 4 + (8 << 20))
    ob, st = pl.pallas_call(
        functools.partial(_ret_kernel, has_init=has_init),
        grid=(batch, nc), in_specs=in_specs,
        out_specs=[pl.BlockSpec((chunk, heads * dv), lambda b, c: (row(b, c), 0)),
                   pl.BlockSpec((None, None, heads, dk, dv), lambda b, c: (layer, b, 0, 0, 0))],
        out_shape=[jax.ShapeDtypeStruct((n, heads * dv), BF16),
                   jax.ShapeDtypeStruct((depth, batch, heads, dk, dv), F32)],
        scratch_shapes=[pltpu.VMEM((heads, dk, dv), F32)],
        input_output_aliases=aliases,
        compiler_params=_params(vmem, 2), name="retention",
    )(*args)
    return ob, st


def _merge_kernel(oa_ref, ob_ref, wa_ref, wb_ref, ga_ref, gb_ref, *refs, w_scale):
    if w_scale is not None:
        wsrc_ref, o_ref, wdst_ref = refs
        col = lax.broadcasted_iota(jnp.int32, (1, wsrc_ref.shape[1]), 1)
        wdst_ref[...] = (wsrc_ref[...] * jnp.where(col < w_scale[0], w_scale[1], 1.0)).astype(BF16)
    else:
        (o_ref,) = refs
    a = jnp.dot(oa_ref[...], wa_ref[...], preferred_element_type=F32)
    b = jnp.dot(ob_ref[...], wb_ref[...], preferred_element_type=F32)
    ga = jax.nn.sigmoid(ga_ref[...].astype(F32))
    gb = jax.nn.sigmoid(gb_ref[...].astype(F32))
    o_ref[...] = (ga * a + gb * b).astype(BF16)


def _merge(oa, ob, main, c_ga, c_gb, w_a, w_b, layer, *, tm_pref=1024, tn_pref=512,
           cast_w_in=None):
    n, k = oa.shape
    d = w_a.shape[2]
    tm = _pick_tile(n, tm_pref, 16)
    tn = _pick_tile(d, tn_pref, V7X_LANES)
    assert c_ga % tn == 0 and c_gb % tn == 0
    ga0, gb0 = c_ga // tn, c_gb // tn
    n_j = d // tn
    in_specs = [pl.BlockSpec((tm, k), lambda i, j: (i, 0)),
                pl.BlockSpec((tm, k), lambda i, j: (i, 0)),
                pl.BlockSpec((None, k, tn), lambda i, j: (layer, 0, j)),
                pl.BlockSpec((None, k, tn), lambda i, j: (layer, 0, j)),
                pl.BlockSpec((tm, tn), lambda i, j: (i, ga0 + j)),
                pl.BlockSpec((tm, tn), lambda i, j: (i, gb0 + j))]
    args = [oa, ob, w_a, w_b, main, main]
    out_specs = [pl.BlockSpec((tm, tn), lambda i, j: (i, j))]
    out_shape = [jax.ShapeDtypeStruct((n, d), BF16)]
    vmem = 4 * tm * k * 2 + 4 * k * tn * 2 + 6 * tm * tn * 2 + 4 * tm * tn * 4 + (4 << 20)
    slices = None
    if cast_w_in is not None:
        slices = _cast_slices(cast_w_in[:1], cast_w_in[1], (n // tm) * n_j,
                              lambda i, j: i * n_j + j)
    if slices:
        in_specs += slices[0]
        args.append(cast_w_in[0])
        out_specs += slices[1]
        out_shape += slices[2]
        vmem += slices[3]
    outs = pl.pallas_call(
        functools.partial(_merge_kernel, w_scale=cast_w_in[2:] if slices else None),
        grid=(n // tm, n_j), in_specs=in_specs, out_specs=out_specs, out_shape=out_shape,
        compiler_params=_params(vmem, 2), name="branch_merge",
    )(*args)
    if cast_w_in is None:
        return outs[0], None
    if slices:
        return outs[0], outs[1]
    w, w_layer, q_cols, q_scale = cast_w_in
    return outs[0], _cast_w_in(w[w_layer:w_layer + 1], q_cols, q_scale)[0]


def _ffn_kernel(*refs, row_chunk, final_norm):
    refs = list(refs)
    x_ref, g_ref, wu_ref, wd_ref = refs[:4]
    fg_ref = refs[4] if final_norm else None
    o_ref, h_scr = refs[-2:]
    f = pl.program_id(1)
    n_chunks = x_ref.shape[0] // row_chunk

    @pl.when(f == 0)
    def _():
        def rows(r, carry):
            rs = pl.ds(pl.multiple_of(r * row_chunk, row_chunk), row_chunk)
            x = x_ref[rs, :]
            h_scr[rs, :] = _rms_rows(x, g_ref[...]).astype(BF16)
            o_ref[rs, :] = x
            return carry
        lax.fori_loop(0, n_chunks, rows, 0, unroll=2)

    u = jnp.dot(h_scr[...], wu_ref[...], preferred_element_type=F32)
    u = jnp.square(jnp.maximum(u, 0.0)).astype(BF16)
    o_ref[...] += jnp.dot(u, wd_ref[...], preferred_element_type=F32)

    if final_norm:
        @pl.when(f == pl.num_programs(1) - 1)
        def _():
            def rows(r, carry):
                rs = pl.ds(pl.multiple_of(r * row_chunk, row_chunk), row_chunk)
                o_ref[rs, :] = _rms_rows(o_ref[rs, :], fg_ref[...])
                return carry
            lax.fori_loop(0, n_chunks, rows, 0)


def _ffn(x, gain, w_up, w_down, layer, final_gain=None, *, tm_pref=1024, tf_pref=512):
    n, d = x.shape
    d_ff = w_up.shape[2]
    tm = _pick_tile(n, tm_pref, 16)
    tf = _pick_tile(d_ff, tf_pref, V7X_LANES)
    final_norm = final_gain is not None
    in_specs = [pl.BlockSpec((tm, d), lambda i, f: (i, 0)),
                pl.BlockSpec((1, d), lambda i, f: (0, 0)),
                pl.BlockSpec((None, d, tf), lambda i, f: (layer, 0, f)),
                pl.BlockSpec((None, tf, d), lambda i, f: (layer, f, 0))]
    args = [x, gain.reshape(1, d), w_up, w_down]
    if final_norm:
        in_specs.append(pl.BlockSpec((1, d), lambda i, f: (0, 0)))
        args.append(final_gain.reshape(1, d))
    vmem = 4 * tm * d * 4 + tm * d * 2 + 4 * d * tf * 2 + 2 * tm * tf * 4 + (6 << 20)
    return pl.pallas_call(
        functools.partial(_ffn_kernel, row_chunk=min(tm, 64), final_norm=final_norm),
        grid=(n // tm, d_ff // tf), in_specs=in_specs,
        out_specs=pl.BlockSpec((tm, d), lambda i, f: (i, 0)),
        out_shape=jax.ShapeDtypeStruct((n, d), F32),
        scratch_shapes=[pltpu.VMEM((tm, d), BF16)],
        compiler_params=_params(vmem, 2), name="ffn",
    )(*args)


def _rope_tables(pos, dk, batch):
    inv = 1.0 / (ROPE_BASE ** jnp.linspace(0.0, 1.0, dk // 2, dtype=F32))
    th = pos.astype(F32)[:, None] * jnp.repeat(inv, 2)[None, :]
    sin, cos = jnp.sin(th), jnp.cos(th)
    even = (jnp.arange(dk) % 2) == 0
    sin_next = jnp.where(even, -sin, 0.0)
    sin_prev = jnp.where(even, 0.0, sin)
    return tuple(jnp.tile(t, (batch, 1)) for t in (cos, sin_next, sin_prev))


def _cast_w_in(w_in, diff_w, q_scale):
    col_scale = jnp.where(jnp.arange(w_in.shape[-1]) < diff_w, q_scale, 1.0).astype(F32)
    return (w_in * col_scale).astype(BF16)


def _mixer(x, layer, w, w_in, p, rope, past, kv_stack, side, *, depth, batch, seq, heads,
           ret_heads):
    d = x.shape[1]
    hd = p["subln_g"].shape[-1]
    diff_w = heads * hd
    dk = p["state"].shape[-2]
    dv = p["state"].shape[-1]
    qk_w = ret_heads * dk
    ret_w = ret_heads * dv
    c_k, c_v, c_rqk = diff_w, 2 * diff_w, 3 * diff_w
    skipped = 2 * diff_w + 2 * qk_w
    c_rv, c_rg, c_ga, c_gb = diff_w, diff_w + ret_w, diff_w + 2 * ret_w, diff_w + 2 * ret_w + d
    main_w = c_gb + d

    main, h = _proj(x, *w_in, 0, main_w, gain=p["norm_mix_g"], emit_h=True,
                    gap=(diff_w, skipped))
    k32, k16, cast_k = _proj_heads(h, *w_in, c_k, heads, kv_stack[0], depth, layer,
                                   cast=() if past else (side["up"], side["a"], side["b"]))
    v32, v16, cast_v = _proj_heads(h, *w_in, c_v, heads, kv_stack[1], depth, layer,
                                   cast=() if past else (side["down"], side["out"]))
    if not past:
        w = dict(zip(("up", "a", "b", "down", "out"), cast_k + cast_v))
    rqk = _proj(h, *w_in, c_rqk, 2 * qk_w, epi="rope", extra=rope,
                rope_scale=dk ** -0.5, tn_pref=qk_w)

    lam_init = 0.8 - 0.6 * math.exp(-0.3 * layer)
    lams = (p["lq1"], p["lk1"], p["lq2"], p["lk2"])
    cast_w_in = None
    if past:
        oa = _attn_decode(main, k16, v16, p["cache_k"], p["cache_v"], layer, p["subln_g"],
                          lam_init, lams, batch=batch, seq=seq, heads=heads)
        state0 = p["state"]
    else:
        if layer + 1 < depth:
            cast_w_in = (side["in"], layer + 1, diff_w, side["q_scale"])
        oa = _attn_prompt(main, k16, v16, p["subln_g"], lam_init, lams, batch=batch,
                          seq=seq, heads=heads)
        state0 = None
    ob, st = _retention(rqk, main, c_rv, c_rg, dv, p["ret_g"], state0, layer, kv_stack[2], depth,
                        batch=batch, seq=seq, heads=ret_heads)
    merged, w_in_next = _merge(oa, ob, main, c_ga, c_gb, w["a"][None], w["b"][None], 0,
                               cast_w_in=cast_w_in)
    x = _proj(merged, w["out"][None], 0, 0, d, epi="residual", extra=(x,), tm_pref=512,
              tn_pref=d)
    return x, (k32, v32, st), w_in_next, w


def kernel(x_prompt, x_sample, cache_diff_k, cache_diff_v, state_ret, norm_mix_g, w_in,
           lambda_q1, lambda_k1, lambda_q2, lambda_k2, diff_subln_g, ret_norm_g,
           w_branch_a, w_branch_b, w_out, norm_ffn_g, w_up, w_down, final_norm_g):
    depth, dec_batch, past_len, heads, hd = cache_diff_k.shape
    batch, seq, d = x_prompt.shape
    dec_seq = x_sample.shape[1]
    ret_heads, dk, dv = state_ret.shape[2:]

    q_scale = (hd // 2) ** -0.5 * LOG2E
    side = {"in": w_in, "up": w_up, "down": w_down, "a": w_branch_a, "b": w_branch_b,
            "out": w_out, "q_scale": q_scale}
    w_in_cur = _cast_w_in(w_in[0:1], heads * hd, q_scale)
    cache_k, cache_v = cache_diff_k, cache_diff_v
    rope_p = _rope_tables(jnp.arange(seq), dk, batch)
    rope_s = _rope_tables(past_len + jnp.arange(dec_seq), dk, dec_batch)

    xp = x_prompt.reshape(batch * seq, d)
    xs = x_sample.reshape(dec_batch * dec_seq, d)
    kv_p, kv_s = (None,) * 3, (None,) * 3
    for l in range(depth):
        p = {"norm_mix_g": norm_mix_g[l], "lq1": lambda_q1[l], "lk1": lambda_k1[l],
             "lq2": lambda_q2[l], "lk2": lambda_k2[l], "subln_g": diff_subln_g[l],
             "ret_g": ret_norm_g[l], "cache_k": cache_k, "cache_v": cache_v, "state": state_ret}
        final_g = final_norm_g if l == depth - 1 else None
        xp, kv_p, w_in_next, w = _mixer(
            xp, l, None, (w_in_cur, 0), p, rope_p, False, kv_p, side, depth=depth, batch=batch,
            seq=seq, heads=heads, ret_heads=ret_heads)
        xs, kv_s, _, _ = _mixer(
            xs, l, w, (w_in_cur, 0), p, rope_s, True, kv_s, None, depth=depth, batch=dec_batch,
            seq=dec_seq, heads=heads, ret_heads=ret_heads)
        xp = _ffn(xp, norm_ffn_g[l], w["up"][None], w["down"][None], 0, final_g)
        xs = _ffn(xs, norm_ffn_g[l], w["up"][None], w["down"][None], 0, final_g)
        if w_in_next is not None:
            w_in_cur = w_in_next[None]
    return (xp.reshape(batch, seq, d), xs.reshape(dec_batch, dec_seq, d),
            kv_p[0].reshape(depth, batch, seq, heads, hd),
            kv_p[1].reshape(depth, batch, seq, heads, hd), kv_p[2],
            kv_s[0].reshape(depth, dec_batch, dec_seq, heads, hd),
            kv_s[1].reshape(depth, dec_batch, dec_seq, heads, hd), kv_s[2])
```

```python
import functools
import math

import jax
import jax.numpy as jnp
from jax import lax
from jax.experimental import pallas as pl
from jax.experimental.pallas import tpu as pltpu

F32 = jnp.float32
BF16 = jnp.bfloat16

NORM_EPS = 1e-6
ROPE_BASE = 10000.0
MASK_CHUNK = 64
NEG_INF = -1e30
SHIFT_LIMIT = 48.0
NORM_SLACK = 1.02
FAST_SWEEP_GROUP = 12
LOG2E = 1.4426950408889634

V7X_VMEM_BYTES = 64 * 1024 * 1024
V7X_LANES = 128
VMEM_REQUEST_CAP = V7X_VMEM_BYTES - 8 * 1024 * 1024

NT_DIMS = (((1,), (1,)), ((), ()))
TN_DIMS = (((0,), (0,)), ((), ()))


def _pick_tile(n, pref, align):
    if n <= pref:
        return n
    t = pref - pref % align
    while t >= align:
        if n % t == 0:
            return t
        t -= align
    raise ValueError(f"no tile for {n} (pref {pref}, align {align})")


def _params(vmem_bytes, n_grid):
    return pltpu.CompilerParams(
        dimension_semantics=("arbitrary",) * n_grid,
        vmem_limit_bytes=int(min(VMEM_REQUEST_CAP, vmem_bytes)))


def _rms_rows(x, gain):
    ms = jnp.mean(x * x, axis=-1, keepdims=True)
    return x * lax.rsqrt(ms + NORM_EPS) * gain


def _proj_kernel(*refs, has_norm, emit_h, epi, row_chunk, rope_scale):
    refs = list(refs)
    x_ref = refs.pop(0)
    g_ref = refs.pop(0) if has_norm else None
    w_ref = refs.pop(0)
    if epi == "rope":
        cos_ref, sina_ref, sinb_ref = refs.pop(0), refs.pop(0), refs.pop(0)
    if epi == "residual":
        res_ref = refs.pop(0)
    o_ref = refs.pop(0)
    hout_ref = refs.pop(0) if emit_h else None
    h_scr = refs.pop(0) if has_norm else None

    j = pl.program_id(1)
    if has_norm:
        @pl.when(j == 0)
        def _():
            def rows(r, carry):
                rs = pl.ds(pl.multiple_of(r * row_chunk, row_chunk), row_chunk)
                hrow = _rms_rows(x_ref[rs, :], g_ref[...]).astype(BF16)
                h_scr[rs, :] = hrow
                if emit_h:
                    hout_ref[rs, :] = hrow
                return carry
            lax.fori_loop(0, x_ref.shape[0] // row_chunk, rows, 0, unroll=2)
        lhs = h_scr[...]
    else:
        lhs = x_ref[...]

    acc = jnp.dot(lhs, w_ref[...], preferred_element_type=F32)
    if epi == "bf16":
        o_ref[...] = acc.astype(BF16)
    elif epi == "residual":
        o_ref[...] = res_ref[...] + acc
    elif epi == "rope":
        n_tiles = pl.num_programs(1)
        scale = jnp.where(j < n_tiles // 2, 1.0, rope_scale).astype(F32)
        cos, sina, sinb = cos_ref[...], sina_ref[...], sinb_ref[...]
        for c in range(acc.shape[1] // V7X_LANES):
            cs = slice(c * V7X_LANES, (c + 1) * V7X_LANES)
            xs = acc[:, cs]
            nxt = pltpu.roll(xs, V7X_LANES - 1, 1)
            prv = pltpu.roll(xs, 1, 1)
            o_ref[:, cs] = ((xs * cos + nxt * sina + prv * sinb) * scale).astype(BF16)
    else:
        raise ValueError(epi)


def _proj(lhs, w, layer, col0, ncols, *, gain=None, emit_h=False, epi="bf16", extra=(),
          rope_scale=1.0, gap=(0, 0), tm_pref=1024, tn_pref=1024):
    n, k = lhs.shape
    has_norm = gain is not None
    tm = _pick_tile(n, tm_pref, 16)
    tn = _pick_tile(ncols, tn_pref, V7X_LANES)
    assert col0 % tn == 0 and gap[0] % tn == 0 and gap[1] % tn == 0
    cb = col0 // tn
    lead_tiles, skip_tiles = gap[0] // tn, gap[1] // tn
    grid = (n // tm, ncols // tn)

    def w_tile(j):
        return cb + j + jnp.where(j >= lead_tiles, skip_tiles, 0) if skip_tiles else cb + j

    in_specs = [pl.BlockSpec((tm, k), lambda i, j: (i, 0))]
    args = [lhs]
    if has_norm:
        in_specs.append(pl.BlockSpec((1, k), lambda i, j: (0, 0)))
        args.append(gain.reshape(1, k))
    in_specs.append(pl.BlockSpec((None, k, tn), lambda i, j: (layer, 0, w_tile(j))))
    args.append(w)
    if epi == "rope":
        for t in extra:
            in_specs.append(pl.BlockSpec((tm, V7X_LANES), lambda i, j: (i, 0)))
            args.append(t)
    elif epi == "residual":
        in_specs.append(pl.BlockSpec((tm, tn), lambda i, j: (i, j)))
        args.append(extra[0])

    tile_spec = pl.BlockSpec((tm, tn), lambda i, j: (i, j))
    out_dtype = F32 if epi == "residual" else BF16
    out_shape = [jax.ShapeDtypeStruct((n, ncols), out_dtype)]
    out_specs = [tile_spec]
    out_bytes = tm * tn * jnp.dtype(out_dtype).itemsize
    if emit_h:
        out_shape.append(jax.ShapeDtypeStruct((n, k), BF16))
        out_specs.append(pl.BlockSpec((tm, k), lambda i, j: (i, 0)))
        out_bytes += tm * k * 2
    scratch = [pltpu.VMEM((tm, k), BF16)] if has_norm else []

    vmem = (2 * tm * k * lhs.dtype.itemsize + 2 * k * tn * 2 + 2 * out_bytes
            + (tm * k * 2 if has_norm else 0) + 3 * tm * tn * 4
            + (2 * tm * tn * 4 if epi == "residual" else 0) + (4 << 20))
    outs = pl.pallas_call(
        functools.partial(_proj_kernel, has_norm=has_norm, emit_h=emit_h, epi=epi,
                          row_chunk=min(tm, 64), rope_scale=rope_scale),
        grid=grid, in_specs=in_specs, out_specs=out_specs, out_shape=out_shape,
        scratch_shapes=scratch, compiler_params=_params(vmem, 2),
        name=f"proj_{epi}")(*args)
    return outs[0] if len(outs) == 1 else tuple(outs)


def _proj_heads_kernel(*refs, aliased, n_cast):
    x_ref, w_ref = refs[0], refs[1]
    n_in = 2 + aliased + n_cast
    o32_ref, o16_ref = refs[n_in], refs[n_in + 1]
    for c in range(n_cast):
        refs[n_in + 2 + c][...] = refs[n_in - n_cast + c][...].astype(BF16)
    acc = jnp.dot(x_ref[...], w_ref[...], preferred_element_type=F32)
    o16_ref[...] = acc.astype(BF16)
    o32_ref[...] = acc.reshape(o32_ref.shape)


def _proj_heads(lhs, w, w_layer, col0, heads, stacked, depth, layer, *, tm_pref=512, cast=()):
    n, k = lhs.shape
    ncols = k
    hd = ncols // heads
    tm = _pick_tile(n, tm_pref, 16)
    cb = col0 // ncols
    assert col0 % ncols == 0
    aliased = stacked is not None
    in_specs = [pl.BlockSpec((tm, k), lambda i: (i, 0)),
                pl.BlockSpec((None, k, ncols), lambda i: (w_layer, 0, cb))]
    args = [lhs, w]
    if aliased:
        in_specs.append(pl.BlockSpec(memory_space=pl.ANY))
        args.append(stacked)
    vmem = 4 * tm * k * 2 + 4 * k * ncols * 2 + 4 * tm * ncols * 4 + 4 * tm * ncols * 2 + (4 << 20)
    out_specs = [pl.BlockSpec((None, tm, heads, hd), lambda i: (layer, i, 0, 0)),
                 pl.BlockSpec((tm, ncols), lambda i: (i, 0))]
    out_shape = [jax.ShapeDtypeStruct((depth, n, heads, hd), F32),
                 jax.ShapeDtypeStruct((n, ncols), BF16)]
    slices = _cast_slices(cast, layer, n // tm, lambda i: i) if cast else None
    if slices:
        in_specs += slices[0]
        args += list(cast)
        out_specs += slices[1]
        out_shape += slices[2]
        vmem += slices[3]
    outs = pl.pallas_call(
        functools.partial(_proj_heads_kernel, aliased=aliased, n_cast=len(cast) if slices else 0),
        grid=(n // tm,), in_specs=in_specs, out_specs=out_specs, out_shape=out_shape,
        input_output_aliases={2: 0} if aliased else {},
        compiler_params=_params(vmem, 1), name="proj_heads")(*args)
    casted = tuple(outs[2:]) if slices else tuple(c[layer].astype(BF16) for c in cast)
    return outs[0], outs[1], casted


def _lam_value(lamc_ref, lq1_ref, lk1_ref, lq2_ref, lk2_ref):
    a = jnp.exp(jnp.sum(lq1_ref[...] * lk1_ref[...], axis=-1, keepdims=True))
    b = jnp.exp(jnp.sum(lq2_ref[...] * lk2_ref[...], axis=-1, keepdims=True))
    return a - b + lamc_ref[0]


def _lane_repeat(x, reps):
    return x if reps == 1 else jnp.concatenate([x] * reps, axis=1)


def _softmax_update(s, v, m_prev, l_prev, acc_prev):
    keys = s.shape[1]
    m_new = jnp.maximum(m_prev, jnp.max(s, axis=1, keepdims=True))
    alpha = jnp.exp2(m_prev - m_new)
    p = jnp.exp2(s - _lane_repeat(m_new, keys // V7X_LANES))
    l_new = alpha * l_prev + jnp.sum(p, axis=1, keepdims=True)
    pv = jnp.dot(p.astype(BF16), v, preferred_element_type=F32)
    acc_new = acc_prev * _lane_repeat(alpha, acc_prev.shape[1] // V7X_LANES) + pv
    return m_new, l_new, acc_new


def _softmax_step(s, v, m_ref, l_ref, acc_ref):
    m_ref[...], l_ref[...], acc_ref[...] = _softmax_update(s, v, m_ref[...], l_ref[...],
                                                           acc_ref[...])


def _diff_finish(a1, l1, a2, l2, lam, gain, out_scale):
    rep = a1.shape[1] // V7X_LANES
    o = a1 * _lane_repeat(1.0 / l1, rep) - a2 * _lane_repeat(lam / l2, rep)
    return _rms_rows(o, gain * out_scale)


def _attn_prompt_kernel(lamc_ref, lq1_ref, lk1_ref, lq2_ref, lk2_ref, q_ref, k_ref, v_ref, g_ref,
                        o_ref, m1, l1, a1, m2, l2, a2, sa1, sa2, sb1, sb2, ka1, ka2, ext, *, dh, tq):
    seq = q_ref.shape[0]
    tk = tq
    nq = seq // tq
    n_units = nq * (nq + 1) // 2
    shift = MASK_CHUNK.bit_length() - 1
    assert tq // MASK_CHUNK < dh
    maps = ((0, m1, l1, a1, ka1), (dh, m2, l2, a2, ka2))

    lane = lax.broadcasted_iota(jnp.int32, (tk, dh), 1)
    row_chunk = lax.shift_right_logical(lax.broadcasted_iota(jnp.int32, (tk, dh), 0), shift)
    key_ext = jnp.where(lane == 0, -1.0, jnp.where(lane - 1 == row_chunk, 1.0, 0.0)).astype(BF16)
    hidden = (lane >= 1) & (lane <= tq // MASK_CHUNK) & (lane - 1 > row_chunk)

    ones = jnp.ones((dh, dh), BF16)

    def widen(c, carry):
        rs = pl.ds(pl.multiple_of(c * tk, tk), tk)
        out = []
        for idx, (lo, _, l_ref, a_ref, ka) in enumerate(maps):
            l_ref[rs, :] = jnp.zeros((tk, V7X_LANES), F32)
            a_ref[rs, :] = jnp.zeros((tk, 2 * dh), F32)
            kc = k_ref[rs, lo:lo + dh]
            qc = q_ref[rs, lo:lo + dh]
            ka[rs, 0:dh] = kc
            ka[rs, dh:2 * dh] = key_ext
            k2 = jnp.dot(kc * kc, ones, preferred_element_type=F32)
            q2 = jnp.dot(qc * qc, ones, preferred_element_type=F32)
            out += [jnp.maximum(carry[2 * idx], jnp.max(k2, axis=0, keepdims=True)),
                    jnp.maximum(carry[2 * idx + 1], jnp.max(q2, axis=0, keepdims=True))]
        return tuple(out)
    norms2 = lax.fori_loop(0, nq, widen, (jnp.zeros((1, dh), F32),) * 4, unroll=True)

    worst = jnp.zeros((1, 1), F32)
    for idx in range(2):
        bound = jnp.sqrt(norms2[2 * idx][:, 0:1] * norms2[2 * idx + 1][:, 0:1]) * NORM_SLACK
        worst = jnp.maximum(worst, bound)
        base = jnp.where(lane == 0, bound, 0.0)
        ext[(2 * idx) * tq:(2 * idx + 1) * tq, :] = base.astype(BF16)
        ext[(2 * idx + 1) * tq:(2 * idx + 2) * tq, :] = jnp.where(hidden, NEG_INF, base).astype(BF16)
    fast = jnp.max(worst) <= SHIFT_LIMIT

    lam = _lam_value(lamc_ref, lq1_ref, lk1_ref, lq2_ref, lk2_ref)
    gain, out_scale = g_ref[...], lamc_ref[1]

    def q_rows(qi):
        return pl.ds(pl.multiple_of(qi * tq, tq), tq)

    def k_rows(j):
        return pl.ds(pl.multiple_of(j * tk, tk), tk)

    def scores(qi, j, bufs):
        diag = jnp.where(qi == j, 1, 0)
        for idx, ((lo, _, _, _, ka), s_scr) in enumerate(zip(maps, bufs)):
            e_rows = pl.ds(pl.multiple_of((2 * idx + diag) * tq, tq), tq)
            qa = jnp.concatenate([q_ref[q_rows(qi), lo:lo + dh], ext[e_rows, :]], axis=1)
            s_scr[...] = lax.dot_general(qa, ka[k_rows(j), :], NT_DIMS,
                                         preferred_element_type=F32)

    def consume(qi, j, bufs, exact):
        v = v_ref[k_rows(j), :]
        rq = q_rows(qi)
        for (_, m_ref, l_ref, a_ref, _), s_scr in zip(maps, bufs):
            s = s_scr[...]
            l_prev, a_prev = l_ref[rq, :], a_ref[rq, :]
            if exact:
                m_ref[rq, :], l_new, a_new = _softmax_update(s, v, m_ref[rq, :], l_prev, a_prev)
            else:
                p = jnp.exp2(s)
                l_new = l_prev + jnp.sum(p, axis=1, keepdims=True)
                a_new = a_prev + jnp.dot(p.astype(BF16), v, preferred_element_type=F32)
            l_ref[rq, :], a_ref[rq, :] = l_new, a_new

    def advance(qi, j):
        last = j == qi
        return jnp.where(last, qi + 1, qi), jnp.where(last, 0, j + 1)

    buf_a, buf_b = (sa1, sa2), (sb1, sb2)
    scores(0, 0, buf_a)

    units = [(qi, j) for qi in range(nq) for j in range(qi + 1)]

    def sweep(exact, group):
        if exact:
            for m_ref in (m1, m2):
                m_ref[...] = jnp.full(m_ref.shape, -jnp.inf, F32)

        def trip(t, carry):
            cur = carry
            for g in range(group):
                nxt = advance(*cur)
                over = nxt[0] >= nq
                safe = (jnp.where(over, nq - 1, nxt[0]), jnp.where(over, 0, nxt[1]))
                mine, other = (buf_a, buf_b) if g % 2 == 0 else (buf_b, buf_a)
                scores(*safe, other)
                consume(*cur, mine, exact)
                cur = nxt
            return cur
        n_trips = n_units // group
        lax.fori_loop(0, n_trips, trip, (jnp.int32(0), jnp.int32(0)))
        rest = units[n_trips * group:]
        for g, (qi, j) in enumerate(rest):
            mine, other = (buf_a, buf_b) if g % 2 == 0 else (buf_b, buf_a)
            if g + 1 < len(rest):
                scores(*rest[g + 1], other)
            consume(qi, j, mine, exact)

    @pl.when(fast)
    def _():
        sweep(False, FAST_SWEEP_GROUP)

    @pl.when(jnp.logical_not(fast))
    def _():
        sweep(True, 2)

    def finish(qi, carry):
        rq = q_rows(qi)
        o = _diff_finish(a1[rq, :], l1[rq, :], a2[rq, :], l2[rq, :], lam, gain, out_scale)
        o_ref[rq, :] = o.astype(BF16)
        return carry
    lax.fori_loop(0, nq, finish, 0, unroll=2)


def _lam_inputs(lam_init, lams):
    specs = [pl.BlockSpec(memory_space=pltpu.SMEM)]
    args = [jnp.array([lam_init, 1.0 - lam_init], F32)]
    for v in lams:
        specs.append(pl.BlockSpec((1, v.shape[-1]), lambda *_: (0, 0)))
        args.append(v.reshape(1, -1))
    return specs, args


def _attn_prompt(dq, k16, v16, gain, lam_init, lams, *, batch, seq, heads, tq_pref=512):
    n, width = k16.shape
    hd = width // heads
    dh = hd // 2
    tq = _pick_tile(seq, tq_pref, V7X_LANES)
    lam_specs, lam_args = _lam_inputs(lam_init, lams)
    per_head = pl.BlockSpec((seq, hd), lambda b, h: (b, h))
    in_specs = lam_specs + [per_head, per_head, per_head,
                            pl.BlockSpec((1, hd), lambda b, h: (0, 0))]
    stat = pltpu.VMEM((seq, V7X_LANES), F32)
    accs = pltpu.VMEM((seq, hd), F32)
    sbuf = pltpu.VMEM((tq, tq), F32)
    wide_k = pltpu.VMEM((seq, hd), BF16)
    vmem = 8 * seq * hd * 2 + 2 * seq * hd * 2 + 2 * seq * hd * 4 + 4 * seq * V7X_LANES * 4 \
        + 12 * tq * tq * 4 + (4 << 20)
    return pl.pallas_call(
        functools.partial(_attn_prompt_kernel, dh=dh, tq=tq),
        grid=(batch, heads), in_specs=in_specs, out_specs=per_head,
        out_shape=jax.ShapeDtypeStruct((n, width), BF16),
        scratch_shapes=[stat, stat, accs, stat, stat, accs, sbuf, sbuf, sbuf, sbuf,
                        wide_k, wide_k, pltpu.VMEM((4 * tq, dh), BF16)],
        compiler_params=_params(vmem, 2), name="diff_attn_prompt",
    )(*lam_args, dq, k16, v16, gain.reshape(1, hd))


def _attn_decode_kernel(lamc_ref, lq1_ref, lk1_ref, lq2_ref, lk2_ref, q_ref, kn_ref, vn_ref,
                        kc_ref, vc_ref, g_ref, o_ref, qd, m_scr, l_scr, a_scr,
                        *, heads, dh, n_new):
    j = pl.program_id(1)
    hd = 2 * dh
    s_q = q_ref.shape[0]

    @pl.when(j == 0)
    def _():
        qd[...] = jnp.zeros(qd.shape, BF16)
        m_scr[...] = jnp.full(m_scr.shape, -jnp.inf, F32)
        l_scr[...] = jnp.zeros(l_scr.shape, F32)
        a_scr[...] = jnp.zeros(a_scr.shape, F32)
        new_mask = lax.broadcasted_iota(jnp.int32, (2 * s_q, kn_ref.shape[0]), 1) < n_new
        for h in range(heads):
            hs = slice(h * hd, (h + 1) * hd)
            qd[h, 0:s_q, 0:dh] = q_ref[:, h * hd:h * hd + dh]
            qd[h, s_q:2 * s_q, dh:hd] = q_ref[:, h * hd + dh:(h + 1) * hd]
            s = lax.dot_general(qd[h], kn_ref[:, hs], NT_DIMS, preferred_element_type=F32)
            s = jnp.where(new_mask, s, NEG_INF)
            _softmax_step(s, vn_ref[:, hs], m_scr.at[h], l_scr.at[h], a_scr.at[h])

    tk = kc_ref.shape[0]
    k_all = kc_ref[...].astype(BF16).reshape(tk, heads * hd)
    v_all = vc_ref[...].astype(BF16).reshape(tk, heads * hd)
    scores = [lax.dot_general(qd[h], k_all[:, h * hd:(h + 1) * hd], NT_DIMS,
                              preferred_element_type=F32) for h in range(heads)]
    for h in range(heads):
        _softmax_step(scores[h], v_all[:, h * hd:(h + 1) * hd], m_scr.at[h], l_scr.at[h],
                      a_scr.at[h])

    @pl.when(j == pl.num_programs(1) - 1)
    def _():
        lam = _lam_value(lamc_ref, lq1_ref, lk1_ref, lq2_ref, lk2_ref)
        for h in range(heads):
            a, l = a_scr[h], l_scr[h]
            o = _diff_finish(a[:s_q], l[:s_q], a[s_q:], l[s_q:], lam, g_ref[...], lamc_ref[1])
            o_ref[:, h * hd:(h + 1) * hd] = o.astype(BF16)


def _attn_decode(dq, k16, v16, cache_k, cache_v, layer, gain, lam_init, lams, *, batch, seq,
                 heads, tk_pref=1024):
    n, width = k16.shape
    hd = width // heads
    dh = hd // 2
    past = cache_k.shape[2]
    tk = _pick_tile(past, tk_pref, V7X_LANES)
    pad = V7X_LANES
    kn = jnp.pad(k16.reshape(batch, seq, width), ((0, 0), (0, pad - seq), (0, 0)))
    vn = jnp.pad(v16.reshape(batch, seq, width), ((0, 0), (0, pad - seq), (0, 0)))
    lam_specs, lam_args = _lam_inputs(lam_init, lams)
    in_specs = lam_specs + [
        pl.BlockSpec((seq, width), lambda b, j: (b, 0)),
        pl.BlockSpec((None, pad, width), lambda b, j: (b, 0, 0)),
        pl.BlockSpec((None, pad, width), lambda b, j: (b, 0, 0)),
        pl.BlockSpec((None, None, tk, heads, hd), lambda b, j: (layer, b, j, 0, 0)),
        pl.BlockSpec((None, None, tk, heads, hd), lambda b, j: (layer, b, j, 0, 0)),
        pl.BlockSpec((1, hd), lambda b, j: (0, 0)),
    ]
    scratch = [pltpu.VMEM((heads, 2 * seq, hd), BF16),
               pltpu.VMEM((heads, 2 * seq, V7X_LANES), F32),
               pltpu.VMEM((heads, 2 * seq, V7X_LANES), F32),
               pltpu.VMEM((heads, 2 * seq, hd), F32)]
    vmem = 4 * tk * width * 4 + 2 * tk * width * 2 + 8 * pad * width * 2 + (8 << 20)
    return pl.pallas_call(
        functools.partial(_attn_decode_kernel, heads=heads, dh=dh, n_new=seq),
        grid=(batch, past // tk), in_specs=in_specs,
        out_specs=pl.BlockSpec((seq, width), lambda b, j: (b, 0)),
        out_shape=jax.ShapeDtypeStruct((n, width), BF16),
        scratch_shapes=scratch, compiler_params=_params(vmem, 2), name="diff_attn_decode",
    )(*lam_args, dq, kn, vn, cache_k, cache_v, gain.reshape(1, hd))


def _ret_tables(heads, chunk, dk, dv):
    log_gamma = jnp.log1p(-jnp.exp2(-5.0 - jnp.arange(heads, dtype=F32)))
    idx = jnp.arange(chunk, dtype=F32)
    diff = idx[:, None] - idx[None, :]
    decay = jnp.where(diff >= 0, jnp.exp(log_gamma[:, None, None] * jnp.maximum(diff, 0.0)), 0.0)
    cross = jnp.exp(log_gamma[:, None] * (idx + 1.0)[None, :])
    kdec = jnp.exp(log_gamma[:, None] * (chunk - 1.0 - idx)[None, :])
    carry = jnp.exp(log_gamma * chunk)
    return (decay,
            jnp.broadcast_to(cross[:, :, None], (heads, chunk, dv)),
            jnp.broadcast_to(kdec[:, :, None], (heads, chunk, dk)),
            carry)


def _ret_kernel(*refs, has_init):
    refs = list(refs)
    carry_ref, q_ref, k_ref, v_ref, rg_ref, dec_ref, cross_ref, kdec_ref, g_ref = refs[:9]
    refs = refs[9:]
    s0_ref = refs.pop(0) if has_init else None
    o_ref, sout_ref, s_scr = refs[-3:]
    heads, dk, dv = s_scr.shape

    @pl.when(pl.program_id(1) == 0)
    def _():
        s_scr[...] = s0_ref[...] if has_init else jnp.zeros(s_scr.shape, F32)

    for h in range(heads):
        ks, vs = slice(h * dk, (h + 1) * dk), slice(h * dv, (h + 1) * dv)
        q, k, v = q_ref[:, ks], k_ref[:, ks], v_ref[:, vs]
        state = s_scr[h]
        scores = lax.dot_general(q, k, NT_DIMS, preferred_element_type=F32) * dec_ref[h]
        inner = jnp.dot(scores.astype(BF16), v, preferred_element_type=F32)
        cross = jnp.dot(q, state.astype(BF16), preferred_element_type=F32) * cross_ref[h]
        k_dec = (k.astype(F32) * kdec_ref[h]).astype(BF16)
        new_state = carry_ref[h] * state + lax.dot_general(k_dec, v, TN_DIMS,
                                                           preferred_element_type=F32)
        s_scr[h] = new_state
        sout_ref[h] = new_state
        gate = rg_ref[:, vs].astype(F32)
        y = _rms_rows(inner + cross, g_ref[...]) * (gate * jax.nn.sigmoid(gate))
        o_ref[:, vs] = y.astype(BF16)


def _cast_slices(weights, layer, n_steps, step_of):
    in_specs, out_specs, out_shapes, vmem = [], [], [], 0
    for w in weights:
        rows, cols = w.shape[1:]
        if rows % (16 * n_steps):
            return None
        r = rows // n_steps
        in_specs.append(pl.BlockSpec((None, r, cols), lambda *g: (layer, step_of(*g), 0)))
        out_specs.append(pl.BlockSpec((r, cols), lambda *g: (step_of(*g), 0)))
        out_shapes.append(jax.ShapeDtypeStruct((rows, cols), BF16))
        vmem += 2 * r * cols * (4 + 2)
    return in_specs, out_specs, out_shapes, vmem


def _retention(rqk, main, c_v, c_gate, dv, gain, state0, layer, state_stack, depth, *, batch, seq,
               heads, chunk_pref=256):
    n = rqk.shape[0]
    dk = rqk.shape[1] // (2 * heads)
    assert c_v % (heads * dv) == 0 and c_gate % (heads * dv) == 0
    vb, gb = c_v // (heads * dv), c_gate // (heads * dv)
    chunk = _pick_tile(seq, chunk_pref, 16)
    nc = seq // chunk
    decay, cross, kdec, carry = _ret_tables(heads, chunk, dk, dv)
    has_init = state0 is not None
    row = lambda b, c: b * nc + c
    whole = lambda shape: pl.BlockSpec(shape, lambda b, c: (0,) * len(shape))
    in_specs = [
        pl.BlockSpec(memory_space=pltpu.SMEM),
        pl.BlockSpec((chunk, heads * dk), lambda b, c: (row(b, c), 0)),
        pl.BlockSpec((chunk, heads * dk), lambda b, c: (row(b, c), 1)),
        pl.BlockSpec((chunk, heads * dv), lambda b, c: (row(b, c), vb)),
        pl.BlockSpec((chunk, heads * dv), lambda b, c: (row(b, c), gb)),
        whole((heads, chunk, chunk)), whole((heads, chunk, dv)), whole((heads, chunk, dk)),
        whole((1, dv)),
    ]
    args = [carry, rqk, rqk, main, main, decay, cross, kdec, gain.reshape(1, dv)]
    if has_init:
        in_specs.append(pl.BlockSpec((None, None, heads, dk, dv),
                                     lambda b, c: (layer, b, 0, 0, 0)))
        args.append(state0)
    aliases = {}
    if state_stack is not None:
        in_specs.append(pl.BlockSpec(memory_space=pl.ANY))
        args.append(state_stack)
        aliases = {len(args) - 1: 1}
    vmem = (2 * heads * chunk * (chunk + dv + dk) * 4 + 8 * chunk * heads * (dk + dv) * 2
            + 6 * heads * dk * dv * 4 + (8 << 20))
    ob, st = pl.pallas_call(
        functools.partial(_ret_kernel, has_init=has_init),
        grid=(batch, nc), in_specs=in_specs,
        out_specs=[pl.BlockSpec((chunk, heads * dv), lambda b, c: (row(b, c), 0)),
                   pl.BlockSpec((None, None, heads, dk, dv), lambda b, c: (layer, b, 0, 0, 0))],
        out_shape=[jax.ShapeDtypeStruct((n, heads * dv), BF16),
                   jax.ShapeDtypeStruct((depth, batch, heads, dk, dv), F32)],
        scratch_shapes=[pltpu.VMEM((heads, dk, dv), F32)],
        input_output_aliases=aliases,
        compiler_params=_params(vmem, 2), name="retention",
    )(*args)
    return ob, st


def _merge_kernel(oa_ref, ob_ref, wa_ref, wb_ref, ga_ref, gb_ref, *refs, w_scale):
    if w_scale is not None:
        wsrc_ref, o_ref, wdst_ref = refs
        col = lax.broadcasted_iota(jnp.int32, (1, wsrc_ref.shape[1]), 1)
        wdst_ref[...] = (wsrc_ref[...] * jnp.where(col < w_scale[0], w_scale[1], 1.0)).astype(BF16)
    else:
        (o_ref,) = refs
    a = jnp.dot(oa_ref[...], wa_ref[...], preferred_element_type=F32)
    b = jnp.dot(ob_ref[...], wb_ref[...], preferred_element_type=F32)
    ga = jax.nn.sigmoid(ga_ref[...].astype(F32))
    gb = jax.nn.sigmoid(gb_ref[...].astype(F32))
    o_ref[...] = (ga * a + gb * b).astype(BF16)


def _merge(oa, ob, main, c_ga, c_gb, w_a, w_b, layer, *, tm_pref=1024, tn_pref=512,
           cast_w_in=None):
    n, k = oa.shape
    d = w_a.shape[2]
    tm = _pick_tile(n, tm_pref, 16)
    tn = _pick_tile(d, tn_pref, V7X_LANES)
    assert c_ga % tn == 0 and c_gb % tn == 0
    ga0, gb0 = c_ga // tn, c_gb // tn
    n_j = d // tn
    in_specs = [pl.BlockSpec((tm, k), lambda i, j: (i, 0)),
                pl.BlockSpec((tm, k), lambda i, j: (i, 0)),
                pl.BlockSpec((None, k, tn), lambda i, j: (layer, 0, j)),
                pl.BlockSpec((None, k, tn), lambda i, j: (layer, 0, j)),
                pl.BlockSpec((tm, tn), lambda i, j: (i, ga0 + j)),
                pl.BlockSpec((tm, tn), lambda i, j: (i, gb0 + j))]
    args = [oa, ob, w_a, w_b, main, main]
    out_specs = [pl.BlockSpec((tm, tn), lambda i, j: (i, j))]
    out_shape = [jax.ShapeDtypeStruct((n, d), BF16)]
    vmem = 4 * tm * k * 2 + 4 * k * tn * 2 + 6 * tm * tn * 2 + 4 * tm * tn * 4 + (4 << 20)
    slices = None
    if cast_w_in is not None:
        slices = _cast_slices(cast_w_in[:1], cast_w_in[1], (n // tm) * n_j,
                              lambda i, j: i * n_j + j)
    if slices:
        in_specs += slices[0]
        args.append(cast_w_in[0])
        out_specs += slices[1]
        out_shape += slices[2]
        vmem += slices[3]
    outs = pl.pallas_call(
        functools.partial(_merge_kernel, w_scale=cast_w_in[2:] if slices else None),
        grid=(n // tm, n_j), in_specs=in_specs, out_specs=out_specs, out_shape=out_shape,
        compiler_params=_params(vmem, 2), name="branch_merge",
    )(*args)
    if cast_w_in is None:
        return outs[0], None
    if slices:
        return outs[0], outs[1]
    w, w_layer, q_cols, q_scale = cast_w_in
    return outs[0], _cast_w_in(w[w_layer:w_layer + 1], q_cols, q_scale)[0]


def _ffn_kernel(*refs, row_chunk, final_norm):
    refs = list(refs)
    x_ref, g_ref, wu_ref, wd_ref = refs[:4]
    fg_ref = refs[4] if final_norm else None
    o_ref, h_scr = refs[-2:]
    f = pl.program_id(1)
    n_chunks = x_ref.shape[0] // row_chunk

    @pl.when(f == 0)
    def _():
        def rows(r, carry):
            rs = pl.ds(pl.multiple_of(r * row_chunk, row_chunk), row_chunk)
            x = x_ref[rs, :]
            h_scr[rs, :] = _rms_rows(x, g_ref[...]).astype(BF16)
            o_ref[rs, :] = x
            return carry
        lax.fori_loop(0, n_chunks, rows, 0, unroll=2)

    u = jnp.dot(h_scr[...], wu_ref[...], preferred_element_type=F32)
    u = jnp.square(jnp.maximum(u, 0.0)).astype(BF16)
    o_ref[...] += jnp.dot(u, wd_ref[...], preferred_element_type=F32)

    if final_norm:
        @pl.when(f == pl.num_programs(1) - 1)
        def _():
            def rows(r, carry):
                rs = pl.ds(pl.multiple_of(r * row_chunk, row_chunk), row_chunk)
                o_ref[rs, :] = _rms_rows(o_ref[rs, :], fg_ref[...])
                return carry
            lax.fori_loop(0, n_chunks, rows, 0)


def _ffn(x, gain, w_up, w_down, layer, final_gain=None, *, tm_pref=1024, tf_pref=512):
    n, d = x.shape
    d_ff = w_up.shape[2]
    tm = _pick_tile(n, tm_pref, 16)
    tf = _pick_tile(d_ff, tf_pref, V7X_LANES)
    final_norm = final_gain is not None
    in_specs = [pl.BlockSpec((tm, d), lambda i, f: (i, 0)),
                pl.BlockSpec((1, d), lambda i, f: (0, 0)),
                pl.BlockSpec((None, d, tf), lambda i, f: (layer, 0, f)),
                pl.BlockSpec((None, tf, d), lambda i, f: (layer, f, 0))]
    args = [x, gain.reshape(1, d), w_up, w_down]
    if final_norm:
        in_specs.append(pl.BlockSpec((1, d), lambda i, f: (0, 0)))
        args.append(final_gain.reshape(1, d))
    vmem = 4 * tm * d * 4 + tm * d * 2 + 4 * d * tf * 2 + 2 * tm * tf * 4 + (6 << 20)
    return pl.pallas_call(
        functools.partial(_ffn_kernel, row_chunk=min(tm, 64), final_norm=final_norm),
        grid=(n // tm, d_ff // tf), in_specs=in_specs,
        out_specs=pl.BlockSpec((tm, d), lambda i, f: (i, 0)),
        out_shape=jax.ShapeDtypeStruct((n, d), F32),
        scratch_shapes=[pltpu.VMEM((tm, d), BF16)],
        compiler_params=_params(vmem, 2), name="ffn",
    )(*args)


def _rope_tables(pos, dk, batch):
    inv = 1.0 / (ROPE_BASE ** jnp.linspace(0.0, 1.0, dk // 2, dtype=F32))
    th = pos.astype(F32)[:, None] * jnp.repeat(inv, 2)[None, :]
    sin, cos = jnp.sin(th), jnp.cos(th)
    even = (jnp.arange(dk) % 2) == 0
    sin_next = jnp.where(even, -sin, 0.0)
    sin_prev = jnp.where(even, 0.0, sin)
    return tuple(jnp.tile(t, (batch, 1)) for t in (cos, sin_next, sin_prev))


def _cast_w_in(w_in, diff_w, q_scale):
    col_scale = jnp.where(jnp.arange(w_in.shape[-1]) < diff_w, q_scale, 1.0).astype(F32)
    return (w_in * col_scale).astype(BF16)


def _mixer(x, layer, w, w_in, p, rope, past, kv_stack, side, *, depth, batch, seq, heads,
           ret_heads):
    d = x.shape[1]
    hd = p["subln_g"].shape[-1]
    diff_w = heads * hd
    dk = p["state"].shape[-2]
    dv = p["state"].shape[-1]
    qk_w = ret_heads * dk
    ret_w = ret_heads * dv
    c_k, c_v, c_rqk = diff_w, 2 * diff_w, 3 * diff_w
    skipped = 2 * diff_w + 2 * qk_w
    c_rv, c_rg, c_ga, c_gb = diff_w, diff_w + ret_w, diff_w + 2 * ret_w, diff_w + 2 * ret_w + d
    main_w = c_gb + d

    main, h = _proj(x, *w_in, 0, main_w, gain=p["norm_mix_g"], emit_h=True,
                    gap=(diff_w, skipped))
    k32, k16, cast_k = _proj_heads(h, *w_in, c_k, heads, kv_stack[0], depth, layer,
                                   cast=() if past else (side["up"], side["a"], side["b"]))
    v32, v16, cast_v = _proj_heads(h, *w_in, c_v, heads, kv_stack[1], depth, layer,
                                   cast=() if past else (side["down"], side["out"]))
    if not past:
        w = dict(zip(("up", "a", "b", "down", "out"), cast_k + cast_v))
    rqk = _proj(h, *w_in, c_rqk, 2 * qk_w, epi="rope", extra=rope,
                rope_scale=dk ** -0.5, tn_pref=qk_w)

    lam_init = 0.8 - 0.6 * math.exp(-0.3 * layer)
    lams = (p["lq1"], p["lk1"], p["lq2"], p["lk2"])
    cast_w_in = None
    if past:
        oa = _attn_decode(main, k16, v16, p["cache_k"], p["cache_v"], layer, p["subln_g"],
                          lam_init, lams, batch=batch, seq=seq, heads=heads)
        state0 = p["state"]
    else:
        if layer + 1 < depth:
            cast_w_in = (side["in"], layer + 1, diff_w, side["q_scale"])
        oa = _attn_prompt(main, k16, v16, p["subln_g"], lam_init, lams, batch=batch,
                          seq=seq, heads=heads)
        state0 = None
    ob, st = _retention(rqk, main, c_rv, c_rg, dv, p["ret_g"], state0, layer, kv_stack[2], depth,
                        batch=batch, seq=seq, heads=ret_heads)
    merged, w_in_next = _merge(oa, ob, main, c_ga, c_gb, w["a"][None], w["b"][None], 0,
                               cast_w_in=cast_w_in)
    x = _proj(merged, w["out"][None], 0, 0, d, epi="residual", extra=(x,), tm_pref=512,
              tn_pref=d)
    return x, (k32, v32, st), w_in_next, w


def kernel(x_prompt, x_sample, cache_diff_k, cache_diff_v, state_ret, norm_mix_g, w_in,
           lambda_q1, lambda_k1, lambda_q2, lambda_k2, diff_subln_g, ret_norm_g,
           w_branch_a, w_branch_b, w_out, norm_ffn_g, w_up, w_down, final_norm_g):
    depth, dec_batch, past_len, heads, hd = cache_diff_k.shape
    batch, seq, d = x_prompt.shape
    dec_seq = x_sample.shape[1]
    ret_heads, dk, dv = state_ret.shape[2:]

    q_scale = (hd // 2) ** -0.5 * LOG2E
    side = {"in": w_in, "up": w_up, "down": w_down, "a": w_branch_a, "b": w_branch_b,
            "out": w_out, "q_scale": q_scale}
    w_in_cur = _cast_w_in(w_in[0:1], heads * hd, q_scale)
    cache_k, cache_v = cache_diff_k, cache_diff_v
    rope_p = _rope_tables(jnp.arange(seq), dk, batch)
    rope_s = _rope_tables(past_len + jnp.arange(dec_seq), dk, dec_batch)

    xp = x_prompt.reshape(batch * seq, d)
    xs = x_sample.reshape(dec_batch * dec_seq, d)
    kv_p, kv_s = (None,) * 3, (None,) * 3
    for l in range(depth):
        p = {"norm_mix_g": norm_mix_g[l], "lq1": lambda_q1[l], "lk1": lambda_k1[l],
             "lq2": lambda_q2[l], "lk2": lambda_k2[l], "subln_g": diff_subln_g[l],
             "ret_g": ret_norm_g[l], "cache_k": cache_k, "cache_v": cache_v, "state": state_ret}
        final_g = final_norm_g if l == depth - 1 else None
        xp, kv_p, w_in_next, w = _mixer(
            xp, l, None, (w_in_cur, 0), p, rope_p, False, kv_p, side, depth=depth, batch=batch,
            seq=seq, heads=heads, ret_heads=ret_heads)
        xs, kv_s, _, _ = _mixer(
            xs, l, w, (w_in_cur, 0), p, rope_s, True, kv_s, None, depth=depth, batch=dec_batch,
            seq=dec_seq, heads=heads, ret_heads=ret_heads)
        xp = _ffn(xp, norm_ffn_g[l], w["up"][None], w["down"][None], 0, final_g)
        xs = _ffn(xs, norm_ffn_g[l], w["up"][None], w["down"][None], 0, final_g)
        if w_in_next is not None:
            w_in_cur = w_in_next[None]
    return (xp.reshape(batch, seq, d), xs.reshape(dec_batch, dec_seq, d),
            kv_p[0].reshape(depth, batch, seq, heads, hd),
            kv_p[1].reshape(depth, batch, seq, heads, hd), kv_p[2],
            kv_s[0].reshape(depth, dec_batch, dec_seq, heads, hd),
            kv_s[1].reshape(depth, dec_batch, dec_seq, heads, hd), kv_s[2])
```
